```python
import math
import numpy as np
import jax
import jax.numpy as jnp
from jax import lax

D_MODEL = 1024
BATCH = 4
SEQ = 8192
DEPTH = 4

GRID_W = 64
CTX_LEN = 256

A_HEADS = 4
A_HD = 64
A_WIDTH = A_HEADS * 2 * A_HD
B_HEADS = 8
B_HD = 64
B_WIDTH = B_HEADS * B_HD
B_DECAY_RANK = 64
B_ICL_RANK = 64
B_GATE_RANK = 128
C_WIDTH = 512
N_BRANCH = 3
IN_SIZES = (A_WIDTH, A_WIDTH, A_WIDTH,
            B_WIDTH, B_WIDTH, B_WIDTH,
            B_DECAY_RANK, B_ICL_RANK, B_GATE_RANK,
            C_WIDTH, C_WIDTH, C_WIDTH,
            N_BRANCH * D_MODEL)
N_IN = 3 * A_WIDTH + 3 * B_WIDTH + B_DECAY_RANK + B_ICL_RANK + B_GATE_RANK + 3 * C_WIDTH + N_BRANCH * D_MODEL
N_EXPERTS = 16
EXPERT_FF = 1024
CAPACITY_FACTOR = 2

ROPE_THETA = 10000.0
Q_BLOCK = 128
NORM_EPS = 1e-6
GN_EPS = 64e-5

kernel_name = "hybrid_diffusion_trunk"


def _split(t, sizes):
    cuts = [int(v) for v in np.cumsum(sizes)[:-1]]
    return jnp.split(t, cuts, axis=-1)


def _rmsnorm(x, g):
    xf = x.astype(jnp.float32)
    y = xf * lax.rsqrt(jnp.mean(xf * xf, axis=-1, keepdims=True) + NORM_EPS)
    return (y * g.astype(jnp.float32)).astype(x.dtype)


def _adaln_params(cond, w, b):
    return jnp.split(jax.nn.silu(cond) @ w + b, 6, axis=-1)


def _dwconv3(x, w):
    xp = jnp.pad(x, ((0, 0), (1, 1), (0, 0)))
    return w[0] * xp[:, :-2] + w[1] * xp[:, 1:-1] + w[2] * xp[:, 2:]


def _axial_rope_tables(rows):
    half = A_HD // 2
    inv = jnp.power(ROPE_THETA, -jnp.arange(0, half, 2, dtype=jnp.float32) / half)
    r = jnp.repeat(jnp.arange(rows, dtype=jnp.float32), GRID_W)
    col = jnp.tile(jnp.arange(GRID_W, dtype=jnp.float32), rows)
    ang = jnp.concatenate([r[:, None] * inv, col[:, None] * inv], axis=-1)
    return jnp.cos(ang), jnp.sin(ang)


def _rope(x, cos, sin):
    half = A_HD // 2
    xf = x.astype(jnp.float32)
    x1, x2 = xf[..., :half], xf[..., half:]
    c = cos[None, :, None, None, :]
    s = sin[None, :, None, None, :]
    return jnp.concatenate([x1 * c - x2 * s, x2 * c + x1 * s], axis=-1).astype(x.dtype)


def _diff_softmax_attn(q, k, v, lam):
    s = jnp.einsum('bqhmd,bkhmd->bhmqk', q, k, preferred_element_type=jnp.float32) * (A_HD ** -0.5)
    p = jax.nn.softmax(s, axis=-1)
    w = p[:, :, 0] - lam * p[:, :, 1]
    return jnp.einsum('bhqk,bkhe->bqhe', w.astype(v.dtype), v)


def _branch_diff_attn(pc, pl, cos, sin, q_g, k_g, lam_vec, subln_g, lam_init, need_ctx):
    def q_heads(p):
        b, t, _ = p[0].shape
        return _rmsnorm(p[0].reshape(b, t, A_HEADS, 2, A_HD), q_g)

    def kv_heads(p):
        b, t, _ = p[1].shape
        k = _rmsnorm(p[1].reshape(b, t, A_HEADS, 2, A_HD), k_g)
        v = p[2].reshape(b, t, A_HEADS, 2 * A_HD)
        return k, v

    kc, vc = kv_heads(pc)
    kl, vl = kv_heads(pl)
    ql = _rope(q_heads(pl), cos, sin)
    kl = _rope(kl, cos, sin)
    lv = lam_vec.astype(jnp.float32)
    lam = jnp.exp(jnp.sum(lv[0] * lv[1])) - jnp.exp(jnp.sum(lv[2] * lv[3])) + lam_init

    k_all = jnp.concatenate([kc, kl], axis=1)
    v_all = jnp.concatenate([vc, vl], axis=1)
    b, t = ql.shape[:2]
    nblk = t // Q_BLOCK
    qb = jnp.moveaxis(ql.reshape(b, nblk, Q_BLOCK, A_HEADS, 2, A_HD), 1, 0)
    ob = lax.map(lambda qq: _diff_softmax_attn(qq, k_all, v_all, lam), qb)
    ol = jnp.moveaxis(ob, 0, 1).reshape(b, t, A_HEADS, 2 * A_HD)

    def post(o):
        return (_rmsnorm(o, subln_g) * (1.0 - lam_init)).reshape(o.shape[0], o.shape[1], A_WIDTH)

    yc = post(_diff_softmax_attn(q_heads(pc), kc, vc, lam)) if need_ctx else None
    return yc, post(ol)


def _bheads(t):
    return t.astype(jnp.float32).reshape(t.shape[0], t.shape[1], B_HEADS, B_HD)


def _rwkv_prep(p, conv_w, k_k):
    r, k, v = jnp.split(_dwconv3(jnp.concatenate(p[:3], axis=-1), conv_w), 3, axis=-1)
    kk = _bheads(k * k_k)
    kk = kk / jnp.maximum(jnp.sqrt(jnp.sum(kk * kk, axis=-1, keepdims=True)), 1e-12)
    return r, k, v, kk


def _rwkv_direction(k, wl, al, w0, w_up, a0, a_up, k_a):
    w = -jax.nn.softplus(-(w0 + jnp.tanh(wl) @ w_up)) - 0.5
    decay = jnp.exp(-jnp.exp(w.astype(jnp.float32)))
    a = jax.nn.sigmoid(a0 + al @ a_up)
    kd = k * (1.0 + (a - 1.0) * k_a)
    return _bheads(decay), _bheads(a), _bheads(kd)


def _rwkv_scan(s0, decay, kk, a, k, v, r, reverse):
    emit = r is not None
    xs = (decay, kk, a, k, v) + ((r,) if emit else ())
    xs = tuple(jnp.moveaxis(t, 1, 0) for t in xs)

    def step(s, inp):
        wt, kkt, at, kt, vt = inp[:5]
        sa = jnp.einsum('bhvk,bhk->bhv', s, kkt)
        s = s * wt[:, :, None, :] - sa[..., None] * (kkt * at)[:, :, None, :] + vt[..., None] * kt[:, :, None, :]
        y = jnp.einsum('bhvk,bhk->bhv', s, inp[5]) if emit else None
        return s, y

    s_fin, ys = lax.scan(step, s0, xs, reverse=reverse)
    return s_fin, (jnp.moveaxis(ys, 0, 1) if emit else None)


def _rwkv_readout(o, r, kd, v, r_k, ln_g, ln_b):
    mu = jnp.mean(o, axis=-1, keepdims=True)
    dlt = o - mu
    on = dlt * lax.rsqrt(jnp.mean(dlt * dlt, axis=-1, keepdims=True) + GN_EPS)
    bonus = jnp.sum(r * kd * r_k.astype(jnp.float32), axis=-1, keepdims=True) * v
    y = on * ln_g.astype(jnp.float32).reshape(B_HEADS, B_HD) + ln_b.astype(jnp.float32).reshape(B_HEADS, B_HD) + bonus
    return y.reshape(o.shape[0], o.shape[1], B_WIDTH)


def _branch_rwkv(pc, pl, conv_w, w0, w_up, a0, a_up, g_up, k_k, k_a, r_k, ln_g, ln_b, need_ctx):
    rc, kc, vc, kkc = _rwkv_prep(pc, conv_w, k_k)
    rl, kl, vl, kkl = _rwkv_prep(pl, conv_w, k_k)
    vc_h = _bheads(vc)
    rl_h, vl_h = _bheads(rl), _bheads(vl)
    rc_h = _bheads(rc) if need_ctx else None
    s0 = jnp.zeros((pl[0].shape[0], B_HEADS, B_HD, B_HD), jnp.float32)
    y_l = 0.0
    y_c = 0.0
    for d in range(2):
        rev = d == 1
        dec_c, a_c, kd_c = _rwkv_direction(kc, pc[3], pc[4], w0[d], w_up[d], a0[d], a_up[d], k_a)
        s_c, o_c = _rwkv_scan(s0, dec_c, kkc, a_c, kd_c, vc_h, rc_h, rev)
        dec_l, a_l, kd_l = _rwkv_direction(kl, pl[3], pl[4], w0[d], w_up[d], a0[d], a_up[d], k_a)
        _, o_l = _rwkv_scan(s_c, dec_l, kkl, a_l, kd_l, vl_h, rl_h, rev)
        y_l = y_l + _rwkv_readout(o_l, rl_h, kd_l, vl_h, r_k, ln_g, ln_b)
        if need_ctx:
            y_c = y_c + _rwkv_readout(o_c, rc_h, kd_c, vc_h, r_k, ln_g, ln_b)
    out_l = (y_l * (jax.nn.sigmoid(pl[5]) @ g_up)).astype(pl[0].dtype)
    out_c = (y_c * (jax.nn.sigmoid(pc[5]) @ g_up)).astype(pc[0].dtype) if need_ctx else None
    return out_c, out_l


def _branch_conv(p, w):
    bg, cg, xv = p
    return bg * _dwconv3(cg * xv, w)


def _merge(ys, gates, w_branch, w_out):
    gs = jnp.split(jax.nn.sigmoid(gates), N_BRANCH, axis=-1)
    m = gs[0] * (ys[0] @ w_branch[0]) + gs[1] * (ys[1] @ w_branch[1]) + gs[2] * (ys[2] @ w_branch[2])
    return m @ w_out


def _mixer(hc, hl, lp, lam_init, cos, sin, need_ctx):
    pc = _split(hc @ lp['w_in'], IN_SIZES)
    pl = _split(hl @ lp['w_in'], IN_SIZES)
    ac, al = _branch_diff_attn(pc[0:3], pl[0:3], cos, sin, lp['q_norm_g'], lp['k_norm_g'],
                               lp['diff_lambda'], lp['diff_subln_g'], lam_init, need_ctx)
    bc, bl = _branch_rwkv(pc[3:9], pl[3:9], lp['rwkv_conv_w'], lp['rwkv_w0'], lp['rwkv_w_up'],
                          lp['rwkv_a0'], lp['rwkv_a_up'], lp['rwkv_g_up'], lp['rwkv_k_k'],
                          lp['rwkv_k_a'], lp['rwkv_r_k'], lp['rwkv_ln_g'], lp['rwkv_ln_b'], need_ctx)
    cl = _branch_conv(pl[9:12], lp['conv_w'])
    yl = _merge((al, bl, cl), pl[12], lp['w_branch'], lp['w_out'])
    yc = None
    if need_ctx:
        cc = _branch_conv(pc[9:12], lp['conv_w'])
        yc = _merge((ac, bc, cc), pc[12], lp['w_branch'], lp['w_out'])
    return yc, yl


def _expert_choice(h, router_w, w1, w3, w2):
    b, n, dm = h.shape
    cap = CAPACITY_FACTOR * n // N_EXPERTS
    aff = jax.nn.softmax((h @ router_w).astype(jnp.float32), axis=-1)
    gate, idx = lax.top_k(jnp.swapaxes(aff, 1, 2), cap)
    xin = jax.vmap(lambda hb, ib: hb[ib])(h, idx)
    hid = jax.nn.silu(jnp.einsum('becd,edf->becf', xin, w1)) * jnp.einsum('becd,edf->becf', xin, w3)
    out = jnp.einsum('becf,efd->becd', hid, w2) * gate[..., None].astype(h.dtype)
    return jax.vmap(lambda ob, ib: jnp.zeros((n, dm), ob.dtype).at[ib.reshape(-1)].add(ob.reshape(-1, dm)))(out, idx)


def _layer(xc, xl, c, c_ctx, lp, lam_init, cos, sin, last):
    need_ctx = not last
    sh1l, sc1l, g1l, sh2l, sc2l, g2l = [m[:, None, :] for m in _adaln_params(c, lp['ada_w'], lp['ada_b'])]
    sh1c, sc1c, g1c, sh2c, sc2c, g2c = _adaln_params(c_ctx, lp['ada_w'], lp['ada_b'])
    hl = _rmsnorm(xl, lp['norm1_g']) * (1.0 + sc1l) + sh1l
    hc = _rmsnorm(xc, lp['norm1_g']) * (1.0 + sc1c) + sh1c
    yc, yl = _mixer(hc, hl, lp, lam_init, cos, sin, need_ctx)
    xl = xl + g1l * yl
    hl = _rmsnorm(xl, lp['norm2_g']) * (1.0 + sc2l) + sh2l
    xl = xl + g2l * _expert_choice(hl, lp['router_w'], lp['exp_w1'], lp['exp_w3'], lp['exp_w2'])
    if need_ctx:
        xc = xc + g1c * yc
        hc = _rmsnorm(xc, lp['norm2_g']) * (1.0 + sc2c) + sh2c
        xc = xc + g2c * _expert_choice(hc, lp['router_w'], lp['exp_w1'], lp['exp_w3'], lp['exp_w2'])
    return xc, xl


def setup_inputs(seed: int = 0) -> dict:
    key = jax.random.key(seed)
    ks = list(jax.random.split(key, 40))
    f32 = jnp.float32

    def nrm(shape, scale):
        return jax.random.normal(ks.pop(), shape, f32) * scale

    D = D_MODEL
    return {
        "x": nrm((BATCH, SEQ, D), 1.0),
        "c": nrm((BATCH, D), 1.0),
        "ctx": nrm((BATCH, CTX_LEN, D), 1.0),
        "c_ctx": nrm((D,), 1.0),
        "norm1_g": 1.0 + nrm((DEPTH, D), 0.05),
        "norm2_g": 1.0 + nrm((DEPTH, D), 0.05),
        "ada_w": nrm((DEPTH, D, 6 * D), 0.3 * D ** -0.5),
        "ada_b": nrm((DEPTH, 6 * D), 0.01),
        "w_in": nrm((DEPTH, D, N_IN), D ** -0.5),
        "q_norm_g": 1.0 + nrm((DEPTH, A_HD), 0.05),
        "k_norm_g": 1.0 + nrm((DEPTH, A_HD), 0.05),
        "diff_lambda": nrm((DEPTH, 4, A_HD), 0.1),
        "diff_subln_g": 1.0 + nrm((DEPTH, 2 * A_HD), 0.05),
        "rwkv_conv_w": jnp.array([0.2, 0.6, 0.2], f32)[None, :, None] + nrm((DEPTH, 3, 3 * B_WIDTH), 0.05),
        "rwkv_w0": jnp.linspace(-6.0, -1.0, B_WIDTH, dtype=f32)[None, None, :] + nrm((DEPTH, 2, B_WIDTH), 0.3),
        "rwkv_w_up": nrm((DEPTH, 2, B_DECAY_RANK, B_WIDTH), 0.1 * B_DECAY_RANK ** -0.5),
        "rwkv_a0": nrm((DEPTH, 2, B_WIDTH), 0.1),
        "rwkv_a_up": nrm((DEPTH, 2, B_ICL_RANK, B_WIDTH), 0.3 * B_ICL_RANK ** -0.5),
        "rwkv_g_up": nrm((DEPTH, B_GATE_RANK, B_WIDTH), B_GATE_RANK ** -0.5),
        "rwkv_k_k": 0.85 + nrm((DEPTH, B_WIDTH), 0.05),
        "rwkv_k_a": 1.0 + nrm((DEPTH, B_WIDTH), 0.05),
        "rwkv_r_k": nrm((DEPTH, B_HEADS, B_HD), 0.1),
        "rwkv_ln_g": 1.0 + nrm((DEPTH, B_WIDTH), 0.05),
        "rwkv_ln_b": nrm((DEPTH, B_WIDTH), 0.01),
        "conv_w": nrm((DEPTH, 3, C_WIDTH), 3 ** -0.5),
        "w_branch": nrm((DEPTH, N_BRANCH, A_WIDTH, D), A_WIDTH ** -0.5),
        "w_out": nrm((DEPTH, D, D), D ** -0.5),
        "router_w": nrm((DEPTH, D, N_EXPERTS), D ** -0.5),
        "exp_w1": nrm((DEPTH, N_EXPERTS, D, EXPERT_FF), D ** -0.5),
        "exp_w3": nrm((DEPTH, N_EXPERTS, D, EXPERT_FF), D ** -0.5),
        "exp_w2": nrm((DEPTH, N_EXPERTS, EXPERT_FF, D), EXPERT_FF ** -0.5),
    }


def reference(x, c, ctx, c_ctx, norm1_g, norm2_g, ada_w, ada_b, w_in, q_norm_g, k_norm_g,
              diff_lambda, diff_subln_g, rwkv_conv_w, rwkv_w0, rwkv_w_up, rwkv_a0, rwkv_a_up,
              rwkv_g_up, rwkv_k_k, rwkv_k_a, rwkv_r_k, rwkv_ln_g, rwkv_ln_b, conv_w, w_branch,
              w_out, router_w, exp_w1, exp_w3, exp_w2):
    n_lat = x.shape[1]
    rows = n_lat // GRID_W
    cos, sin = _axial_rope_tables(rows)
    xc, xl = ctx, x
    for i in range(DEPTH):
        lp = {
            'norm1_g': norm1_g[i], 'norm2_g': norm2_g[i], 'ada_w': ada_w[i], 'ada_b': ada_b[i],
            'w_in': w_in[i], 'q_norm_g': q_norm_g[i], 'k_norm_g': k_norm_g[i],
            'diff_lambda': diff_lambda[i], 'diff_subln_g': diff_subln_g[i],
            'rwkv_conv_w': rwkv_conv_w[i], 'rwkv_w0': rwkv_w0[i], 'rwkv_w_up': rwkv_w_up[i],
            'rwkv_a0': rwkv_a0[i], 'rwkv_a_up': rwkv_a_up[i], 'rwkv_g_up': rwkv_g_up[i],
            'rwkv_k_k': rwkv_k_k[i], 'rwkv_k_a': rwkv_k_a[i], 'rwkv_r_k': rwkv_r_k[i],
            'rwkv_ln_g': rwkv_ln_g[i], 'rwkv_ln_b': rwkv_ln_b[i], 'conv_w': conv_w[i],
            'w_branch': w_branch[i], 'w_out': w_out[i], 'router_w': router_w[i],
            'exp_w1': exp_w1[i], 'exp_w3': exp_w3[i], 'exp_w2': exp_w2[i],
        }
        lam_init = 0.8 - 0.6 * math.exp(-0.3 * i)
        xc, xl = _layer(xc, xl, c, c_ctx, lp, lam_init, cos, sin, i == DEPTH - 1)
    return xl
```

```python
import functools
import math

import numpy as np
import jax
import jax.numpy as jnp
from jax import lax
from jax.experimental import pallas as pl
from jax.experimental.pallas import tpu as pltpu

F32 = jnp.float32
BF16 = jnp.bfloat16
HIGHEST = lax.Precision.HIGHEST

D_MODEL = 1024
GRID_W = 64
A_HEADS = 4
A_HD = 64
A_WIDTH = A_HEADS * 2 * A_HD
B_HEADS = 8
B_HD = 64
B_WIDTH = B_HEADS * B_HD
B_DECAY_RANK = 64
B_ICL_RANK = 64
B_GATE_RANK = 128
C_WIDTH = 512
N_BRANCH = 3
N_EXPERTS = 16
EXPERT_FF = 1024
CAPACITY_FACTOR = 2
ROPE_THETA = 10000.0
NORM_EPS = 1e-6
GN_EPS = 64e-5

COL_ATTN = 0
COL_RWKV = 1536
COL_CONV = 3072
COL_LOWRANK = 4608
COL_GATES = 5120
N_IN_PAD = 8192

VMEM_LIMIT = 56 * 1024 * 1024
LANES = 128
SUBLANES = 8
SLOT_BLOCK = 256


def _cparams(sem):
    return pltpu.CompilerParams(dimension_semantics=sem, vmem_limit_bytes=VMEM_LIMIT)


def _pick(n, pref):
    t = min(n, pref)
    while n % t:
        t -= SUBLANES
    return t


def _group_ones(width, group):
    idx = np.arange(width) // group
    return jnp.asarray((idx[:, None] == idx[None, :]).astype(np.float32))


def _silu(x):
    return x * jax.nn.sigmoid(x)


def _ada_kernel(cond_ref, w_ref, b_ref, o_ref):
    c = cond_ref[...]
    o_ref[0] = jnp.dot(_silu(c), w_ref[0], precision=HIGHEST, preferred_element_type=F32) + b_ref[0]


def _ada_params(cond, ada_w, ada_b):
    depth, d, n6 = ada_w.shape
    tn = _pick(n6, 1536)
    return pl.pallas_call(
        _ada_kernel,
        out_shape=jax.ShapeDtypeStruct((depth, cond.shape[0], n6), F32),
        grid=(depth, n6 // tn),
        in_specs=[pl.BlockSpec(cond.shape, lambda i, j: (0, 0)),
                  pl.BlockSpec((1, d, tn), lambda i, j: (i, 0, j)),
                  pl.BlockSpec((1, 1, tn), lambda i, j: (i, 0, j))],
        out_specs=pl.BlockSpec((1, cond.shape[0], tn), lambda i, j: (i, 0, j)),
        compiler_params=_cparams(("parallel", "parallel")),
        name="ada_params",
    )(cond, ada_w, ada_b.reshape(depth, 1, n6))


def _proj_kernel(x_ref, g_ref, sc_ref, sh_ref, w_ref, o_ref):
    x = x_ref[0]
    ms = jnp.mean(x * x, axis=-1, keepdims=True)
    y = x * lax.rsqrt(ms + NORM_EPS) * g_ref[...]
    h = y * (1.0 + sc_ref[0]) + sh_ref[0]
    o_ref[0] = jnp.dot(h.astype(BF16), w_ref[...], preferred_element_type=F32)


def _norm_mod_proj(x, g, sc, sh, w_bf16):
    b, n, d = x.shape
    npad = w_bf16.shape[1]
    tm = _pick(n, 512)
    tn = 1024
    return pl.pallas_call(
        _proj_kernel,
        out_shape=jax.ShapeDtypeStruct((b, n, npad), F32),
        grid=(npad // tn, b, n // tm),
        in_specs=[pl.BlockSpec((1, tm, d), lambda j, bi, i: (bi, i, 0)),
                  pl.BlockSpec((1, d), lambda j, bi, i: (0, 0)),
                  pl.BlockSpec((1, 1, d), lambda j, bi, i: (bi, 0, 0)),
                  pl.BlockSpec((1, 1, d), lambda j, bi, i: (bi, 0, 0)),
                  pl.BlockSpec((d, tn), lambda j, bi, i: (0, j))],
        out_specs=pl.BlockSpec((1, tm, tn), lambda j, bi, i: (bi, i, j)),
        compiler_params=_cparams(("parallel", "parallel", "parallel")),
        name="norm_mod_proj",
    )(x, g.reshape(1, d), sc, sh, w_bf16)


def _swap_halves(x, half):
    width = x.shape[-1]
    lane = lax.broadcasted_iota(jnp.int32, (1, width), 1) % (2 * half)
    fwd = pltpu.roll(x, width - half, axis=1)
    bwd = pltpu.roll(x, half, axis=1)
    return jnp.where(lane < half, fwd, bwd)


def _qkv_prep_kernel(*refs, use_rope):
    if use_rope:
        q_ref, k_ref, v_ref, qg_ref, kg_ref, ones_ref, cos_ref, sin_ref, qo_ref, ko_ref, vo_ref = refs
    else:
        q_ref, k_ref, v_ref, qg_ref, kg_ref, ones_ref, qo_ref, ko_ref, vo_ref = refs

    def head_norm(x, gain):
        ms = jnp.dot(x * x, ones_ref[...], precision=HIGHEST, preferred_element_type=F32) * (1.0 / A_HD)
        return x * lax.rsqrt(ms + NORM_EPS) * gain

    def rope(x):
        if not use_rope:
            return x
        return x * cos_ref[...] + _swap_halves(x, A_HD // 2) * sin_ref[...]

    q = rope(head_norm(q_ref[0], qg_ref[...]))
    k = rope(head_norm(k_ref[0], kg_ref[...]))
    qo_ref[0] = (q * (A_HD ** -0.5)).astype(BF16)
    ko_ref[0] = k.astype(BF16)
    vo_ref[0] = v_ref[0].astype(BF16)


def _qkv_prep(p, qg, kg, ones, cos_t, sin_t):
    b, n, _ = p.shape
    tm = _pick(n, 512)
    use_rope = cos_t is not None
    w = A_WIDTH
    col = lambda c: pl.BlockSpec((1, tm, w), lambda bi, i, c=c: (bi, i, COL_ATTN // w + c))
    const = lambda shape: pl.BlockSpec(shape, lambda bi, i: (0,) * len(shape))
    in_specs = [col(0), col(1), col(2), const((1, w)), const((1, w)), const((w, w))]
    args = [p, p, p, qg, kg, ones]
    if use_rope:
        in_specs += [pl.BlockSpec((tm, w), lambda bi, i: (i, 0))] * 2
        args += [cos_t, sin_t]
    out = jax.ShapeDtypeStruct((b, n, w), BF16)
    return pl.pallas_call(
        functools.partial(_qkv_prep_kernel, use_rope=use_rope),
        out_shape=(out, out, out),
        grid=(b, n // tm),
        in_specs=in_specs,
        out_specs=[pl.BlockSpec((1, tm, w), lambda bi, i: (bi, i, 0))] * 3,
        compiler_params=_cparams(("parallel", "parallel")),
        name="qkv_prep_rope" if use_rope else "qkv_prep",
    )(*args)


def _map_masked(q):
    lane = lax.broadcasted_iota(jnp.int32, (1, 2 * A_HD), 1)
    zero = jnp.zeros_like(q)
    return (jnp.where(lane < A_HD, q, zero), jnp.where(lane >= A_HD, q, zero))


def _lambda_full(dl_ref, lam_init):
    dl = dl_ref[...]
    s1 = jnp.sum(dl[0:1] * dl[1:2], axis=-1, keepdims=True)
    s2 = jnp.sum(dl[2:3] * dl[3:4], axis=-1, keepdims=True)
    return jnp.exp(s1) - jnp.exp(s2) + lam_init


def _subln(o, sg_ref, lam_init):
    ms = jnp.mean(o * o, axis=-1, keepdims=True)
    return o * lax.rsqrt(ms + NORM_EPS) * sg_ref[...] * (1.0 - lam_init)


def _scores(qm, k):
    return lax.dot_general(qm, k, (((1,), (1,)), ((), ())), preferred_element_type=F32)


def _flash_kernel(dl_ref, sg_ref, q_ref, kc_ref, vc_ref, k_ref, v_ref, o_ref, m_s, l_s, acc_s, *, lam_init, nkv):
    kj = pl.program_id(3)
    qm = _map_masked(q_ref[0])

    def update(k, v):
        for mi in range(2):
            s = _scores(qm[mi], k)
            m_prev = m_s[mi]
            m_new = jnp.maximum(m_prev, jnp.max(s, axis=-1, keepdims=True))
            alpha = jnp.exp(m_prev - m_new)
            p = jnp.exp(s - m_new)
            l_s[mi] = alpha * l_s[mi] + jnp.sum(p, axis=-1, keepdims=True)
            acc_s[mi] = alpha * acc_s[mi] + jnp.dot(p.astype(BF16), v, preferred_element_type=F32)
            m_s[mi] = m_new

    @pl.when(kj == 0)
    def _():
        m_s[...] = jnp.full(m_s.shape, -jnp.inf, F32)
        l_s[...] = jnp.zeros(l_s.shape, F32)
        acc_s[...] = jnp.zeros(acc_s.shape, F32)
        update(kc_ref[0], vc_ref[0])

    update(k_ref[0], v_ref[0])

    @pl.when(kj == nkv - 1)
    def _():
        lam = _lambda_full(dl_ref, lam_init)
        o = acc_s[0] / l_s[0] - lam * (acc_s[1] / l_s[1])
        o_ref[0] = _subln(o, sg_ref, lam_init)


def _flash_attention(dl, sg, q, kc, vc, kl, vl, lam_init):
    b, s, _ = q.shape
    nc = kc.shape[1]
    hw = 2 * A_HD
    tq = _pick(s, 512)
    tk = _pick(s, 1024)
    nkv = s // tk
    return pl.pallas_call(
        functools.partial(_flash_kernel, lam_init=lam_init, nkv=nkv),
        out_shape=jax.ShapeDtypeStruct((b, s, A_WIDTH), F32),
        grid=(b, A_HEADS, s // tq, nkv),
        in_specs=[pl.BlockSpec((4, A_HD), lambda bi, h, i, j: (0, 0)),
                  pl.BlockSpec((1, hw), lambda bi, h, i, j: (0, 0)),
                  pl.BlockSpec((1, tq, hw), lambda bi, h, i, j: (bi, i, h)),
                  pl.BlockSpec((1, nc, hw), lambda bi, h, i, j: (bi, 0, h)),
                  pl.BlockSpec((1, nc, hw), lambda bi, h, i, j: (bi, 0, h)),
                  pl.BlockSpec((1, tk, hw), lambda bi, h, i, j: (bi, j, h)),
                  pl.BlockSpec((1, tk, hw), lambda bi, h, i, j: (bi, j, h))],
        out_specs=pl.BlockSpec((1, tq, hw), lambda bi, h, i, j: (bi, i, h)),
        scratch_shapes=[pltpu.VMEM((2, tq, 1), F32), pltpu.VMEM((2, tq, 1), F32), pltpu.VMEM((2, tq, hw), F32)],
        compiler_params=_cparams(("parallel", "parallel", "parallel", "arbitrary")),
        name="diff_flash_attention",
    )(dl, sg, q, kc, vc, kl, vl)


def _ctx_attn_kernel(dl_ref, sg_ref, q_ref, k_ref, v_ref, o_ref, *, lam_init):
    qm = _map_masked(q_ref[0])
    outs = []
    for mi in range(2):
        s = _scores(qm[mi], k_ref[0])
        p = jnp.exp(s - jnp.max(s, axis=-1, keepdims=True))
        o = jnp.dot(p.astype(BF16), v_ref[0], preferred_element_type=F32)
        outs.append(o / jnp.sum(p, axis=-1, keepdims=True))
    lam = _lambda_full(dl_ref, lam_init)
    o_ref[0] = _subln(outs[0] - lam * outs[1], sg_ref, lam_init)


def _ctx_attention(dl, sg, q, k, v, lam_init):
    b, n, _ = q.shape
    hw = 2 * A_HD
    blk = pl.BlockSpec((1, n, hw), lambda bi, h: (bi, 0, h))
    return pl.pallas_call(
        functools.partial(_ctx_attn_kernel, lam_init=lam_init),
        out_shape=jax.ShapeDtypeStruct((b, n, A_WIDTH), F32),
        grid=(b, A_HEADS),
        in_specs=[pl.BlockSpec((4, A_HD), lambda bi, h: (0, 0)), pl.BlockSpec((1, hw), lambda bi, h: (0, 0)),
                  blk, blk, blk],
        out_specs=blk,
        compiler_params=_cparams(("parallel", "parallel")),
        name="ctx_attention",
    )(dl, sg, q, k, v)


def _halo_specs(tm, width, colblk, n):
    r8 = tm // SUBLANES
    last = n // SUBLANES - 1
    main = pl.BlockSpec((1, tm, width), lambda bi, i: (bi, i, colblk))
    prev = pl.BlockSpec((1, SUBLANES, width), lambda bi, i: (bi, jnp.maximum(i * r8 - 1, 0), colblk))
    nxt = pl.BlockSpec((1, SUBLANES, width), lambda bi, i: (bi, jnp.minimum((i + 1) * r8, last), colblk))
    return [main, prev, nxt]


def _dwconv3(x, prev_row, next_row, w_ref):
    tm = x.shape[0]
    row = lax.broadcasted_iota(jnp.int32, (tm, 1), 0)
    xp = jnp.where(row == 0, prev_row, pltpu.roll(x, 1, axis=0))
    xn = jnp.where(row == tm - 1, next_row, pltpu.roll(x, tm - 1, axis=0))
    w = w_ref[...]
    return w[0:1] * xp + w[1:2] * x + w[2:3] * xn


def _edge_rows(prev_ref, next_ref):
    i = pl.program_id(1)
    n = pl.num_programs(1)
    pr = prev_ref[0][SUBLANES - 1:SUBLANES]
    nr = next_ref[0][0:1]
    pr = jnp.where(i == 0, jnp.zeros_like(pr), pr)
    nr = jnp.where(i == n - 1, jnp.zeros_like(nr), nr)
    return pr, nr


def _softplus(u):
    return jnp.maximum(u, 0.0) + jnp.log(1.0 + jnp.exp(-jnp.abs(u)))


def _rwkv_prep_kernel(rkv_ref, prev_ref, next_ref, lr_ref, cw_ref, w0_ref, wup_ref, a0_ref, aup_ref, gup_ref,
                      kk_ref_w, ka_ref, rk_ref, ones_ref,
                      r_o, v_o, kk_o, g_o, bonus_o, dec0_o, b0_o, kd0_o, dec1_o, b1_o, kd1_o):
    pr, nr = _edge_rows(prev_ref, next_ref)
    rkv = _dwconv3(rkv_ref[0], pr, nr, cw_ref)
    w = B_WIDTH
    r, k, v = rkv[:, 0:w], rkv[:, w:2 * w], rkv[:, 2 * w:3 * w]
    ones = ones_ref[...]
    gsum = lambda t: jnp.dot(t, ones, precision=HIGHEST, preferred_element_type=F32)
    kkr = k * kk_ref_w[...]
    nrm = jnp.maximum(jnp.sqrt(gsum(kkr * kkr)), 1e-12)
    kk = kkr / nrm
    lr = lr_ref[0]
    th = jnp.tanh(lr)
    sg = jax.nn.sigmoid(lr)
    r_o[0] = r
    v_o[0] = v
    kk_o[0] = kk
    g_o[0] = jnp.dot(sg, gup_ref[...], precision=HIGHEST, preferred_element_type=F32)
    kd_sum = jnp.zeros_like(k)
    outs = ((dec0_o, b0_o, kd0_o), (dec1_o, b1_o, kd1_o))
    for d in range(2):
        z = w0_ref[d:d + 1] + jnp.dot(th, wup_ref[d], precision=HIGHEST, preferred_element_type=F32)
        wlog = -_softplus(-z) - 0.5
        dec = jnp.exp(-jnp.exp(wlog))
        a = jax.nn.sigmoid(a0_ref[d:d + 1] + jnp.dot(lr, aup_ref[d], precision=HIGHEST, preferred_element_type=F32))
        kd = k * (1.0 + (a - 1.0) * ka_ref[...])
        outs[d][0][0] = dec
        outs[d][1][0] = kk * a
        outs[d][2][0] = kd
        kd_sum = kd_sum + kd
    bonus_o[0] = gsum(r * kd_sum * rk_ref[...]) * v


def _rwkv_prep(p, lw):
    b, n, _ = p.shape
    tm = _pick(n, 256)
    w = B_WIDTH
    w3 = 3 * w
    const = lambda shape: pl.BlockSpec(shape, lambda bi, i: (0,) * len(shape))
    in_specs = _halo_specs(tm, w3, COL_RWKV // w3, n) + [
        pl.BlockSpec((1, tm, 256), lambda bi, i: (bi, i, COL_LOWRANK // 256)),
        const((3, w3)), const((2, w)), const((2, 256, w)), const((2, w)), const((2, 256, w)), const((256, w)),
        const((1, w)), const((1, w)), const((1, w)), const((w, w))]
    out = jax.ShapeDtypeStruct((b, n, w), F32)
    return pl.pallas_call(
        _rwkv_prep_kernel,
        out_shape=(out,) * 11,
        grid=(b, n // tm),
        in_specs=in_specs,
        out_specs=[pl.BlockSpec((1, tm, w), lambda bi, i: (bi, i, 0))] * 11,
        compiler_params=_cparams(("parallel", "parallel")),
        name="rwkv_prep",
    )(p, p, p, p, lw['rwkv_conv_w'], lw['rwkv_w0'], lw['w_up_pad'], lw['rwkv_a0'], lw['a_up_pad'], lw['g_up_pad'],
      lw['rwkv_k_k'], lw['rwkv_k_a'], lw['rwkv_r_k'], lw['ones64'])


V_SLABS = B_HD // 2 // SUBLANES


def _scan_kernel(w_ref, b_ref, kd_ref, r_ref, kkn_ref, v_ref, y_ref, s_ref, sa_ref, *, tc):
    @pl.when(pl.program_id(0) == 0)
    def _():
        s_ref[...] = jnp.zeros(s_ref.shape, F32)
        sa_ref[...] = jnp.zeros(sa_ref.shape, F32)

    lanes = s_ref.shape[-1]

    def step(t, carry):
        sa = [sa_ref[s] for s in range(V_SLABS)]
        vv = [v_ref[t, pl.ds(SUBLANES * s, SUBLANES), :] for s in range(V_SLABS)]
        y = [jnp.zeros((SUBLANES, lanes), F32) for _ in range(V_SLABS)]
        san = [jnp.zeros((SUBLANES, lanes), F32) for _ in range(V_SLABS)]
        for k in range(B_HD):
            row = lambda ref: jnp.broadcast_to(ref[t, pl.ds(k, 1), :], (SUBLANES, lanes))
            wk, bk, kdk, rk, kkn = row(w_ref), row(b_ref), row(kd_ref), row(r_ref), row(kkn_ref)
            for s in range(V_SLABS):
                new = s_ref[s, k] * wk - sa[s] * bk + vv[s] * kdk
                s_ref[s, k] = new
                y[s] = y[s] + new * rk
                san[s] = san[s] + new * kkn
        for s in range(V_SLABS):
            y_ref[t, pl.ds(SUBLANES * s, SUBLANES), :] = y[s]
            sa_ref[s] = san[s]
        return carry

    lax.fori_loop(0, tc, step, 0)


def _rwkv_scan(w, bb, kd, r, kkn, v):
    t, _, lanes = w.shape
    tc = _pick(t, 64)
    kspec = pl.BlockSpec((tc, B_HD, lanes), lambda i: (i, 0, 0))
    vspec = pl.BlockSpec((tc, B_HD // 2, lanes), lambda i: (i, 0, 0))
    return pl.pallas_call(
        functools.partial(_scan_kernel, tc=tc),
        out_shape=jax.ShapeDtypeStruct(v.shape, F32),
        grid=(t // tc,),
        in_specs=[kspec] * 5 + [vspec],
        out_specs=vspec,
        scratch_shapes=[pltpu.VMEM((V_SLABS, B_HD, SUBLANES, lanes), F32), pltpu.VMEM((V_SLABS, SUBLANES, lanes), F32)],
        compiler_params=_cparams(("arbitrary",)),
        name="rwkv_scan",
    )(w, bb, kd, r, kkn, v)


def _seg_flip(x, nc):
    return jnp.concatenate([x[:, :nc][:, ::-1], x[:, nc:][:, ::-1]], axis=1)


def _rwkv_scan_layout(prep_c, prep_l):
    nc = prep_c['r'].shape[1]
    cat = lambda name: jnp.concatenate([prep_c[name], prep_l[name]], axis=1)
    bsz, t, _ = cat('r').shape

    def both_dirs(fwd, bwd):
        return jnp.stack([fwd, _seg_flip(bwd, nc)], axis=0)

    def k_operand(x2):
        x = x2.reshape(2, bsz, t, B_HEADS, B_HD)
        x = jnp.transpose(x, (2, 4, 0, 1, 3)).reshape(t, B_HD, 2 * bsz * B_HEADS)
        return jnp.concatenate([x, x], axis=-1)

    def v_operand(x2):
        x = x2.reshape(2, bsz, t, B_HEADS, 2, B_HD // 2)
        return jnp.transpose(x, (2, 5, 4, 0, 1, 3)).reshape(t, B_HD // 2, 4 * bsz * B_HEADS)

    shared = lambda name: both_dirs(cat(name), cat(name))
    w = k_operand(both_dirs(cat('dec0'), cat('dec1')))
    bb = k_operand(both_dirs(cat('b0'), cat('b1')))
    kd = k_operand(both_dirs(cat('kd0'), cat('kd1')))
    r = k_operand(shared('r'))
    kk = k_operand(shared('kk'))
    kkn = jnp.concatenate([kk[1:], jnp.zeros_like(kk[:1])], axis=0)
    v = v_operand(shared('v'))
    return (w, bb, kd, r, kkn, v), (bsz, t, nc)


def _rwkv_unscan(y, dims):
    bsz, t, nc = dims
    y = y.reshape(t, B_HD // 2, 2, 2, bsz, B_HEADS)
    y = jnp.transpose(y, (3, 4, 0, 5, 2, 1)).reshape(2, bsz, t, B_WIDTH)
    o0, o1 = y[0], _seg_flip(y[1], nc)
    return (o0[:, :nc], o1[:, :nc]), (o0[:, nc:], o1[:, nc:])


def _merge_kernel(a_ref, o0_ref, o1_ref, bonus_ref, g_ref, cv_ref, cprev_ref, cnext_ref, ga_ref, gb_ref, gc_ref,
                  x_ref, lng_ref, lnb_ref, cw_ref, wb_ref, wo_ref, ones_ref, g1_ref, n2_ref, sc_ref, sh_ref,
                  x_o, hf_o, hb_o):
    ones = ones_ref[...]
    gmean = lambda t: jnp.dot(t, ones, precision=HIGHEST, preferred_element_type=F32) * (1.0 / B_HD)

    def group_norm(o):
        dlt = o - gmean(o)
        return dlt * lax.rsqrt(gmean(dlt * dlt) + GN_EPS) * lng_ref[...] + lnb_ref[...]

    yb = (group_norm(o0_ref[0]) + group_norm(o1_ref[0]) + bonus_ref[0]) * g_ref[0]

    w = C_WIDTH
    u = lambda blk: blk[:, w:2 * w] * blk[:, 2 * w:3 * w]
    cv = cv_ref[0]
    i = pl.program_id(1)
    pr = u(cprev_ref[0][SUBLANES - 1:SUBLANES])
    nr = u(cnext_ref[0][0:1])
    pr = jnp.where(i == 0, jnp.zeros_like(pr), pr)
    nr = jnp.where(i == pl.num_programs(1) - 1, jnp.zeros_like(nr), nr)
    yc = cv[:, 0:w] * _dwconv3(u(cv), pr, nr, cw_ref)

    d = D_MODEL
    m = jnp.zeros((cv.shape[0], d), F32)
    for j, (yj, gate_ref) in enumerate(((a_ref[0], ga_ref), (yb, gb_ref), (yc, gc_ref))):
        m = m + jax.nn.sigmoid(gate_ref[0]) * jnp.dot(yj.astype(BF16), wb_ref[j], preferred_element_type=F32)
    ymix = jnp.dot(m.astype(BF16), wo_ref[...], preferred_element_type=F32)
    x1 = x_ref[0] + g1_ref[0] * ymix
    x_o[0] = x1
    ms = jnp.mean(x1 * x1, axis=-1, keepdims=True)
    h = x1 * lax.rsqrt(ms + NORM_EPS) * n2_ref[...] * (1.0 + sc_ref[0]) + sh_ref[0]
    hf_o[0] = h
    hb_o[0] = h.astype(BF16)


def _merge(a, o0, o1, bonus, g, p, x, lw, g1, sc2, sh2):
    b, n, d = x.shape
    tm = _pick(n, 256)
    w = C_WIDTH
    tile = lambda width: pl.BlockSpec((1, tm, width), lambda bi, i: (bi, i, 0))
    const = lambda shape: pl.BlockSpec(shape, lambda bi, i: (0,) * len(shape))
    per_b = pl.BlockSpec((1, 1, d), lambda bi, i: (bi, 0, 0))
    gate = lambda j: pl.BlockSpec((1, tm, d), lambda bi, i, j=j: (bi, i, COL_GATES // d + j))
    in_specs = ([tile(w)] * 5 + _halo_specs(tm, 3 * w, COL_CONV // (3 * w), n) + [gate(0), gate(1), gate(2)]
                + [tile(d), const((1, w)), const((1, w)), const((3, w)), const((N_BRANCH, w, d)), const((d, d)),
                   const((w, w)), per_b, const((1, d)), per_b, per_b])
    return pl.pallas_call(
        _merge_kernel,
        out_shape=(jax.ShapeDtypeStruct((b, n, d), F32), jax.ShapeDtypeStruct((b, n, d), F32),
                   jax.ShapeDtypeStruct((b, n, d), BF16)),
        grid=(b, n // tm),
        in_specs=in_specs,
        out_specs=[tile(d)] * 3,
        compiler_params=_cparams(("parallel", "parallel")),
        name="merge_mixers",
    )(a, o0, o1, bonus, g, p, p, p, p, p, p, x, lw['rwkv_ln_g'], lw['rwkv_ln_b'], lw['conv_w'], lw['w_branch_bf16'],
      lw['w_out_bf16'], lw['ones64'], g1, lw['norm2_g'], sc2, sh2)


def _router_kernel(h_ref, rw_ref, o_ref):
    logits = lax.dot_general(rw_ref[...], h_ref[0], (((1,), (1,)), ((), ())), precision=HIGHEST,
                             preferred_element_type=F32)
    z = jnp.exp(logits - jnp.max(logits, axis=0, keepdims=True))
    o_ref[0] = z / jnp.sum(z, axis=0, keepdims=True)


def _router(h, rw_t):
    b, n, d = h.shape
    tm = _pick(n, 512)
    return pl.pallas_call(
        _router_kernel,
        out_shape=jax.ShapeDtypeStruct((b, N_EXPERTS, n), F32),
        grid=(b, n // tm),
        in_specs=[pl.BlockSpec((1, tm, d), lambda bi, i: (bi, i, 0)), pl.BlockSpec((N_EXPERTS, d), lambda bi, i: (0, 0))],
        out_specs=pl.BlockSpec((1, N_EXPERTS, tm), lambda bi, i: (bi, 0, i)),
        compiler_params=_cparams(("parallel", "parallel")),
        name="router_affinity",
    )(h, rw_t)


def _select_kernel(aff_ref, tri_ref, pos_o, cnt_o, *, cap, lb):
    aff = aff_ref[0]
    n = aff.shape[1]
    bits = pltpu.bitcast(aff, jnp.int32)
    count = lambda mask: jnp.sum(mask.astype(F32), axis=1, keepdims=True)

    def refine(i, lo):
        cand = lo | jnp.left_shift(jnp.int32(1), 30 - i)
        return jnp.where(count(bits >= cand) >= cap, cand, lo)

    thr = lax.fori_loop(0, 31, refine, jnp.zeros((N_EXPERTS, 1), jnp.int32))
    gt = bits > thr
    eq = bits == thr
    need = cap - count(gt)
    tri = tri_ref[...]

    def prefix(mask_of_block, write):
        off = jnp.zeros((N_EXPERTS, 1), F32)
        for j in range(n // lb):
            m = mask_of_block(j)
            excl = jnp.dot(m.astype(BF16), tri, preferred_element_type=F32) + off
            write(j, m, excl)
            off = off + jnp.sum(m.astype(F32), axis=1, keepdims=True)

    blk = lambda x, j: x[:, j * lb:(j + 1) * lb]
    sel_blocks = [None] * (n // lb)

    def write_sel(j, m, excl):
        sel_blocks[j] = blk(gt, j) | (m & (excl < need))

    prefix(lambda j: blk(eq, j), write_sel)

    def write_pos(j, m, excl):
        c = excl.astype(jnp.int32)
        cnt_o[0, :, j * lb:(j + 1) * lb] = c
        pos_o[0, :, j * lb:(j + 1) * lb] = jnp.where(m, c, -1)

    prefix(lambda j: sel_blocks[j], write_pos)


def _select(aff, cap):
    b, e, n = aff.shape
    lb = min(n, LANES)
    tri = jnp.asarray(np.triu(np.ones((lb, lb), np.float32), 1), BF16)
    blk = pl.BlockSpec((1, e, n), lambda bi: (bi, 0, 0))
    out = jax.ShapeDtypeStruct((b, e, n), jnp.int32)
    return pl.pallas_call(
        functools.partial(_select_kernel, cap=cap, lb=lb),
        out_shape=(out, out),
        grid=(b,),
        in_specs=[blk, pl.BlockSpec((lb, lb), lambda bi: (0, 0))],
        out_specs=[blk, blk],
        compiler_params=_cparams(("parallel",)),
        name="expert_select",
    )(aff, tri)


def _expert_kernel(starts_ref, pos_ref, aff_ref, h_ref, w1_ref, w3_ref, w2_ref, yh_o, yl_o, xin_s, gate_s,
                   *, cap, tile, ntile, win):
    e, bi, t = pl.program_id(0), pl.program_id(1), pl.program_id(2)

    @pl.when(t == 0)
    def _():
        xin_s[...] = jnp.zeros(xin_s.shape, F32)
        gate_s[...] = jnp.zeros(gate_s.shape, F32)

    start = starts_ref[(e * pl.num_programs(1) + bi) * ntile + t]
    w0 = pl.multiple_of((start // SUBLANES) * SUBLANES, SUBLANES)
    slot = w0 + lax.broadcasted_iota(jnp.int32, (win, tile), 0)
    onehot = pos_ref[0, 0] == slot
    xin_s[pl.ds(w0, win), :] += jnp.dot(onehot.astype(BF16), h_ref[0], preferred_element_type=F32)
    affb = jnp.broadcast_to(aff_ref[0, 0], (SUBLANES, tile))
    gate_s[pl.ds(w0, win), :] += lax.dot_general(onehot.astype(F32), affb, (((1,), (1,)), ((), ())),
                                                 precision=HIGHEST, preferred_element_type=F32)

    @pl.when(t == ntile - 1)
    def _():
        rc = min(cap, SLOT_BLOCK)
        capp = yh_o.shape[2]
        if capp > cap:
            yh_o[0, 0, cap:capp, :] = jnp.zeros((capp - cap, yh_o.shape[3]), BF16)
            yl_o[0, 0, cap:capp, :] = jnp.zeros((capp - cap, yl_o.shape[3]), BF16)
        for c in range(cap // rc):
            x = xin_s[c * rc:(c + 1) * rc, :].astype(BF16)
            hid = _silu(jnp.dot(x, w1_ref[0], preferred_element_type=F32)) * jnp.dot(x, w3_ref[0], preferred_element_type=F32)
            y = jnp.dot(hid.astype(BF16), w2_ref[0], preferred_element_type=F32)
            y = y * gate_s[c * rc:(c + 1) * rc, 0:1]
            yh = y.astype(BF16)
            yh_o[0, 0, c * rc:(c + 1) * rc, :] = yh
            yl_o[0, 0, c * rc:(c + 1) * rc, :] = (y - yh.astype(F32)).astype(BF16)


def _moe_tile(n):
    return _pick(n, 256)


def _experts(starts, pos4, aff4, h_bf16, w1, w3, w2, cap):
    b, n, d = h_bf16.shape
    tile = _moe_tile(n)
    ntile = n // tile
    win = tile + SUBLANES
    rows = cap + win
    capp = -(-cap // SLOT_BLOCK) * SLOT_BLOCK
    f = w1.shape[2]
    grid_spec = pltpu.PrefetchScalarGridSpec(
        num_scalar_prefetch=1,
        grid=(N_EXPERTS, b, ntile),
        in_specs=[pl.BlockSpec((1, 1, 1, tile), lambda e, bi, t, s: (bi, e, 0, t)),
                  pl.BlockSpec((1, 1, 1, tile), lambda e, bi, t, s: (bi, e, 0, t)),
                  pl.BlockSpec((1, tile, d), lambda e, bi, t, s: (bi, t, 0)),
                  pl.BlockSpec((1, d, f), lambda e, bi, t, s: (e, 0, 0)),
                  pl.BlockSpec((1, d, f), lambda e, bi, t, s: (e, 0, 0)),
                  pl.BlockSpec((1, f, d), lambda e, bi, t, s: (e, 0, 0))],
        out_specs=[pl.BlockSpec((1, 1, capp, d), lambda e, bi, t, s: (e, bi, 0, 0))] * 2,
        scratch_shapes=[pltpu.VMEM((rows, d), F32), pltpu.VMEM((rows, SUBLANES), F32)])
    out = jax.ShapeDtypeStruct((N_EXPERTS, b, capp, d), BF16)
    return pl.pallas_call(
        functools.partial(_expert_kernel, cap=cap, tile=tile, ntile=ntile, win=win),
        out_shape=(out, out),
        grid_spec=grid_spec,
        compiler_params=_cparams(("arbitrary", "arbitrary", "arbitrary")),
        name="expert_gather_ffn",
    )(starts, pos4, aff4, h_bf16, w1, w3, w2)


def _combine_kernel(starts_ref, pos_ref, yha_ref, yla_ref, yhb_ref, ylb_ref, x_ref, g2_ref, o_ref,
                    *, tile, tpg, ntile, sb):
    bi, gi, e, tt = pl.program_id(0), pl.program_id(1), pl.program_id(2), pl.program_id(3)

    @pl.when((e == 0) & (tt == 0))
    def _():
        o_ref[...] = jnp.zeros(o_ref.shape, F32)

    start = starts_ref[(e * pl.num_programs(0) + bi) * ntile + gi * tpg + tt]
    k = start // sb
    posrow = pos_ref[0, 0]
    contrib = jnp.zeros((tile, o_ref.shape[-1]), F32)
    for half, (yh_ref, yl_ref) in enumerate(((yha_ref, yla_ref), (yhb_ref, ylb_ref))):
        slot = (k + half) * sb + lax.broadcasted_iota(jnp.int32, (sb, tile), 0)
        onehot_t = (posrow == slot).astype(F32).T.astype(BF16)
        contrib = contrib + jnp.dot(onehot_t, yh_ref[0, 0], preferred_element_type=F32)
        contrib = contrib + jnp.dot(onehot_t, yl_ref[0, 0], preferred_element_type=F32)
    o_ref[0, pl.ds(pl.multiple_of(tt * tile, tile), tile), :] += contrib

    @pl.when((e == pl.num_programs(2) - 1) & (tt == tpg - 1))
    def _():
        o_ref[0] = x_ref[0] + g2_ref[0] * o_ref[0]


def _combine(starts, pos4, yh, yl, x, g2, cap):
    b, n, d = x.shape
    tile = _moe_tile(n)
    ntile = n // tile
    gt = _pick(n, 1024)
    tpg = gt // tile
    sb = SLOT_BLOCK
    nsb = yh.shape[2] // sb

    def yblk(half):
        def imap(bi, gi, e, tt, s):
            k = s[(e * b + bi) * ntile + gi * tpg + tt] // sb
            return (e, bi, jnp.minimum(k + half, nsb - 1), 0)
        return pl.BlockSpec((1, 1, sb, d), imap)

    grid_spec = pltpu.PrefetchScalarGridSpec(
        num_scalar_prefetch=1,
        grid=(b, n // gt, N_EXPERTS, tpg),
        in_specs=[pl.BlockSpec((1, 1, 1, tile), lambda bi, gi, e, tt, s: (bi, e, 0, gi * tpg + tt)),
                  yblk(0), yblk(0), yblk(1), yblk(1),
                  pl.BlockSpec((1, gt, d), lambda bi, gi, e, tt, s: (bi, gi, 0)),
                  pl.BlockSpec((1, 1, d), lambda bi, gi, e, tt, s: (bi, 0, 0))],
        out_specs=pl.BlockSpec((1, gt, d), lambda bi, gi, e, tt, s: (bi, gi, 0)),
        scratch_shapes=[])
    return pl.pallas_call(
        functools.partial(_combine_kernel, tile=tile, tpg=tpg, ntile=ntile, sb=sb),
        out_shape=jax.ShapeDtypeStruct((b, n, d), F32),
        grid_spec=grid_spec,
        compiler_params=_cparams(("parallel", "parallel", "arbitrary", "arbitrary")),
        name="expert_combine",
    )(starts, pos4, yh, yl, yh, yl, x, g2)


def _moe(h_f32, h_bf16, x, g2, lw):
    b, n, _ = x.shape
    cap = CAPACITY_FACTOR * n // N_EXPERTS
    aff = _router(h_f32, lw['router_w_t'])
    pos, cnt = _select(aff, cap)
    tile = _moe_tile(n)
    starts = jnp.transpose(cnt[:, :, ::tile], (1, 0, 2)).reshape(-1)
    pos4 = pos.reshape(b, N_EXPERTS, 1, n)
    aff4 = aff.reshape(b, N_EXPERTS, 1, n)
    yh, yl = _experts(starts, pos4, aff4, h_bf16, lw['w1_bf16'], lw['w3_bf16'], lw['w2_bf16'], cap)
    return _combine(starts, pos4, yh, yl, x, g2, cap)


def _pad_w_in(w_in):
    d = w_in.shape[0]
    sizes = (A_WIDTH,) * 3 + (B_WIDTH,) * 3 + (B_DECAY_RANK, B_ICL_RANK, B_GATE_RANK) + (C_WIDTH,) * 3 + (N_BRANCH * D_MODEL,)
    cuts = np.cumsum((0,) + sizes)
    piece = lambda lo, hi: w_in[:, int(cuts[lo]):int(cuts[hi])]
    zeros = jnp.zeros((d, COL_GATES - COL_LOWRANK - 256), w_in.dtype)
    out = jnp.concatenate([piece(0, 3), piece(3, 6), piece(9, 12), piece(6, 9), zeros, piece(12, 13)], axis=1)
    assert out.shape[1] == N_IN_PAD
    return out.astype(BF16)


def _rows_at(w, lo, total=256):
    pad = [(0, 0)] * w.ndim
    pad[-2] = (lo, total - lo - w.shape[-2])
    return jnp.pad(w, pad)


def _rope_tables(rows):
    half = A_HD // 2
    inv = jnp.power(ROPE_THETA, -jnp.arange(0, half, 2, dtype=F32) / half)
    r = jnp.repeat(jnp.arange(rows, dtype=F32), GRID_W)
    col = jnp.tile(jnp.arange(GRID_W, dtype=F32), rows)
    ang = jnp.concatenate([r[:, None] * inv, col[:, None] * inv], axis=-1)
    cos, sin = jnp.cos(ang), jnp.sin(ang)
    reps = A_WIDTH // A_HD
    cos_t = jnp.tile(jnp.concatenate([cos, cos], axis=-1), (1, reps))
    sin_t = jnp.tile(jnp.concatenate([-sin, sin], axis=-1), (1, reps))
    return cos_t, sin_t


def _layer_weights(i, norm2_g, w_in, q_norm_g, k_norm_g, diff_lambda, diff_subln_g, rwkv_conv_w, rwkv_w0, rwkv_w_up,
                   rwkv_a0, rwkv_a_up, rwkv_g_up, rwkv_k_k, rwkv_k_a, rwkv_r_k, rwkv_ln_g, rwkv_ln_b, conv_w,
                   w_branch, w_out, router_w, exp_w1, exp_w3, exp_w2):
    row = lambda v: v.reshape(1, -1)
    return {
        'norm2_g': row(norm2_g[i]),
        'w_in_pad': _pad_w_in(w_in[i]),
        'q_g': row(jnp.tile(q_norm_g[i], A_WIDTH // A_HD)), 'k_g': row(jnp.tile(k_norm_g[i], A_WIDTH // A_HD)),
        'diff_lambda': diff_lambda[i], 'subln_g': row(diff_subln_g[i]),
        'rwkv_conv_w': rwkv_conv_w[i], 'rwkv_w0': rwkv_w0[i], 'rwkv_a0': rwkv_a0[i],
        'w_up_pad': _rows_at(rwkv_w_up[i], 0), 'a_up_pad': _rows_at(rwkv_a_up[i], B_DECAY_RANK),
        'g_up_pad': _rows_at(rwkv_g_up[i], B_DECAY_RANK + B_ICL_RANK),
        'rwkv_k_k': row(rwkv_k_k[i]), 'rwkv_k_a': row(rwkv_k_a[i]), 'rwkv_r_k': row(rwkv_r_k[i]),
        'rwkv_ln_g': row(rwkv_ln_g[i]), 'rwkv_ln_b': row(rwkv_ln_b[i]),
        'conv_w': conv_w[i], 'w_branch_bf16': w_branch[i].astype(BF16), 'w_out_bf16': w_out[i].astype(BF16),
        'router_w_t': router_w[i].T,
        'w1_bf16': exp_w1[i].astype(BF16), 'w3_bf16': exp_w3[i].astype(BF16), 'w2_bf16': exp_w2[i].astype(BF16),
        'ones64': _group_ones(B_WIDTH, B_HD),
    }


def _mixer_inputs(x, norm1_g, sc1, sh1, lw, cos_t, sin_t):
    p = _norm_mod_proj(x, norm1_g, sc1, sh1, lw['w_in_pad'])
    q, k, v = _qkv_prep(p, lw['q_g'], lw['k_g'], lw['ones64'], cos_t, sin_t)
    names = ('r', 'v', 'kk', 'g', 'bonus', 'dec0', 'b0', 'kd0', 'dec1', 'b1', 'kd1')
    prep = dict(zip(names, _rwkv_prep(p, lw)))
    return p, (q, k, v), prep


def kernel(x, c, ctx, c_ctx, norm1_g, norm2_g, ada_w, ada_b, w_in, q_norm_g, k_norm_g, diff_lambda, diff_subln_g,
           rwkv_conv_w, rwkv_w0, rwkv_w_up, rwkv_a0, rwkv_a_up, rwkv_g_up, rwkv_k_k, rwkv_k_a, rwkv_r_k, rwkv_ln_g,
           rwkv_ln_b, conv_w, w_branch, w_out, router_w, exp_w1, exp_w3, exp_w2):
    bsz, n_lat, d = x.shape
    depth = ada_w.shape[0]
    cos_t, sin_t = _rope_tables(n_lat // GRID_W)

    cond = jnp.concatenate([c, jnp.broadcast_to(c_ctx[None, :], (SUBLANES - bsz % SUBLANES, d))], axis=0)
    ada = _ada_params(cond, ada_w, ada_b)

    xc, xl = ctx, x
    for i in range(depth):
        lam_init = 0.8 - 0.6 * math.exp(-0.3 * i)
        need_ctx = i != depth - 1
        lw = _layer_weights(i, norm2_g, w_in, q_norm_g, k_norm_g, diff_lambda, diff_subln_g, rwkv_conv_w, rwkv_w0,
                            rwkv_w_up, rwkv_a0, rwkv_a_up, rwkv_g_up, rwkv_k_k, rwkv_k_a, rwkv_r_k, rwkv_ln_g,
                            rwkv_ln_b, conv_w, w_branch, w_out, router_w, exp_w1, exp_w3, exp_w2)
        mods = ada[i].reshape(-1, 6, d)
        lat = [mods[:bsz, j][:, None, :] for j in range(6)]
        cxt = [jnp.broadcast_to(mods[bsz, j][None, None, :], (bsz, 1, d)) for j in range(6)]

        pc, (qc, kc, vc), prep_c = _mixer_inputs(xc, norm1_g[i], cxt[1], cxt[0], lw, None, None)
        pl_, (ql, kl, vl), prep_l = _mixer_inputs(xl, norm1_g[i], lat[1], lat[0], lw, cos_t, sin_t)

        al = _flash_attention(lw['diff_lambda'], lw['subln_g'], ql, kc, vc, kl, vl, lam_init)
        operands, dims = _rwkv_scan_layout(prep_c, prep_l)
        (oc0, oc1), (ol0, ol1) = _rwkv_unscan(_rwkv_scan(*operands), dims)

        xl, hl, hlb = _merge(al, ol0, ol1, prep_l['bonus'], prep_l['g'], pl_, xl, lw, lat[2], lat[4], lat[3])
        xl = _moe(hl, hlb, xl, lat[5], lw)
        if need_ctx:
            ac = _ctx_attention(lw['diff_lambda'], lw['subln_g'], qc, kc, vc, lam_init)
            xc, hc, hcb = _merge(ac, oc0, oc1, prep_c['bonus'], prep_c['g'], pc, xc, lw, cxt[2], cxt[4], cxt[3])
            xc = _moe(hc, hcb, xc, cxt[5], lw)
    return xl
```

```python
import functools
import math

import numpy as np
import jax
import jax.numpy as jnp
from jax import lax
from jax.experimental import pallas as pl
from jax.experimental.pallas import tpu as pltpu

F32 = jnp.float32
BF16 = jnp.bfloat16
HIGHEST = lax.Precision.HIGHEST

D_MODEL = 1024
GRID_W = 64
A_HEADS = 4
A_HD = 64
A_WIDTH = A_HEADS * 2 * A_HD
B_HEADS = 8
B_HD = 64
B_WIDTH = B_HEADS * B_HD
B_DECAY_RANK = 64
B_ICL_RANK = 64
B_GATE_RANK = 128
C_WIDTH = 512
N_BRANCH = 3
N_EXPERTS = 16
EXPERT_FF = 1024
CAPACITY_FACTOR = 2
ROPE_THETA = 10000.0
NORM_EPS = 1e-6
GN_EPS = 64e-5

COL_ATTN = 0
COL_RWKV = 1536
COL_CONV = 3072
COL_LOWRANK = 4608
COL_GATES = 5120
N_IN_PAD = 8192

VMEM_LIMIT = 56 * 1024 * 1024
LANES = 128
SUBLANES = 8
SLOT_BLOCK = 256


def _cparams(sem):
    return pltpu.CompilerParams(dimension_semantics=sem, vmem_limit_bytes=VMEM_LIMIT)


def _pick(n, pref):
    t = min(n, pref)
    while n % t:
        t -= SUBLANES
    return t


def _group_ones(width, group):
    idx = np.arange(width) // group
    return jnp.asarray((idx[:, None] == idx[None, :]).astype(np.float32))


def _silu(x):
    return x * jax.nn.sigmoid(x)


def _ada_kernel(cond_ref, w_ref, b_ref, o_ref):
    c = cond_ref[...]
    o_ref[0] = jnp.dot(_silu(c), w_ref[0], precision=HIGHEST, preferred_element_type=F32) + b_ref[0]


def _ada_params(cond, ada_w, ada_b):
    depth, d, n6 = ada_w.shape
    tn = _pick(n6, 1536)
    return pl.pallas_call(
        _ada_kernel,
        out_shape=jax.ShapeDtypeStruct((depth, cond.shape[0], n6), F32),
        grid=(depth, n6 // tn),
        in_specs=[pl.BlockSpec(cond.shape, lambda i, j: (0, 0)),
                  pl.BlockSpec((1, d, tn), lambda i, j: (i, 0, j)),
                  pl.BlockSpec((1, 1, tn), lambda i, j: (i, 0, j))],
        out_specs=pl.BlockSpec((1, cond.shape[0], tn), lambda i, j: (i, 0, j)),
        compiler_params=_cparams(("parallel", "parallel")),
        name="ada_params",
    )(cond, ada_w, ada_b.reshape(depth, 1, n6))


def _proj_kernel(x_ref, g_ref, sc_ref, sh_ref, w_ref, o_ref):
    x = x_ref[0]
    ms = jnp.mean(x * x, axis=-1, keepdims=True)
    y = x * lax.rsqrt(ms + NORM_EPS) * g_ref[...]
    h = y * (1.0 + sc_ref[0]) + sh_ref[0]
    o_ref[0] = jnp.dot(h.astype(BF16), w_ref[...], preferred_element_type=F32)


def _norm_mod_proj(x, g, sc, sh, w_bf16):
    b, n, d = x.shape
    npad = w_bf16.shape[1]
    tm = _pick(n, 512)
    tn = 1024
    return pl.pallas_call(
        _proj_kernel,
        out_shape=jax.ShapeDtypeStruct((b, n, npad), F32),
        grid=(npad // tn, b, n // tm),
        in_specs=[pl.BlockSpec((1, tm, d), lambda j, bi, i: (bi, i, 0)),
                  pl.BlockSpec((1, d), lambda j, bi, i: (0, 0)),
                  pl.BlockSpec((1, 1, d), lambda j, bi, i: (bi, 0, 0)),
                  pl.BlockSpec((1, 1, d), lambda j, bi, i: (bi, 0, 0)),
                  pl.BlockSpec((d, tn), lambda j, bi, i: (0, j))],
        out_specs=pl.BlockSpec((1, tm, tn), lambda j, bi, i: (bi, i, j)),
        compiler_params=_cparams(("parallel", "parallel", "parallel")),
        name="norm_mod_proj",
    )(x, g.reshape(1, d), sc, sh, w_bf16)


def _swap_halves(x, half):
    width = x.shape[-1]
    lane = lax.broadcasted_iota(jnp.int32, (1, width), 1) % (2 * half)
    fwd = pltpu.roll(x, width - half, axis=1)
    bwd = pltpu.roll(x, half, axis=1)
    return jnp.where(lane < half, fwd, bwd)


def _qkv_prep_kernel(*refs, use_rope):
    if use_rope:
        q_ref, k_ref, v_ref, qg_ref, kg_ref, ones_ref, cos_ref, sin_ref, qo_ref, ko_ref, vo_ref = refs
    else:
        q_ref, k_ref, v_ref, qg_ref, kg_ref, ones_ref, qo_ref, ko_ref, vo_ref = refs

    def head_norm(x, gain):
        ms = jnp.dot(x * x, ones_ref[...], precision=HIGHEST, preferred_element_type=F32) * (1.0 / A_HD)
        return x * lax.rsqrt(ms + NORM_EPS) * gain

    def rope(x):
        if not use_rope:
            return x
        return x * cos_ref[...] + _swap_halves(x, A_HD // 2) * sin_ref[...]

    q = rope(head_norm(q_ref[0], qg_ref[...]))
    k = rope(head_norm(k_ref[0], kg_ref[...]))
    qo_ref[0] = (q * (A_HD ** -0.5)).astype(BF16)
    ko_ref[0] = k.astype(BF16)
    vo_ref[0] = v_ref[0].astype(BF16)


def _qkv_prep(p, qg, kg, ones, cos_t, sin_t):
    b, n, _ = p.shape
    tm = _pick(n, 512)
    use_rope = cos_t is not None
    w = A_WIDTH
    col = lambda c: pl.BlockSpec((1, tm, w), lambda bi, i, c=c: (bi, i, COL_ATTN // w + c))
    const = lambda shape: pl.BlockSpec(shape, lambda bi, i: (0,) * len(shape))
    in_specs = [col(0), col(1), col(2), const((1, w)), const((1, w)), const((w, w))]
    args = [p, p, p, qg, kg, ones]
    if use_rope:
        in_specs += [pl.BlockSpec((tm, w), lambda bi, i: (i, 0))] * 2
        args += [cos_t, sin_t]
    out = jax.ShapeDtypeStruct((b, n, w), BF16)
    return pl.pallas_call(
        functools.partial(_qkv_prep_kernel, use_rope=use_rope),
        out_shape=(out, out, out),
        grid=(b, n // tm),
        in_specs=in_specs,
        out_specs=[pl.BlockSpec((1, tm, w), lambda bi, i: (bi, i, 0))] * 3,
        compiler_params=_cparams(("parallel", "parallel")),
        name="qkv_prep_rope" if use_rope else "qkv_prep",
    )(*args)


def _map_masked(q):
    lane = lax.broadcasted_iota(jnp.int32, (1, 2 * A_HD), 1)
    zero = jnp.zeros_like(q)
    return (jnp.where(lane < A_HD, q, zero), jnp.where(lane >= A_HD, q, zero))


def _lambda_full(dl_ref, lam_init):
    dl = dl_ref[...]
    s1 = jnp.sum(dl[0:1] * dl[1:2], axis=-1, keepdims=True)
    s2 = jnp.sum(dl[2:3] * dl[3:4], axis=-1, keepdims=True)
    return jnp.exp(s1) - jnp.exp(s2) + lam_init


def _subln(o, sg_ref, lam_init):
    ms = jnp.mean(o * o, axis=-1, keepdims=True)
    return o * lax.rsqrt(ms + NORM_EPS) * sg_ref[...] * (1.0 - lam_init)


def _scores(qm, k):
    return lax.dot_general(qm, k, (((1,), (1,)), ((), ())), preferred_element_type=F32)


def _flash_kernel(dl_ref, sg_ref, q_ref, kc_ref, vc_ref, k_ref, v_ref, o_ref, m0, l0, a0, m1, l1, a1, *, lam_init, nkv):
    kj = pl.program_id(3)
    qm = _map_masked(q_ref[0])
    state = ((m0, l0, a0), (m1, l1, a1))

    def update(k, v):
        for q_map, (m_s, l_s, acc_s) in zip(qm, state):
            s = _scores(q_map, k)
            m_prev = m_s[...]
            m_new = jnp.maximum(m_prev, jnp.max(s, axis=-1, keepdims=True))
            alpha = jnp.exp(m_prev - m_new)
            p = jnp.exp(s - m_new)
            l_s[...] = alpha * l_s[...] + jnp.sum(p, axis=-1, keepdims=True)
            acc_s[...] = alpha * acc_s[...] + jnp.dot(p.astype(BF16), v, preferred_element_type=F32)
            m_s[...] = m_new

    @pl.when(kj == 0)
    def _():
        for m_s, l_s, acc_s in state:
            m_s[...] = jnp.full(m_s.shape, -jnp.inf, F32)
            l_s[...] = jnp.zeros(l_s.shape, F32)
            acc_s[...] = jnp.zeros(acc_s.shape, F32)
        update(kc_ref[0], vc_ref[0])

    update(k_ref[0], v_ref[0])

    @pl.when(kj == nkv - 1)
    def _():
        lam = _lambda_full(dl_ref, lam_init)
        o = a0[...] / l0[...] - lam * (a1[...] / l1[...])
        o_ref[0] = _subln(o, sg_ref, lam_init)


def _flash_attention(dl, sg, q, kc, vc, kl, vl, lam_init):
    b, s, _ = q.shape
    nc = kc.shape[1]
    hw = 2 * A_HD
    tq = _pick(s, 512)
    tk = _pick(s, 1024)
    nkv = s // tk
    return pl.pallas_call(
        functools.partial(_flash_kernel, lam_init=lam_init, nkv=nkv),
        out_shape=jax.ShapeDtypeStruct((b, s, A_WIDTH), F32),
        grid=(b, A_HEADS, s // tq, nkv),
        in_specs=[pl.BlockSpec((4, A_HD), lambda bi, h, i, j: (0, 0)),
                  pl.BlockSpec((1, hw), lambda bi, h, i, j: (0, 0)),
                  pl.BlockSpec((1, tq, hw), lambda bi, h, i, j: (bi, i, h)),
                  pl.BlockSpec((1, nc, hw), lambda bi, h, i, j: (bi, 0, h)),
                  pl.BlockSpec((1, nc, hw), lambda bi, h, i, j: (bi, 0, h)),
                  pl.BlockSpec((1, tk, hw), lambda bi, h, i, j: (bi, j, h)),
                  pl.BlockSpec((1, tk, hw), lambda bi, h, i, j: (bi, j, h))],
        out_specs=pl.BlockSpec((1, tq, hw), lambda bi, h, i, j: (bi, i, h)),
        scratch_shapes=[pltpu.VMEM((tq, 1), F32), pltpu.VMEM((tq, 1), F32), pltpu.VMEM((tq, hw), F32)] * 2,
        compiler_params=_cparams(("parallel", "parallel", "parallel", "arbitrary")),
        name="diff_flash_attention",
    )(dl, sg, q, kc, vc, kl, vl)


def _ctx_attn_kernel(dl_ref, sg_ref, q_ref, k_ref, v_ref, o_ref, *, lam_init):
    qm = _map_masked(q_ref[0])
    outs = []
    for mi in range(2):
        s = _scores(qm[mi], k_ref[0])
        p = jnp.exp(s - jnp.max(s, axis=-1, keepdims=True))
        o = jnp.dot(p.astype(BF16), v_ref[0], preferred_element_type=F32)
        outs.append(o / jnp.sum(p, axis=-1, keepdims=True))
    lam = _lambda_full(dl_ref, lam_init)
    o_ref[0] = _subln(outs[0] - lam * outs[1], sg_ref, lam_init)


def _ctx_attention(dl, sg, q, k, v, lam_init):
    b, n, _ = q.shape
    hw = 2 * A_HD
    blk = pl.BlockSpec((1, n, hw), lambda bi, h: (bi, 0, h))
    return pl.pallas_call(
        functools.partial(_ctx_attn_kernel, lam_init=lam_init),
        out_shape=jax.ShapeDtypeStruct((b, n, A_WIDTH), F32),
        grid=(b, A_HEADS),
        in_specs=[pl.BlockSpec((4, A_HD), lambda bi, h: (0, 0)), pl.BlockSpec((1, hw), lambda bi, h: (0, 0)),
                  blk, blk, blk],
        out_specs=blk,
        compiler_params=_cparams(("parallel", "parallel")),
        name="ctx_attention",
    )(dl, sg, q, k, v)


def _halo_specs(tm, width, colblk, n):
    r8 = tm // SUBLANES
    last = n // SUBLANES - 1
    main = pl.BlockSpec((1, tm, width), lambda bi, i: (bi, i, colblk))
    prev = pl.BlockSpec((1, SUBLANES, width), lambda bi, i: (bi, jnp.maximum(i * r8 - 1, 0), colblk))
    nxt = pl.BlockSpec((1, SUBLANES, width), lambda bi, i: (bi, jnp.minimum((i + 1) * r8, last), colblk))
    return [main, prev, nxt]


def _dwconv3(x, prev_row, next_row, w_ref):
    tm = x.shape[0]
    row = lax.broadcasted_iota(jnp.int32, (tm, 1), 0)
    xp = jnp.where(row == 0, prev_row, pltpu.roll(x, 1, axis=0))
    xn = jnp.where(row == tm - 1, next_row, pltpu.roll(x, tm - 1, axis=0))
    w = w_ref[...]
    return w[0:1] * xp + w[1:2] * x + w[2:3] * xn


def _edge_rows(prev_ref, next_ref):
    i = pl.program_id(1)
    n = pl.num_programs(1)
    pr = prev_ref[0][SUBLANES - 1:SUBLANES]
    nr = next_ref[0][0:1]
    pr = jnp.where(i == 0, jnp.zeros_like(pr), pr)
    nr = jnp.where(i == n - 1, jnp.zeros_like(nr), nr)
    return pr, nr


def _softplus(u):
    return jnp.maximum(u, 0.0) + jnp.log(1.0 + jnp.exp(-jnp.abs(u)))


def _rwkv_prep_kernel(rkv_ref, prev_ref, next_ref, lr_ref, cw_ref, w0_ref, wup_ref, a0_ref, aup_ref, gup_ref,
                      kk_ref_w, ka_ref, rk_ref, ones_ref,
                      r_o, v_o, kk_o, g_o, bonus_o, dec0_o, b0_o, kd0_o, dec1_o, b1_o, kd1_o):
    pr, nr = _edge_rows(prev_ref, next_ref)
    rkv = _dwconv3(rkv_ref[0], pr, nr, cw_ref)
    w = B_WIDTH
    r, k, v = rkv[:, 0:w], rkv[:, w:2 * w], rkv[:, 2 * w:3 * w]
    ones = ones_ref[...]
    gsum = lambda t: jnp.dot(t, ones, precision=HIGHEST, preferred_element_type=F32)
    kkr = k * kk_ref_w[...]
    nrm = jnp.maximum(jnp.sqrt(gsum(kkr * kkr)), 1e-12)
    kk = kkr / nrm
    lr = lr_ref[0]
    th = jnp.tanh(lr)
    sg = jax.nn.sigmoid(lr)
    r_o[0] = r
    v_o[0] = v
    kk_o[0] = kk
    g_o[0] = jnp.dot(sg, gup_ref[...], precision=HIGHEST, preferred_element_type=F32)
    kd_sum = jnp.zeros_like(k)
    outs = ((dec0_o, b0_o, kd0_o), (dec1_o, b1_o, kd1_o))
    for d in range(2):
        z = w0_ref[d:d + 1] + jnp.dot(th, wup_ref[d], precision=HIGHEST, preferred_element_type=F32)
        wlog = -_softplus(-z) - 0.5
        dec = jnp.exp(-jnp.exp(wlog))
        a = jax.nn.sigmoid(a0_ref[d:d + 1] + jnp.dot(lr, aup_ref[d], precision=HIGHEST, preferred_element_type=F32))
        kd = k * (1.0 + (a - 1.0) * ka_ref[...])
        outs[d][0][0] = dec
        outs[d][1][0] = kk * a
        outs[d][2][0] = kd
        kd_sum = kd_sum + kd
    bonus_o[0] = gsum(r * kd_sum * rk_ref[...]) * v


def _rwkv_prep(p, lw):
    b, n, _ = p.shape
    tm = _pick(n, 256)
    w = B_WIDTH
    w3 = 3 * w
    const = lambda shape: pl.BlockSpec(shape, lambda bi, i: (0,) * len(shape))
    in_specs = _halo_specs(tm, w3, COL_RWKV // w3, n) + [
        pl.BlockSpec((1, tm, 256), lambda bi, i: (bi, i, COL_LOWRANK // 256)),
        const((3, w3)), const((2, w)), const((2, 256, w)), const((2, w)), const((2, 256, w)), const((256, w)),
        const((1, w)), const((1, w)), const((1, w)), const((w, w))]
    out = jax.ShapeDtypeStruct((b, n, w), F32)
    return pl.pallas_call(
        _rwkv_prep_kernel,
        out_shape=(out,) * 11,
        grid=(b, n // tm),
        in_specs=in_specs,
        out_specs=[pl.BlockSpec((1, tm, w), lambda bi, i: (bi, i, 0))] * 11,
        compiler_params=_cparams(("parallel", "parallel")),
        name="rwkv_prep",
    )(p, p, p, p, lw['rwkv_conv_w'], lw['rwkv_w0'], lw['w_up_pad'], lw['rwkv_a0'], lw['a_up_pad'], lw['g_up_pad'],
      lw['rwkv_k_k'], lw['rwkv_k_a'], lw['rwkv_r_k'], lw['ones64'])


V_SLABS = B_HD // 2 // SUBLANES


def _scan_kernel(w_ref, b_ref, kd_ref, r_ref, kk_ref, kkx_ref, v_ref, y_ref, s_ref, sa_ref, *, tc):
    @pl.when(pl.program_id(0) == 0)
    def _():
        s_ref[...] = jnp.zeros(s_ref.shape, F32)
        sa_ref[...] = jnp.zeros(sa_ref.shape, F32)

    lanes = s_ref.shape[-1]
    vrows = B_HD // 2
    bcast = lambda row: jnp.broadcast_to(row, (SUBLANES, lanes))

    def step(t, last_of_block):
        base = pl.multiple_of(t * vrows, vrows)
        sa = [sa_ref[s] for s in range(V_SLABS)]
        vv = [v_ref[pl.ds(base + SUBLANES * s, SUBLANES), :] for s in range(V_SLABS)]
        y = [jnp.zeros((SUBLANES, lanes), F32) for _ in range(V_SLABS)]
        san = [jnp.zeros((SUBLANES, lanes), F32) for _ in range(V_SLABS)]
        for k in range(B_HD):
            row = lambda ref: bcast(ref[k, pl.ds(t, 1), :])
            wk, bk, kdk, rk = row(w_ref), row(b_ref), row(kd_ref), row(r_ref)
            kkn = bcast(kkx_ref[k, 0:1, :]) if last_of_block else bcast(kk_ref[k, pl.ds(t + 1, 1), :])
            for s in range(V_SLABS):
                new = s_ref[s, k] * wk - sa[s] * bk + vv[s] * kdk
                s_ref[s, k] = new
                y[s] = y[s] + new * rk
                san[s] = san[s] + new * kkn
        for s in range(V_SLABS):
            y_ref[pl.ds(base + SUBLANES * s, SUBLANES), :] = y[s]
            sa_ref[s] = san[s]

    def body(t, carry):
        step(t, False)
        return carry

    lax.fori_loop(0, tc - 1, body, 0)
    step(tc - 1, True)


def _rwkv_scan(w, bb, kd, r, kk, v):
    _, t, lanes = w.shape
    tc = _pick(t, 64)
    vrows = B_HD // 2
    last8 = t // SUBLANES - 1
    kspec = pl.BlockSpec((B_HD, tc, lanes), lambda i: (0, i, 0))
    kxspec = pl.BlockSpec((B_HD, SUBLANES, lanes), lambda i: (0, jnp.minimum((i + 1) * (tc // SUBLANES), last8), 0))
    vspec = pl.BlockSpec((tc * vrows, lanes), lambda i: (i, 0))
    return pl.pallas_call(
        functools.partial(_scan_kernel, tc=tc),
        out_shape=jax.ShapeDtypeStruct(v.shape, F32),
        grid=(t // tc,),
        in_specs=[kspec] * 5 + [kxspec, vspec],
        out_specs=vspec,
        scratch_shapes=[pltpu.VMEM((V_SLABS, B_HD, SUBLANES, lanes), F32), pltpu.VMEM((V_SLABS, SUBLANES, lanes), F32)],
        compiler_params=_cparams(("arbitrary",)),
        name="rwkv_scan",
    )(w, bb, kd, r, kk, kk, v)


TIME_BLOCK = 128
HEAD_PAIRS = B_WIDTH // LANES


def _scan_block_maps(nctx, nlat):
    fwd_c = lambda i: jnp.minimum(i, nctx - 1)
    fwd_l = lambda i: jnp.clip(i - nctx, 0, nlat - 1)
    bwd_c = lambda i: jnp.clip(nctx - 1 - i, 0, nctx - 1)
    bwd_l = lambda i: jnp.clip(nlat - 1 - (i - nctx), 0, nlat - 1)
    return fwd_c, fwd_l, bwd_c, bwd_l


def _fill_transposed(u_ref, src_ref, direction, rev_ref, bsz):
    for b in range(bsz):
        for j in range(HEAD_PAIRS):
            tile = src_ref[b, :, j * LANES:(j + 1) * LANES]
            if direction == 1:
                tile = jnp.dot(rev_ref[...], tile, precision=HIGHEST, preferred_element_type=F32)
            q = (direction * bsz + b) * HEAD_PAIRS + j
            u_ref[q * LANES:(q + 1) * LANES, :] = tile.T


def _fill_both(u_ref, cf_ref, lf_ref, cb_ref, lb_ref, rev_ref, nctx, bsz):
    i = pl.program_id(0)

    @pl.when(i < nctx)
    def _():
        _fill_transposed(u_ref, cf_ref, 0, rev_ref, bsz)
        _fill_transposed(u_ref, cb_ref, 1, rev_ref, bsz)

    @pl.when(i >= nctx)
    def _():
        _fill_transposed(u_ref, lf_ref, 0, rev_ref, bsz)
        _fill_transposed(u_ref, lb_ref, 1, rev_ref, bsz)


def _to_scan_k_kernel(cf_ref, lf_ref, cb_ref, lb_ref, rev_ref, o_ref, u_ref, *, nctx, bsz):
    _fill_both(u_ref, cf_ref, lf_ref, cb_ref, lb_ref, rev_ref, nctx, bsz)
    nrow = u_ref.shape[0] // B_HD
    for k in range(B_HD):
        g = u_ref[pl.ds(k, nrow, stride=B_HD), :]
        o_ref[k] = jnp.concatenate([g, g], axis=0).T


def _to_scan_v_kernel(cf_ref, lf_ref, cb_ref, lb_ref, rev_ref, o_ref, u_ref, *, nctx, bsz):
    _fill_both(u_ref, cf_ref, lf_ref, cb_ref, lb_ref, rev_ref, nctx, bsz)
    nrow = u_ref.shape[0] // B_HD
    vrows = B_HD // 2
    for vi in range(vrows):
        halves = [u_ref[pl.ds(half * vrows + vi, nrow, stride=B_HD), :] for half in range(2)]
        o_ref[pl.ds(vi, TIME_BLOCK, stride=vrows), :] = jnp.concatenate(halves, axis=0).T


def _to_scan(ctx_f, lat_f, ctx_b, lat_b, rev, values):
    bsz, nc, width = ctx_f.shape
    nl = lat_f.shape[1]
    nctx, nlat = nc // TIME_BLOCK, nl // TIME_BLOCK
    lanes = 4 * bsz * B_HEADS
    t = nc + nl
    maps = _scan_block_maps(nctx, nlat)
    src = lambda f: pl.BlockSpec((bsz, TIME_BLOCK, width), lambda i, f=f: (0, f(i), 0))
    if values:
        kern, out_shape = _to_scan_v_kernel, (t * (B_HD // 2), lanes)
        out_spec = pl.BlockSpec((TIME_BLOCK * (B_HD // 2), lanes), lambda i: (i, 0))
    else:
        kern, out_shape = _to_scan_k_kernel, (B_HD, t, lanes)
        out_spec = pl.BlockSpec((B_HD, TIME_BLOCK, lanes), lambda i: (0, i, 0))
    return pl.pallas_call(
        functools.partial(kern, nctx=nctx, bsz=bsz),
        out_shape=jax.ShapeDtypeStruct(out_shape, F32),
        grid=(nctx + nlat,),
        in_specs=[src(maps[0]), src(maps[1]), src(maps[2]), src(maps[3]),
                  pl.BlockSpec((TIME_BLOCK, TIME_BLOCK), lambda i: (0, 0))],
        out_specs=out_spec,
        scratch_shapes=[pltpu.VMEM((2 * bsz * HEAD_PAIRS * LANES, TIME_BLOCK), F32)],
        compiler_params=_cparams(("parallel",)),
        name="to_scan_v" if values else "to_scan_k",
    )(ctx_f, lat_f, ctx_b, lat_b, rev)


def _from_scan_kernel(y_ref, rev_ref, o0_ref, o1_ref, u_ref, *, bsz):
    nrow = u_ref.shape[0] // B_HD
    vrows = B_HD // 2
    for vi in range(vrows):
        mt = y_ref[pl.ds(vi, TIME_BLOCK, stride=vrows), :].T
        for half in range(2):
            u_ref[pl.ds(half * vrows + vi, nrow, stride=B_HD), :] = mt[half * nrow:(half + 1) * nrow]
    for direction, o_ref in enumerate((o0_ref, o1_ref)):
        for b in range(bsz):
            for j in range(HEAD_PAIRS):
                q = (direction * bsz + b) * HEAD_PAIRS + j
                tile = u_ref[q * LANES:(q + 1) * LANES, :].T
                if direction == 1:
                    tile = jnp.dot(rev_ref[...], tile, precision=HIGHEST, preferred_element_type=F32)
                o_ref[b, :, j * LANES:(j + 1) * LANES] = tile


def _from_scan(y, rev, bsz, nc, nl):
    nctx, nlat = nc // TIME_BLOCK, nl // TIME_BLOCK
    lanes = y.shape[1]
    vrows = B_HD // 2
    bwd = lambda i: jnp.where(i < nctx, nctx - 1 - i, 2 * nctx + nlat - 1 - i)
    out = jax.ShapeDtypeStruct((bsz, nc + nl, B_WIDTH), F32)
    return pl.pallas_call(
        functools.partial(_from_scan_kernel, bsz=bsz),
        out_shape=(out, out),
        grid=(nctx + nlat,),
        in_specs=[pl.BlockSpec((TIME_BLOCK * vrows, lanes), lambda i: (i, 0)),
                  pl.BlockSpec((TIME_BLOCK, TIME_BLOCK), lambda i: (0, 0))],
        out_specs=[pl.BlockSpec((bsz, TIME_BLOCK, B_WIDTH), lambda i: (0, i, 0)),
                   pl.BlockSpec((bsz, TIME_BLOCK, B_WIDTH), lambda i: (0, bwd(i), 0))],
        scratch_shapes=[pltpu.VMEM((2 * bsz * HEAD_PAIRS * LANES, TIME_BLOCK), F32)],
        compiler_params=_cparams(("parallel",)),
        name="from_scan",
    )(y, rev)


def _rwkv_bidirectional(prep_c, prep_l):
    bsz, nc, _ = prep_c['r'].shape
    nl = prep_l['r'].shape[1]
    rev = jnp.asarray(np.eye(TIME_BLOCK, dtype=np.float32)[::-1].copy())
    pair = lambda fwd, bwd, values=False: _to_scan(prep_c[fwd], prep_l[fwd], prep_c[bwd], prep_l[bwd], rev, values)
    y = _rwkv_scan(pair('dec0', 'dec1'), pair('b0', 'b1'), pair('kd0', 'kd1'), pair('r', 'r'), pair('kk', 'kk'),
                   pair('v', 'v', True))
    return _from_scan(y, rev, bsz, nc, nl)


def _merge_kernel(a_ref, o0_ref, o1_ref, bonus_ref, g_ref, cv_ref, cprev_ref, cnext_ref, ga_ref, gb_ref, gc_ref,
                  x_ref, lng_ref, lnb_ref, cw_ref, wb_ref, wo_ref, ones_ref, g1_ref, n2_ref, sc_ref, sh_ref,
                  x_o, hf_o, hb_o):
    ones = ones_ref[...]
    gmean = lambda t: jnp.dot(t, ones, precision=HIGHEST, preferred_element_type=F32) * (1.0 / B_HD)

    def group_norm(o):
        dlt = o - gmean(o)
        return dlt * lax.rsqrt(gmean(dlt * dlt) + GN_EPS) * lng_ref[...] + lnb_ref[...]

    yb = (group_norm(o0_ref[0]) + group_norm(o1_ref[0]) + bonus_ref[0]) * g_ref[0]

    w = C_WIDTH
    u = lambda blk: blk[:, w:2 * w] * blk[:, 2 * w:3 * w]
    cv = cv_ref[0]
    i = pl.program_id(1)
    pr = u(cprev_ref[0][SUBLANES - 1:SUBLANES])
    nr = u(cnext_ref[0][0:1])
    pr = jnp.where(i == 0, jnp.zeros_like(pr), pr)
    nr = jnp.where(i == pl.num_programs(1) - 1, jnp.zeros_like(nr), nr)
    yc = cv[:, 0:w] * _dwconv3(u(cv), pr, nr, cw_ref)

    d = D_MODEL
    m = jnp.zeros((cv.shape[0], d), F32)
    for j, (yj, gate_ref) in enumerate(((a_ref[0], ga_ref), (yb, gb_ref), (yc, gc_ref))):
        m = m + jax.nn.sigmoid(gate_ref[0]) * jnp.dot(yj.astype(BF16), wb_ref[j], preferred_element_type=F32)
    ymix = jnp.dot(m.astype(BF16), wo_ref[...], preferred_element_type=F32)
    x1 = x_ref[0] + g1_ref[0] * ymix
    x_o[0] = x1
    ms = jnp.mean(x1 * x1, axis=-1, keepdims=True)
    h = x1 * lax.rsqrt(ms + NORM_EPS) * n2_ref[...] * (1.0 + sc_ref[0]) + sh_ref[0]
    hf_o[0] = h
    hb_o[0] = h.astype(BF16)


def _merge(a, o0, o1, row_off, bonus, g, p, x, lw, g1, sc2, sh2):
    b, n, d = x.shape
    tm = _pick(n, 256)
    w = C_WIDTH
    assert row_off % tm == 0
    tile = lambda width: pl.BlockSpec((1, tm, width), lambda bi, i: (bi, i, 0))
    scan_tile = pl.BlockSpec((1, tm, w), lambda bi, i: (bi, i + row_off // tm, 0))
    const = lambda shape: pl.BlockSpec(shape, lambda bi, i: (0,) * len(shape))
    per_b = pl.BlockSpec((1, 1, d), lambda bi, i: (bi, 0, 0))
    gate = lambda j: pl.BlockSpec((1, tm, d), lambda bi, i, j=j: (bi, i, COL_GATES // d + j))
    in_specs = ([tile(w), scan_tile, scan_tile, tile(w), tile(w)]
                + _halo_specs(tm, 3 * w, COL_CONV // (3 * w), n) + [gate(0), gate(1), gate(2)]
                + [tile(d), const((1, w)), const((1, w)), const((3, w)), const((N_BRANCH, w, d)), const((d, d)),
                   const((w, w)), per_b, const((1, d)), per_b, per_b])
    return pl.pallas_call(
        _merge_kernel,
        out_shape=(jax.ShapeDtypeStruct((b, n, d), F32), jax.ShapeDtypeStruct((b, n, d), F32),
                   jax.ShapeDtypeStruct((b, n, d), BF16)),
        grid=(b, n // tm),
        in_specs=in_specs,
        out_specs=[tile(d)] * 3,
        compiler_params=_cparams(("parallel", "parallel")),
        name="merge_mixers",
    )(a, o0, o1, bonus, g, p, p, p, p, p, p, x, lw['rwkv_ln_g'], lw['rwkv_ln_b'], lw['conv_w'], lw['w_branch_bf16'],
      lw['w_out_bf16'], lw['ones64'], g1, lw['norm2_g'], sc2, sh2)


def _router_kernel(h_ref, rw_ref, o_ref):
    logits = lax.dot_general(rw_ref[...], h_ref[0], (((1,), (1,)), ((), ())), precision=HIGHEST,
                             preferred_element_type=F32)
    z = jnp.exp(logits - jnp.max(logits, axis=0, keepdims=True))
    o_ref[0] = z / jnp.sum(z, axis=0, keepdims=True)


def _router(h, rw_t):
    b, n, d = h.shape
    tm = _pick(n, 512)
    return pl.pallas_call(
        _router_kernel,
        out_shape=jax.ShapeDtypeStruct((b, N_EXPERTS, n), F32),
        grid=(b, n // tm),
        in_specs=[pl.BlockSpec((1, tm, d), lambda bi, i: (bi, i, 0)), pl.BlockSpec((N_EXPERTS, d), lambda bi, i: (0, 0))],
        out_specs=pl.BlockSpec((1, N_EXPERTS, tm), lambda bi, i: (bi, 0, i)),
        compiler_params=_cparams(("parallel", "parallel")),
        name="router_affinity",
    )(h, rw_t)


def _select_kernel(aff_ref, tri_ref, pos_o, cnt_o, *, cap, lb):
    aff = aff_ref[0]
    n = aff.shape[1]
    bits = pltpu.bitcast(aff, jnp.int32)
    count = lambda mask: jnp.sum(mask.astype(F32), axis=1, keepdims=True)

    def refine(i, lo):
        cand = lo | jnp.left_shift(jnp.int32(1), 30 - i)
        return jnp.where(count(bits >= cand) >= cap, cand, lo)

    thr = lax.fori_loop(0, 31, refine, jnp.zeros((N_EXPERTS, 1), jnp.int32))
    gt = bits > thr
    eq = bits == thr
    need = cap - count(gt)
    tri = tri_ref[...]

    def prefix(mask_of_block, write):
        off = jnp.zeros((N_EXPERTS, 1), F32)
        for j in range(n // lb):
            m = mask_of_block(j)
            excl = jnp.dot(m.astype(BF16), tri, preferred_element_type=F32) + off
            write(j, m, excl)
            off = off + jnp.sum(m.astype(F32), axis=1, keepdims=True)

    blk = lambda x, j: x[:, j * lb:(j + 1) * lb]
    sel_blocks = [None] * (n // lb)

    def write_sel(j, m, excl):
        sel_blocks[j] = blk(gt, j) | (m & (excl < need))

    prefix(lambda j: blk(eq, j), write_sel)

    def write_pos(j, m, excl):
        c = excl.astype(jnp.int32)
        cnt_o[0, :, j * lb:(j + 1) * lb] = c
        pos_o[0, :, j * lb:(j + 1) * lb] = jnp.where(m, c, -1)

    prefix(lambda j: sel_blocks[j], write_pos)


def _select(aff, cap):
    b, e, n = aff.shape
    lb = min(n, LANES)
    tri = jnp.asarray(np.triu(np.ones((lb, lb), np.float32), 1), BF16)
    blk = pl.BlockSpec((1, e, n), lambda bi: (bi, 0, 0))
    out = jax.ShapeDtypeStruct((b, e, n), jnp.int32)
    return pl.pallas_call(
        functools.partial(_select_kernel, cap=cap, lb=lb),
        out_shape=(out, out),
        grid=(b,),
        in_specs=[blk, pl.BlockSpec((lb, lb), lambda bi: (0, 0))],
        out_specs=[blk, blk],
        compiler_params=_cparams(("parallel",)),
        name="expert_select",
    )(aff, tri)


GATHER_CHUNK = 64


def _expert_kernel(starts_ref, pos_ref, aff_ref, h_ref, w1_ref, w3_ref, w2_ref, y_o, xin_s, gate_s,
                   *, cap, tile, ntile, nchunk):
    e, bi, t = pl.program_id(0), pl.program_id(1), pl.program_id(2)

    @pl.when(t == 0)
    def _():
        xin_s[...] = jnp.zeros(xin_s.shape, F32)
        gate_s[...] = jnp.zeros(gate_s.shape, F32)

    base = (e * pl.num_programs(1) + bi) * (ntile + 1) + t
    start, end = starts_ref[base], starts_ref[base + 1]
    w0 = (start // SUBLANES) * SUBLANES
    affb = jnp.broadcast_to(aff_ref[0, 0], (SUBLANES, tile))
    for c in range(nchunk):
        r0 = pl.multiple_of(w0 + c * GATHER_CHUNK, SUBLANES)

        @pl.when((r0 < end) & (r0 + GATHER_CHUNK > start))
        def _():
            slot = r0 + lax.broadcasted_iota(jnp.int32, (GATHER_CHUNK, tile), 0)
            onehot = pos_ref[0, 0] == slot
            xin_s[pl.ds(r0, GATHER_CHUNK), :] += jnp.dot(onehot.astype(BF16), h_ref[0], preferred_element_type=F32)
            gate_s[pl.ds(r0, GATHER_CHUNK), :] += lax.dot_general(
                onehot.astype(F32), affb, (((1,), (1,)), ((), ())), precision=HIGHEST, preferred_element_type=F32)

    @pl.when(t == ntile - 1)
    def _():
        rc = min(cap, SLOT_BLOCK)
        capp = y_o.shape[2]
        if capp > cap:
            y_o[0, 0, cap:capp, :] = jnp.zeros((capp - cap, y_o.shape[3]), BF16)
        for c in range(cap // rc):
            x = xin_s[c * rc:(c + 1) * rc, :].astype(BF16)
            hid = _silu(jnp.dot(x, w1_ref[0], preferred_element_type=F32)) * jnp.dot(x, w3_ref[0], preferred_element_type=F32)
            y = jnp.dot(hid.astype(BF16), w2_ref[0], preferred_element_type=F32)
            y_o[0, 0, c * rc:(c + 1) * rc, :] = (y * gate_s[c * rc:(c + 1) * rc, 0:1]).astype(BF16)


def _moe_tile(n):
    return _pick(n, 256)


def _experts(starts, pos4, aff4, h_bf16, w1, w3, w2, cap):
    b, n, d = h_bf16.shape
    tile = _moe_tile(n)
    ntile = n // tile
    nchunk = tile // GATHER_CHUNK + 1
    rows = cap + nchunk * GATHER_CHUNK
    capp = -(-cap // SLOT_BLOCK) * SLOT_BLOCK
    f = w1.shape[2]
    grid_spec = pltpu.PrefetchScalarGridSpec(
        num_scalar_prefetch=1,
        grid=(N_EXPERTS, b, ntile),
        in_specs=[pl.BlockSpec((1, 1, 1, tile), lambda e, bi, t, s: (bi, e, 0, t)),
                  pl.BlockSpec((1, 1, 1, tile), lambda e, bi, t, s: (bi, e, 0, t)),
                  pl.BlockSpec((1, tile, d), lambda e, bi, t, s: (bi, t, 0)),
                  pl.BlockSpec((1, d, f), lambda e, bi, t, s: (e, 0, 0)),
                  pl.BlockSpec((1, d, f), lambda e, bi, t, s: (e, 0, 0)),
                  pl.BlockSpec((1, f, d), lambda e, bi, t, s: (e, 0, 0))],
        out_specs=pl.BlockSpec((1, 1, capp, d), lambda e, bi, t, s: (e, bi, 0, 0)),
        scratch_shapes=[pltpu.VMEM((rows, d), F32), pltpu.VMEM((rows, SUBLANES), F32)])
    return pl.pallas_call(
        functools.partial(_expert_kernel, cap=cap, tile=tile, ntile=ntile, nchunk=nchunk),
        out_shape=jax.ShapeDtypeStruct((N_EXPERTS, b, capp, d), BF16),
        grid_spec=grid_spec,
        compiler_params=_cparams(("arbitrary", "arbitrary", "arbitrary")),
        name="expert_gather_ffn",
    )(starts, pos4, aff4, h_bf16, w1, w3, w2)


def _combine_kernel(starts_ref, pos_ref, ya_ref, yb_ref, x_ref, g2_ref, o_ref, *, tile, tpg, ntile, sb):
    bi, gi, e, tt = pl.program_id(0), pl.program_id(1), pl.program_id(2), pl.program_id(3)

    @pl.when((e == 0) & (tt == 0))
    def _():
        o_ref[...] = jnp.zeros(o_ref.shape, F32)

    base = (e * pl.num_programs(0) + bi) * (ntile + 1) + gi * tpg + tt
    start, end = starts_ref[base], starts_ref[base + 1]
    k = start // sb
    rows = pl.ds(pl.multiple_of(tt * tile, tile), tile)
    for half, y_ref in enumerate((ya_ref, yb_ref)):
        s0 = (k + half) * sb

        @pl.when((s0 < end) & (end > start))
        def _():
            slot = s0 + lax.broadcasted_iota(jnp.int32, (sb, tile), 0)
            onehot_t = (pos_ref[0, 0] == slot).astype(F32).T.astype(BF16)
            o_ref[0, rows, :] += jnp.dot(onehot_t, y_ref[0, 0], preferred_element_type=F32)

    @pl.when((e == pl.num_programs(2) - 1) & (tt == tpg - 1))
    def _():
        o_ref[0] = x_ref[0] + g2_ref[0] * o_ref[0]


def _combine(starts, pos4, y, x, g2):
    b, n, d = x.shape
    tile = _moe_tile(n)
    ntile = n // tile
    gt = _pick(n, 1024)
    tpg = gt // tile
    sb = SLOT_BLOCK
    nsb = y.shape[2] // sb

    def yblk(half):
        def imap(bi, gi, e, tt, s):
            k = s[(e * b + bi) * (ntile + 1) + gi * tpg + tt] // sb
            return (e, bi, jnp.minimum(k + half, nsb - 1), 0)
        return pl.BlockSpec((1, 1, sb, d), imap)

    grid_spec = pltpu.PrefetchScalarGridSpec(
        num_scalar_prefetch=1,
        grid=(b, n // gt, N_EXPERTS, tpg),
        in_specs=[pl.BlockSpec((1, 1, 1, tile), lambda bi, gi, e, tt, s: (bi, e, 0, gi * tpg + tt)),
                  yblk(0), yblk(1),
                  pl.BlockSpec((1, gt, d), lambda bi, gi, e, tt, s: (bi, gi, 0)),
                  pl.BlockSpec((1, 1, d), lambda bi, gi, e, tt, s: (bi, 0, 0))],
        out_specs=pl.BlockSpec((1, gt, d), lambda bi, gi, e, tt, s: (bi, gi, 0)),
        scratch_shapes=[])
    return pl.pallas_call(
        functools.partial(_combine_kernel, tile=tile, tpg=tpg, ntile=ntile, sb=sb),
        out_shape=jax.ShapeDtypeStruct((b, n, d), F32),
        grid_spec=grid_spec,
        compiler_params=_cparams(("parallel", "parallel", "arbitrary", "arbitrary")),
        name="expert_combine",
    )(starts, pos4, y, y, x, g2)


def _moe(h_f32, h_bf16, x, g2, lw):
    b, n, _ = x.shape
    cap = CAPACITY_FACTOR * n // N_EXPERTS
    aff = _router(h_f32, lw['router_w_t'])
    pos, cnt = _select(aff, cap)
    tile = _moe_tile(n)
    starts = jnp.concatenate([cnt[:, :, ::tile], jnp.full((b, N_EXPERTS, 1), cap, jnp.int32)], axis=2)
    starts = jnp.transpose(starts, (1, 0, 2)).reshape(-1)
    pos4 = pos.reshape(b, N_EXPERTS, 1, n)
    aff4 = aff.reshape(b, N_EXPERTS, 1, n)
    y = _experts(starts, pos4, aff4, h_bf16, lw['w1_bf16'], lw['w3_bf16'], lw['w2_bf16'], cap)
    return _combine(starts, pos4, y, x, g2)


def _pad_w_in(w_in):
    d = w_in.shape[0]
    sizes = (A_WIDTH,) * 3 + (B_WIDTH,) * 3 + (B_DECAY_RANK, B_ICL_RANK, B_GATE_RANK) + (C_WIDTH,) * 3 + (N_BRANCH * D_MODEL,)
    cuts = np.cumsum((0,) + sizes)
    piece = lambda lo, hi: w_in[:, int(cuts[lo]):int(cuts[hi])]
    zeros = jnp.zeros((d, COL_GATES - COL_LOWRANK - 256), w_in.dtype)
    out = jnp.concatenate([piece(0, 3), piece(3, 6), piece(9, 12), piece(6, 9), zeros, piece(12, 13)], axis=1)
    assert out.shape[1] == N_IN_PAD
    return out.astype(BF16)


def _rows_at(w, lo, total=256):
    pad = [(0, 0)] * w.ndim
    pad[-2] = (lo, total - lo - w.shape[-2])
    return jnp.pad(w, pad)


def _rope_tables(rows):
    half = A_HD // 2
    inv = jnp.power(ROPE_THETA, -jnp.arange(0, half, 2, dtype=F32) / half)
    r = jnp.repeat(jnp.arange(rows, dtype=F32), GRID_W)
    col = jnp.tile(jnp.arange(GRID_W, dtype=F32), rows)
    ang = jnp.concatenate([r[:, None] * inv, col[:, None] * inv], axis=-1)
    cos, sin = jnp.cos(ang), jnp.sin(ang)
    reps = A_WIDTH // A_HD
    cos_t = jnp.tile(jnp.concatenate([cos, cos], axis=-1), (1, reps))
    sin_t = jnp.tile(jnp.concatenate([-sin, sin], axis=-1), (1, reps))
    return cos_t, sin_t


def _layer_weights(i, norm2_g, w_in, q_norm_g, k_norm_g, diff_lambda, diff_subln_g, rwkv_conv_w, rwkv_w0, rwkv_w_up,
                   rwkv_a0, rwkv_a_up, rwkv_g_up, rwkv_k_k, rwkv_k_a, rwkv_r_k, rwkv_ln_g, rwkv_ln_b, conv_w,
                   w_branch, w_out, router_w, exp_w1, exp_w3, exp_w2):
    row = lambda v: v.reshape(1, -1)
    return {
        'norm2_g': row(norm2_g[i]),
        'w_in_pad': _pad_w_in(w_in[i]),
        'q_g': row(jnp.tile(q_norm_g[i], A_WIDTH // A_HD)), 'k_g': row(jnp.tile(k_norm_g[i], A_WIDTH // A_HD)),
        'diff_lambda': diff_lambda[i], 'subln_g': row(diff_subln_g[i]),
        'rwkv_conv_w': rwkv_conv_w[i], 'rwkv_w0': rwkv_w0[i], 'rwkv_a0': rwkv_a0[i],
        'w_up_pad': _rows_at(rwkv_w_up[i], 0), 'a_up_pad': _rows_at(rwkv_a_up[i], B_DECAY_RANK),
        'g_up_pad': _rows_at(rwkv_g_up[i], B_DECAY_RANK + B_ICL_RANK),
        'rwkv_k_k': row(rwkv_k_k[i]), 'rwkv_k_a': row(rwkv_k_a[i]), 'rwkv_r_k': row(rwkv_r_k[i]),
        'rwkv_ln_g': row(rwkv_ln_g[i]), 'rwkv_ln_b': row(rwkv_ln_b[i]),
        'conv_w': conv_w[i], 'w_branch_bf16': w_branch[i].astype(BF16), 'w_out_bf16': w_out[i].astype(BF16),
        'router_w_t': router_w[i].T,
        'w1_bf16': exp_w1[i].astype(BF16), 'w3_bf16': exp_w3[i].astype(BF16), 'w2_bf16': exp_w2[i].astype(BF16),
        'ones64': _group_ones(B_WIDTH, B_HD),
    }


def _mixer_inputs(x, norm1_g, sc1, sh1, lw, cos_t, sin_t):
    p = _norm_mod_proj(x, norm1_g, sc1, sh1, lw['w_in_pad'])
    q, k, v = _qkv_prep(p, lw['q_g'], lw['k_g'], lw['ones64'], cos_t, sin_t)
    names = ('r', 'v', 'kk', 'g', 'bonus', 'dec0', 'b0', 'kd0', 'dec1', 'b1', 'kd1')
    prep = dict(zip(names, _rwkv_prep(p, lw)))
    return p, (q, k, v), prep


def kernel(x, c, ctx, c_ctx, norm1_g, norm2_g, ada_w, ada_b, w_in, q_norm_g, k_norm_g, diff_lambda, diff_subln_g,
           rwkv_conv_w, rwkv_w0, rwkv_w_up, rwkv_a0, rwkv_a_up, rwkv_g_up, rwkv_k_k, rwkv_k_a, rwkv_r_k, rwkv_ln_g,
           rwkv_ln_b, conv_w, w_branch, w_out, router_w, exp_w1, exp_w3, exp_w2):
    bsz, n_lat, d = x.shape
    depth = ada_w.shape[0]
    cos_t, sin_t = _rope_tables(n_lat // GRID_W)

    cond = jnp.concatenate([c, jnp.broadcast_to(c_ctx[None, :], (SUBLANES - bsz % SUBLANES, d))], axis=0)
    ada = _ada_params(cond, ada_w, ada_b)

    xc, xl = ctx, x
    for i in range(depth):
        lam_init = 0.8 - 0.6 * math.exp(-0.3 * i)
        need_ctx = i != depth - 1
        lw = _layer_weights(i, norm2_g, w_in, q_norm_g, k_norm_g, diff_lambda, diff_subln_g, rwkv_conv_w, rwkv_w0,
                            rwkv_w_up, rwkv_a0, rwkv_a_up, rwkv_g_up, rwkv_k_k, rwkv_k_a, rwkv_r_k, rwkv_ln_g,
                            rwkv_ln_b, conv_w, w_branch, w_out, router_w, exp_w1, exp_w3, exp_w2)
        mods = ada[i].reshape(-1, 6, d)
        lat = [mods[:bsz, j][:, None, :] for j in range(6)]
        cxt = [jnp.broadcast_to(mods[bsz, j][None, None, :], (bsz, 1, d)) for j in range(6)]

        pc, (qc, kc, vc), prep_c = _mixer_inputs(xc, norm1_g[i], cxt[1], cxt[0], lw, None, None)
        pl_, (ql, kl, vl), prep_l = _mixer_inputs(xl, norm1_g[i], lat[1], lat[0], lw, cos_t, sin_t)

        al = _flash_attention(lw['diff_lambda'], lw['subln_g'], ql, kc, vc, kl, vl, lam_init)
        o0, o1 = _rwkv_bidirectional(prep_c, prep_l)
        n_ctx = xc.shape[1]

        xl, hl, hlb = _merge(al, o0, o1, n_ctx, prep_l['bonus'], prep_l['g'], pl_, xl, lw, lat[2], lat[4], lat[3])
        xl = _moe(hl, hlb, xl, lat[5], lw)
        if need_ctx:
            ac = _ctx_attention(lw['diff_lambda'], lw['subln_g'], qc, kc, vc, lam_init)
            xc, hc, hcb = _merge(ac, o0, o1, 0, prep_c['bonus'], prep_c['g'], pc, xc, lw, cxt[2], cxt[4], cxt[3])
            xc = _moe(hc, hcb, xc, cxt[5], lw)
    return xl
```

```python
import functools
import math

import numpy as np
import jax
import jax.numpy as jnp
from jax import lax
from jax.experimental import pallas as pl
from jax.experimental.pallas import tpu as pltpu

F32 = jnp.float32
BF16 = jnp.bfloat16
HIGHEST = lax.Precision.HIGHEST

D_MODEL = 1024
GRID_W = 64
A_HEADS = 4
A_HD = 64
A_WIDTH = A_HEADS * 2 * A_HD
B_HEADS = 8
B_HD = 64
B_WIDTH = B_HEADS * B_HD
B_DECAY_RANK = 64
B_ICL_RANK = 64
B_GATE_RANK = 128
C_WIDTH = 512
N_BRANCH = 3
N_EXPERTS = 16
EXPERT_FF = 1024
CAPACITY_FACTOR = 2
ROPE_THETA = 10000.0
NORM_EPS = 1e-6
GN_EPS = 64e-5
LOG2_E = math.log2(math.e)

COL_ATTN = 0
COL_RWKV = 1536
COL_CONV = 3072
COL_LOWRANK = 4608
COL_GATES = 5120
N_IN_PAD = 8192

VMEM_LIMIT = 56 * 1024 * 1024
LANES = 128
SUBLANES = 8
SLOT_BLOCK = 256


def _cparams(sem):
    return pltpu.CompilerParams(dimension_semantics=sem, vmem_limit_bytes=VMEM_LIMIT)


def _pick(n, pref):
    t = min(n, pref)
    while n % t:
        t -= SUBLANES
    return t


def _group_ones(width, group):
    idx = np.arange(width) // group
    return jnp.asarray((idx[:, None] == idx[None, :]).astype(np.float32))


def _silu(x):
    return x * jax.nn.sigmoid(x)


def _split2(x):
    hi = x.astype(BF16)
    return hi, (x - hi.astype(F32)).astype(BF16)


def _group_sum(x, ones_bf16):
    hi, lo = _split2(x)
    return (jnp.dot(hi, ones_bf16, preferred_element_type=F32) + jnp.dot(lo, ones_bf16, preferred_element_type=F32))


def _dot3(a, b):
    a_hi, a_lo = _split2(a)
    b_hi, b_lo = _split2(b)
    dot = lambda u, v: jnp.dot(u, v, preferred_element_type=F32)
    return dot(a_hi, b_hi) + (dot(a_hi, b_lo) + dot(a_lo, b_hi))


def _ada_kernel(cond_ref, w_ref, b_ref, o_ref):
    c = cond_ref[...]
    o_ref[0] = jnp.dot(_silu(c), w_ref[0], precision=HIGHEST, preferred_element_type=F32) + b_ref[0]


def _ada_params(cond, ada_w, ada_b):
    depth, d, n6 = ada_w.shape
    tn = _pick(n6, 1536)
    return pl.pallas_call(
        _ada_kernel,
        out_shape=jax.ShapeDtypeStruct((depth, cond.shape[0], n6), F32),
        grid=(depth, n6 // tn),
        in_specs=[pl.BlockSpec(cond.shape, lambda i, j: (0, 0)),
                  pl.BlockSpec((1, d, tn), lambda i, j: (i, 0, j)),
                  pl.BlockSpec((1, 1, tn), lambda i, j: (i, 0, j))],
        out_specs=pl.BlockSpec((1, cond.shape[0], tn), lambda i, j: (i, 0, j)),
        compiler_params=_cparams(("parallel", "parallel")),
        name="ada_params",
    )(cond, ada_w, ada_b.reshape(depth, 1, n6))


def _proj_kernel(x_ref, g_ref, sc_ref, sh_ref, w_ref, o_ref):
    x = x_ref[0]
    ms = jnp.mean(x * x, axis=-1, keepdims=True)
    y = x * lax.rsqrt(ms + NORM_EPS) * g_ref[...]
    h = y * (1.0 + sc_ref[0]) + sh_ref[0]
    o_ref[0] = jnp.dot(h.astype(BF16), w_ref[...], preferred_element_type=F32).astype(BF16)


def _norm_mod_proj(x, g, sc, sh, w_bf16):
    b, n, d = x.shape
    npad = w_bf16.shape[1]
    tm = _pick(n, 512)
    tn = 2048
    return pl.pallas_call(
        _proj_kernel,
        out_shape=jax.ShapeDtypeStruct((b, n, npad), BF16),
        grid=(npad // tn, b, n // tm),
        in_specs=[pl.BlockSpec((1, tm, d), lambda j, bi, i: (bi, i, 0)),
                  pl.BlockSpec((1, d), lambda j, bi, i: (0, 0)),
                  pl.BlockSpec((1, 1, d), lambda j, bi, i: (bi, 0, 0)),
                  pl.BlockSpec((1, 1, d), lambda j, bi, i: (bi, 0, 0)),
                  pl.BlockSpec((d, tn), lambda j, bi, i: (0, j))],
        out_specs=pl.BlockSpec((1, tm, tn), lambda j, bi, i: (bi, i, j)),
        compiler_params=_cparams(("parallel", "parallel", "parallel")),
        name="norm_mod_proj",
    )(x, g.reshape(1, d), sc, sh, w_bf16)


def _swap_halves(x, half):
    width = x.shape[-1]
    lane = lax.broadcasted_iota(jnp.int32, (1, width), 1) % (2 * half)
    fwd = pltpu.roll(x, width - half, axis=1)
    bwd = pltpu.roll(x, half, axis=1)
    return jnp.where(lane < half, fwd, bwd)


def _qkv_prep_kernel(*refs, use_rope):
    if use_rope:
        q_ref, k_ref, qg_ref, kg_ref, ones_ref, cos_ref, sin_ref, qo_ref, ko_ref = refs
    else:
        q_ref, k_ref, qg_ref, kg_ref, ones_ref, qo_ref, ko_ref = refs

    def head_norm(x, gain):
        ms = _group_sum(x * x, ones_ref[...]) * (1.0 / A_HD)
        return x * lax.rsqrt(ms + NORM_EPS) * gain

    def rope(x):
        if not use_rope:
            return x
        return x * cos_ref[...] + _swap_halves(x, A_HD // 2) * sin_ref[...]

    q = rope(head_norm(q_ref[0].astype(F32), qg_ref[...]))
    k = rope(head_norm(k_ref[0].astype(F32), kg_ref[...]))
    qo_ref[0] = (q * (A_HD ** -0.5 * LOG2_E)).astype(BF16)
    ko_ref[0] = k.astype(BF16)


def _qkv_prep(p, qg, kg, ones, cos_t, sin_t):
    b, n, _ = p.shape
    tm = _pick(n, 512)
    use_rope = cos_t is not None
    w = A_WIDTH
    col = lambda c: pl.BlockSpec((1, tm, w), lambda bi, i, c=c: (bi, i, COL_ATTN // w + c))
    const = lambda shape: pl.BlockSpec(shape, lambda bi, i: (0,) * len(shape))
    in_specs = [col(0), col(1), const((1, w)), const((1, w)), const((w, w))]
    args = [p, p, qg, kg, ones]
    if use_rope:
        in_specs += [pl.BlockSpec((tm, w), lambda bi, i: (i, 0))] * 2
        args += [cos_t, sin_t]
    out = jax.ShapeDtypeStruct((b, n, w), BF16)
    return pl.pallas_call(
        functools.partial(_qkv_prep_kernel, use_rope=use_rope),
        out_shape=(out, out),
        grid=(b, n // tm),
        in_specs=in_specs,
        out_specs=[pl.BlockSpec((1, tm, w), lambda bi, i: (bi, i, 0))] * 2,
        compiler_params=_cparams(("parallel", "parallel")),
        name="qkv_prep_rope" if use_rope else "qkv_prep",
    )(*args)


def _map_stacked(q):
    lane = lax.broadcasted_iota(jnp.int32, (1, 2 * A_HD), 1)
    zero = jnp.zeros_like(q)
    return jnp.concatenate([jnp.where(lane < A_HD, q, zero), jnp.where(lane >= A_HD, q, zero)], axis=0)


def _lane_tiled(x, width):
    return jnp.concatenate([x] * (width // LANES), axis=1)


def _lane_block_sum(p):
    out = p[:, 0:LANES]
    for j in range(1, p.shape[1] // LANES):
        out = out + p[:, j * LANES:(j + 1) * LANES]
    return out


def _lambda_full(dl_ref, lam_init):
    dl = dl_ref[...]
    s1 = jnp.sum(dl[0:1] * dl[1:2], axis=-1, keepdims=True)
    s2 = jnp.sum(dl[2:3] * dl[3:4], axis=-1, keepdims=True)
    return jnp.exp(s1) - jnp.exp(s2) + lam_init


def _subln(o, sg_ref, lam_init):
    ms = jnp.mean(o * o, axis=-1, keepdims=True)
    return o * lax.rsqrt(ms + NORM_EPS) * sg_ref[...] * (1.0 - lam_init)


def _scores(qm, k):
    return lax.dot_general(qm, k, (((1,), (1,)), ((), ())), preferred_element_type=F32)


def _flash_kernel(dl_ref, sg_ref, q_ref, kc_ref, vc_ref, k_ref, v_ref, o_ref, m_s, l_s, acc_s, *, lam_init, nkv):
    kj = pl.program_id(3)
    tq = q_ref.shape[1]
    qs = _map_stacked(q_ref[0])

    def update(k, v):
        s = _scores(qs, k)
        m_prev = m_s[...]
        m_new = jnp.maximum(m_prev, jnp.max(s, axis=-1, keepdims=True))
        alpha = jnp.exp2(m_prev - m_new)
        p = jnp.exp2(s - _lane_tiled(m_new, s.shape[1]))
        l_s[...] = alpha * l_s[...] + _lane_block_sum(p)
        acc_s[...] = alpha * acc_s[...] + jnp.dot(p.astype(BF16), v, preferred_element_type=F32)
        m_s[...] = m_new

    @pl.when(kj == 0)
    def _():
        m_s[...] = jnp.full(m_s.shape, -jnp.inf, F32)
        l_s[...] = jnp.zeros(l_s.shape, F32)
        acc_s[...] = jnp.zeros(acc_s.shape, F32)
        update(kc_ref[0], vc_ref[0])

    update(k_ref[0], v_ref[0])

    @pl.when(kj == nkv - 1)
    def _():
        lam = _lambda_full(dl_ref, lam_init)
        o = acc_s[...] / jnp.sum(l_s[...], axis=-1, keepdims=True)
        o_ref[0] = _subln(o[0:tq] - lam * o[tq:2 * tq], sg_ref, lam_init).astype(o_ref.dtype)


V_COL_BLOCK = (COL_ATTN + 2 * A_WIDTH) // (2 * A_HD)


def _flash_attention(dl, sg, q, kc, vc, kl, vl, lam_init):
    b, s, _ = q.shape
    nc = kc.shape[1]
    hw = 2 * A_HD
    tq = _pick(s, 512)
    tk = _pick(s, 1024)
    nkv = s // tk
    return pl.pallas_call(
        functools.partial(_flash_kernel, lam_init=lam_init, nkv=nkv),
        out_shape=jax.ShapeDtypeStruct((b, s, A_WIDTH), BF16),
        grid=(b, A_HEADS, s // tq, nkv),
        in_specs=[pl.BlockSpec((4, A_HD), lambda bi, h, i, j: (0, 0)),
                  pl.BlockSpec((1, hw), lambda bi, h, i, j: (0, 0)),
                  pl.BlockSpec((1, tq, hw), lambda bi, h, i, j: (bi, i, h)),
                  pl.BlockSpec((1, nc, hw), lambda bi, h, i, j: (bi, 0, h)),
                  pl.BlockSpec((1, nc, hw), lambda bi, h, i, j: (bi, 0, V_COL_BLOCK + h)),
                  pl.BlockSpec((1, tk, hw), lambda bi, h, i, j: (bi, j, h)),
                  pl.BlockSpec((1, tk, hw), lambda bi, h, i, j: (bi, j, V_COL_BLOCK + h))],
        out_specs=pl.BlockSpec((1, tq, hw), lambda bi, h, i, j: (bi, i, h)),
        scratch_shapes=[pltpu.VMEM((2 * tq, LANES), F32), pltpu.VMEM((2 * tq, LANES), F32), pltpu.VMEM((2 * tq, hw), F32)],
        compiler_params=_cparams(("parallel", "parallel", "parallel", "arbitrary")),
        name="diff_flash_attention",
    )(dl, sg, q, kc, vc, kl, vl)


def _ctx_attn_kernel(dl_ref, sg_ref, q_ref, k_ref, v_ref, o_ref, *, lam_init):
    n = q_ref.shape[1]
    s = _scores(_map_stacked(q_ref[0]), k_ref[0])
    p = jnp.exp2(s - jnp.max(s, axis=-1, keepdims=True))
    o = jnp.dot(p.astype(BF16), v_ref[0], preferred_element_type=F32) / jnp.sum(p, axis=-1, keepdims=True)
    lam = _lambda_full(dl_ref, lam_init)
    o_ref[0] = _subln(o[0:n] - lam * o[n:2 * n], sg_ref, lam_init).astype(o_ref.dtype)


def _ctx_attention(dl, sg, q, k, v, lam_init):
    b, n, _ = q.shape
    hw = 2 * A_HD
    blk = pl.BlockSpec((1, n, hw), lambda bi, h: (bi, 0, h))
    vblk = pl.BlockSpec((1, n, hw), lambda bi, h: (bi, 0, V_COL_BLOCK + h))
    return pl.pallas_call(
        functools.partial(_ctx_attn_kernel, lam_init=lam_init),
        out_shape=jax.ShapeDtypeStruct((b, n, A_WIDTH), BF16),
        grid=(b, A_HEADS),
        in_specs=[pl.BlockSpec((4, A_HD), lambda bi, h: (0, 0)), pl.BlockSpec((1, hw), lambda bi, h: (0, 0)),
                  blk, blk, vblk],
        out_specs=blk,
        compiler_params=_cparams(("parallel", "parallel")),
        name="ctx_attention",
    )(dl, sg, q, k, v)


HALO_ROWS = 16


def _halo_specs(tm, width, colblk, n):
    rh = tm // HALO_ROWS
    last = n // HALO_ROWS - 1
    main = pl.BlockSpec((1, tm, width), lambda bi, i: (bi, i, colblk))
    prev = pl.BlockSpec((1, HALO_ROWS, width), lambda bi, i: (bi, jnp.maximum(i * rh - 1, 0), colblk))
    nxt = pl.BlockSpec((1, HALO_ROWS, width), lambda bi, i: (bi, jnp.minimum((i + 1) * rh, last), colblk))
    return [main, prev, nxt]


def _dwconv3(x, prev_row, next_row, w_ref):
    tm = x.shape[0]
    row = lax.broadcasted_iota(jnp.int32, (tm, 1), 0)
    xp = jnp.where(row == 0, prev_row, pltpu.roll(x, 1, axis=0))
    xn = jnp.where(row == tm - 1, next_row, pltpu.roll(x, tm - 1, axis=0))
    w = w_ref[...]
    return w[0:1] * xp + w[1:2] * x + w[2:3] * xn


def _edge_rows(prev_ref, next_ref):
    i = pl.program_id(1)
    n = pl.num_programs(1)
    pr = prev_ref[0].astype(F32)[HALO_ROWS - 1:HALO_ROWS]
    nr = next_ref[0].astype(F32)[0:1]
    pr = jnp.where(i == 0, jnp.zeros_like(pr), pr)
    nr = jnp.where(i == n - 1, jnp.zeros_like(nr), nr)
    return pr, nr


def _softplus(u):
    return jnp.maximum(u, 0.0) + jnp.log(1.0 + jnp.exp(-jnp.abs(u)))


def _rwkv_prep_kernel(rkv_ref, prev_ref, next_ref, lr_ref, cw_ref, w0_ref, wup_ref, a0_ref, aup_ref, gup_ref,
                      kk_ref_w, ka_ref, rk_ref, ones_ref,
                      r_o, v_o, kk_o, g_o, bonus_o, dec0_o, b0_o, kd0_o, dec1_o, b1_o, kd1_o):
    pr, nr = _edge_rows(prev_ref, next_ref)
    rkv = _dwconv3(rkv_ref[0].astype(F32), pr, nr, cw_ref)
    w = B_WIDTH
    r, k, v = rkv[:, 0:w], rkv[:, w:2 * w], rkv[:, 2 * w:3 * w]
    gsum = lambda t: _group_sum(t, ones_ref[...])
    kkr = k * kk_ref_w[...]
    nrm = jnp.maximum(jnp.sqrt(gsum(kkr * kkr)), 1e-12)
    kk = kkr / nrm
    lr = lr_ref[0].astype(F32)
    th = jnp.tanh(lr)
    sg = jax.nn.sigmoid(lr)
    r_o[0] = r
    v_o[0] = v
    kk_o[0] = kk
    g_o[0] = _dot3(sg, gup_ref[...])
    kd_sum = jnp.zeros_like(k)
    outs = ((dec0_o, b0_o, kd0_o), (dec1_o, b1_o, kd1_o))
    for d in range(2):
        z = w0_ref[d:d + 1] + _dot3(th, wup_ref[d])
        wlog = -_softplus(-z) - 0.5
        dec = jnp.exp(-jnp.exp(wlog))
        a = jax.nn.sigmoid(a0_ref[d:d + 1] + _dot3(lr, aup_ref[d]))
        kd = k * (1.0 + (a - 1.0) * ka_ref[...])
        outs[d][0][0] = dec
        outs[d][1][0] = kk * a
        outs[d][2][0] = kd
        kd_sum = kd_sum + kd
    bonus_o[0] = gsum(r * kd_sum * rk_ref[...]) * v


def _rwkv_prep(p, lw):
    b, n, _ = p.shape
    tm = _pick(n, 256)
    w = B_WIDTH
    w3 = 3 * w
    const = lambda shape: pl.BlockSpec(shape, lambda bi, i: (0,) * len(shape))
    in_specs = _halo_specs(tm, w3, COL_RWKV // w3, n) + [
        pl.BlockSpec((1, tm, 256), lambda bi, i: (bi, i, COL_LOWRANK // 256)),
        const((3, w3)), const((2, w)), const((2, 256, w)), const((2, w)), const((2, 256, w)), const((256, w)),
        const((1, w)), const((1, w)), const((1, w)), const((w, w))]
    out = jax.ShapeDtypeStruct((b, n, w), F32)
    return pl.pallas_call(
        _rwkv_prep_kernel,
        out_shape=(out,) * 11,
        grid=(b, n // tm),
        in_specs=in_specs,
        out_specs=[pl.BlockSpec((1, tm, w), lambda bi, i: (bi, i, 0))] * 11,
        compiler_params=_cparams(("parallel", "parallel")),
        name="rwkv_prep",
    )(p, p, p, p, lw['rwkv_conv_w'], lw['rwkv_w0'], lw['w_up_pad'], lw['rwkv_a0'], lw['a_up_pad'], lw['g_up_pad'],
      lw['rwkv_k_k'], lw['rwkv_k_a'], lw['rwkv_r_k'], lw['ones64'])


V_SLABS = B_HD // 2 // SUBLANES


def _scan_kernel(w_ref, b_ref, kd_ref, r_ref, kk_ref, kkx_ref, v_ref, y_ref, s_ref, sa_ref, *, tc):
    @pl.when(pl.program_id(0) == 0)
    def _():
        s_ref[...] = jnp.zeros(s_ref.shape, F32)
        sa_ref[...] = jnp.zeros(sa_ref.shape, F32)

    lanes = s_ref.shape[-1]
    vrows = B_HD // 2
    bcast = lambda row: jnp.broadcast_to(row, (SUBLANES, lanes))

    def step(t, last_of_block):
        base = pl.multiple_of(t * vrows, vrows)
        sa = [sa_ref[s] for s in range(V_SLABS)]
        vv = [v_ref[pl.ds(base + SUBLANES * s, SUBLANES), :] for s in range(V_SLABS)]
        y = [jnp.zeros((SUBLANES, lanes), F32) for _ in range(V_SLABS)]
        san = [jnp.zeros((SUBLANES, lanes), F32) for _ in range(V_SLABS)]
        for k in range(B_HD):
            row = lambda ref: bcast(ref[k, pl.ds(t, 1), :])
            wk, bk, kdk, rk = row(w_ref), row(b_ref), row(kd_ref), row(r_ref)
            kkn = bcast(kkx_ref[k, 0:1, :]) if last_of_block else bcast(kk_ref[k, pl.ds(t + 1, 1), :])
            for s in range(V_SLABS):
                new = s_ref[s, k] * wk - sa[s] * bk + vv[s] * kdk
                s_ref[s, k] = new
                y[s] = y[s] + new * rk
                san[s] = san[s] + new * kkn
        for s in range(V_SLABS):
            y_ref[pl.ds(base + SUBLANES * s, SUBLANES), :] = y[s]
            sa_ref[s] = san[s]

    def body(t, carry):
        step(t, False)
        return carry

    lax.fori_loop(0, tc - 1, body, 0)
    step(tc - 1, True)


def _rwkv_scan(w, bb, kd, r, kk, v):
    _, t, lanes = w.shape
    tc = _pick(t, 64)
    vrows = B_HD // 2
    last8 = t // SUBLANES - 1
    kspec = pl.BlockSpec((B_HD, tc, lanes), lambda i: (0, i, 0))
    kxspec = pl.BlockSpec((B_HD, SUBLANES, lanes), lambda i: (0, jnp.minimum((i + 1) * (tc // SUBLANES), last8), 0))
    vspec = pl.BlockSpec((tc * vrows, lanes), lambda i: (i, 0))
    return pl.pallas_call(
        functools.partial(_scan_kernel, tc=tc),
        out_shape=jax.ShapeDtypeStruct(v.shape, F32),
        grid=(t // tc,),
        in_specs=[kspec] * 5 + [kxspec, vspec],
        out_specs=vspec,
        scratch_shapes=[pltpu.VMEM((V_SLABS, B_HD, SUBLANES, lanes), F32), pltpu.VMEM((V_SLABS, SUBLANES, lanes), F32)],
        compiler_params=_cparams(("arbitrary",)),
        name="rwkv_scan",
    )(w, bb, kd, r, kk, kk, v)


TIME_BLOCK = 128
HEAD_PAIRS = B_WIDTH // LANES


def _scan_block_maps(nctx, nlat):
    fwd_c = lambda i: jnp.minimum(i, nctx - 1)
    fwd_l = lambda i: jnp.clip(i - nctx, 0, nlat - 1)
    bwd_c = lambda i: jnp.clip(nctx - 1 - i, 0, nctx - 1)
    bwd_l = lambda i: jnp.clip(nlat - 1 - (i - nctx), 0, nlat - 1)
    return fwd_c, fwd_l, bwd_c, bwd_l


def _fill_transposed(u_ref, src_ref, direction, rev_ref, bsz):
    for b in range(bsz):
        for j in range(HEAD_PAIRS):
            tile = src_ref[b, :, j * LANES:(j + 1) * LANES]
            if direction == 1:
                tile = jnp.dot(rev_ref[...], tile, precision=HIGHEST, preferred_element_type=F32)
            q = (direction * bsz + b) * HEAD_PAIRS + j
            u_ref[q * LANES:(q + 1) * LANES, :] = tile.T


def _fill_both(u_ref, cf_ref, lf_ref, cb_ref, lb_ref, rev_ref, nctx, bsz):
    i = pl.program_id(0)

    @pl.when(i < nctx)
    def _():
        _fill_transposed(u_ref, cf_ref, 0, rev_ref, bsz)
        _fill_transposed(u_ref, cb_ref, 1, rev_ref, bsz)

    @pl.when(i >= nctx)
    def _():
        _fill_transposed(u_ref, lf_ref, 0, rev_ref, bsz)
        _fill_transposed(u_ref, lb_ref, 1, rev_ref, bsz)


def _to_scan_k_kernel(cf_ref, lf_ref, cb_ref, lb_ref, rev_ref, o_ref, u_ref, *, nctx, bsz):
    _fill_both(u_ref, cf_ref, lf_ref, cb_ref, lb_ref, rev_ref, nctx, bsz)
    nrow = u_ref.shape[0] // B_HD
    for k in range(B_HD):
        g = u_ref[pl.ds(k, nrow, stride=B_HD), :]
        o_ref[k] = jnp.concatenate([g, g], axis=0).T


def _to_scan_v_kernel(cf_ref, lf_ref, cb_ref, lb_ref, rev_ref, o_ref, u_ref, *, nctx, bsz):
    _fill_both(u_ref, cf_ref, lf_ref, cb_ref, lb_ref, rev_ref, nctx, bsz)
    nrow = u_ref.shape[0] // B_HD
    vrows = B_HD // 2
    for vi in range(vrows):
        halves = [u_ref[pl.ds(half * vrows + vi, nrow, stride=B_HD), :] for half in range(2)]
        o_ref[pl.ds(vi, TIME_BLOCK, stride=vrows), :] = jnp.concatenate(halves, axis=0).T


def _to_scan(ctx_f, lat_f, ctx_b, lat_b, rev, values):
    bsz, nc, width = ctx_f.shape
    nl = lat_f.shape[1]
    nctx, nlat = nc // TIME_BLOCK, nl // TIME_BLOCK
    lanes = 4 * bsz * B_HEADS
    t = nc + nl
    maps = _scan_block_maps(nctx, nlat)
    src = lambda f: pl.BlockSpec((bsz, TIME_BLOCK, width), lambda i, f=f: (0, f(i), 0))
    if values:
        kern, out_shape = _to_scan_v_kernel, (t * (B_HD // 2), lanes)
        out_spec = pl.BlockSpec((TIME_BLOCK * (B_HD // 2), lanes), lambda i: (i, 0))
    else:
        kern, out_shape = _to_scan_k_kernel, (B_HD, t, lanes)
        out_spec = pl.BlockSpec((B_HD, TIME_BLOCK, lanes), lambda i: (0, i, 0))
    return pl.pallas_call(
        functools.partial(kern, nctx=nctx, bsz=bsz),
        out_shape=jax.ShapeDtypeStruct(out_shape, F32),
        grid=(nctx + nlat,),
        in_specs=[src(maps[0]), src(maps[1]), src(maps[2]), src(maps[3]),
                  pl.BlockSpec((TIME_BLOCK, TIME_BLOCK), lambda i: (0, 0))],
        out_specs=out_spec,
        scratch_shapes=[pltpu.VMEM((2 * bsz * HEAD_PAIRS * LANES, TIME_BLOCK), F32)],
        compiler_params=_cparams(("parallel",)),
        name="to_scan_v" if values else "to_scan_k",
    )(ctx_f, lat_f, ctx_b, lat_b, rev)


def _from_scan_kernel(y_ref, rev_ref, o0_ref, o1_ref, u_ref, *, bsz):
    nrow = u_ref.shape[0] // B_HD
    vrows = B_HD // 2
    for vi in range(vrows):
        mt = y_ref[pl.ds(vi, TIME_BLOCK, stride=vrows), :].T
        for half in range(2):
            u_ref[pl.ds(half * vrows + vi, nrow, stride=B_HD), :] = mt[half * nrow:(half + 1) * nrow]
    for direction, o_ref in enumerate((o0_ref, o1_ref)):
        for b in range(bsz):
            for j in range(HEAD_PAIRS):
                q = (direction * bsz + b) * HEAD_PAIRS + j
                tile = u_ref[q * LANES:(q + 1) * LANES, :].T
                if direction == 1:
                    tile = jnp.dot(rev_ref[...], tile, precision=HIGHEST, preferred_element_type=F32)
                o_ref[b, :, j * LANES:(j + 1) * LANES] = tile


def _from_scan(y, rev, bsz, nc, nl):
    nctx, nlat = nc // TIME_BLOCK, nl // TIME_BLOCK
    lanes = y.shape[1]
    vrows = B_HD // 2
    bwd = lambda i: jnp.where(i < nctx, nctx - 1 - i, 2 * nctx + nlat - 1 - i)
    out = jax.ShapeDtypeStruct((bsz, nc + nl, B_WIDTH), F32)
    return pl.pallas_call(
        functools.partial(_from_scan_kernel, bsz=bsz),
        out_shape=(out, out),
        grid=(nctx + nlat,),
        in_specs=[pl.BlockSpec((TIME_BLOCK * vrows, lanes), lambda i: (i, 0)),
                  pl.BlockSpec((TIME_BLOCK, TIME_BLOCK), lambda i: (0, 0))],
        out_specs=[pl.BlockSpec((bsz, TIME_BLOCK, B_WIDTH), lambda i: (0, i, 0)),
                   pl.BlockSpec((bsz, TIME_BLOCK, B_WIDTH), lambda i: (0, bwd(i), 0))],
        scratch_shapes=[pltpu.VMEM((2 * bsz * HEAD_PAIRS * LANES, TIME_BLOCK), F32)],
        compiler_params=_cparams(("parallel",)),
        name="from_scan",
    )(y, rev)


def _rwkv_bidirectional(prep_c, prep_l):
    bsz, nc, _ = prep_c['r'].shape
    nl = prep_l['r'].shape[1]
    rev = jnp.asarray(np.eye(TIME_BLOCK, dtype=np.float32)[::-1].copy())
    pair = lambda fwd, bwd, values=False: _to_scan(prep_c[fwd], prep_l[fwd], prep_c[bwd], prep_l[bwd], rev, values)
    y = _rwkv_scan(pair('dec0', 'dec1'), pair('b0', 'b1'), pair('kd0', 'kd1'), pair('r', 'r'), pair('kk', 'kk'),
                   pair('v', 'v', True))
    return _from_scan(y, rev, bsz, nc, nl)


def _merge_kernel(a_ref, o0_ref, o1_ref, bonus_ref, g_ref, cv_ref, cprev_ref, cnext_ref, ga_ref, gb_ref, gc_ref,
                  x_ref, lng_ref, lnb_ref, cw_ref, wb_ref, wo_ref, ones_ref, g1_ref, n2_ref, sc_ref, sh_ref,
                  x_o, hf_o, hb_o):
    gmean = lambda t: _group_sum(t, ones_ref[...]) * (1.0 / B_HD)

    def group_norm(o):
        dlt = o - gmean(o)
        return dlt * lax.rsqrt(gmean(dlt * dlt) + GN_EPS) * lng_ref[...] + lnb_ref[...]

    yb = (group_norm(o0_ref[0]) + group_norm(o1_ref[0]) + bonus_ref[0]) * g_ref[0]

    w = C_WIDTH
    u = lambda blk: blk[:, w:2 * w] * blk[:, 2 * w:3 * w]
    cv = cv_ref[0].astype(F32)
    i = pl.program_id(1)
    pr = u(cprev_ref[0].astype(F32)[HALO_ROWS - 1:HALO_ROWS])
    nr = u(cnext_ref[0].astype(F32)[0:1])
    pr = jnp.where(i == 0, jnp.zeros_like(pr), pr)
    nr = jnp.where(i == pl.num_programs(1) - 1, jnp.zeros_like(nr), nr)
    yc = cv[:, 0:w] * _dwconv3(u(cv), pr, nr, cw_ref)

    d = D_MODEL
    m = jnp.zeros((cv.shape[0], d), F32)
    for j, (yj, gate_ref) in enumerate(((a_ref[0], ga_ref), (yb, gb_ref), (yc, gc_ref))):
        gate = jax.nn.sigmoid(gate_ref[0].astype(F32))
        m = m + gate * jnp.dot(yj.astype(BF16), wb_ref[j], preferred_element_type=F32)
    ymix = jnp.dot(m.astype(BF16), wo_ref[...], preferred_element_type=F32)
    x1 = x_ref[0] + g1_ref[0] * ymix
    x_o[0] = x1
    ms = jnp.mean(x1 * x1, axis=-1, keepdims=True)
    h = x1 * lax.rsqrt(ms + NORM_EPS) * n2_ref[...] * (1.0 + sc_ref[0]) + sh_ref[0]
    hf_o[0] = h
    hb_o[0] = h.astype(BF16)


def _merge(a, o0, o1, row_off, bonus, g, p, x, lw, g1, sc2, sh2):
    b, n, d = x.shape
    tm = _pick(n, 256)
    w = C_WIDTH
    assert row_off % tm == 0
    tile = lambda width: pl.BlockSpec((1, tm, width), lambda bi, i: (bi, i, 0))
    scan_tile = pl.BlockSpec((1, tm, w), lambda bi, i: (bi, i + row_off // tm, 0))
    const = lambda shape: pl.BlockSpec(shape, lambda bi, i: (0,) * len(shape))
    per_b = pl.BlockSpec((1, 1, d), lambda bi, i: (bi, 0, 0))
    gate = lambda j: pl.BlockSpec((1, tm, d), lambda bi, i, j=j: (bi, i, COL_GATES // d + j))
    in_specs = ([tile(w), scan_tile, scan_tile, tile(w), tile(w)]
                + _halo_specs(tm, 3 * w, COL_CONV // (3 * w), n) + [gate(0), gate(1), gate(2)]
                + [tile(d), const((1, w)), const((1, w)), const((3, w)), const((N_BRANCH, w, d)), const((d, d)),
                   const((w, w)), per_b, const((1, d)), per_b, per_b])
    return pl.pallas_call(
        _merge_kernel,
        out_shape=(jax.ShapeDtypeStruct((b, n, d), F32), jax.ShapeDtypeStruct((b, n, d), F32),
                   jax.ShapeDtypeStruct((b, n, d), BF16)),
        grid=(b, n // tm),
        in_specs=in_specs,
        out_specs=[tile(d)] * 3,
        compiler_params=_cparams(("parallel", "parallel")),
        name="merge_mixers",
    )(a, o0, o1, bonus, g, p, p, p, p, p, p, x, lw['rwkv_ln_g'], lw['rwkv_ln_b'], lw['conv_w'], lw['w_branch_bf16'],
      lw['w_out_bf16'], lw['ones64'], g1, lw['norm2_g'], sc2, sh2)


def _router_kernel(h_ref, rw_ref, o_ref):
    logits = lax.dot_general(rw_ref[...], h_ref[0], (((1,), (1,)), ((), ())), precision=HIGHEST,
                             preferred_element_type=F32)
    z = jnp.exp(logits - jnp.max(logits, axis=0, keepdims=True))
    o_ref[0] = z / jnp.sum(z, axis=0, keepdims=True)


def _router(h, rw_t):
    b, n, d = h.shape
    tm = _pick(n, 512)
    return pl.pallas_call(
        _router_kernel,
        out_shape=jax.ShapeDtypeStruct((b, N_EXPERTS, n), F32),
        grid=(b, n // tm),
        in_specs=[pl.BlockSpec((1, tm, d), lambda bi, i: (bi, i, 0)), pl.BlockSpec((N_EXPERTS, d), lambda bi, i: (0, 0))],
        out_specs=pl.BlockSpec((1, N_EXPERTS, tm), lambda bi, i: (bi, 0, i)),
        compiler_params=_cparams(("parallel", "parallel")),
        name="router_affinity",
    )(h, rw_t)


def _select_kernel(aff_ref, tri_ref, pos_o, cnt_o, *, cap, lb):
    aff = aff_ref[0]
    n = aff.shape[1]
    bits = pltpu.bitcast(aff, jnp.int32)
    count = lambda mask: jnp.sum(mask.astype(F32), axis=1, keepdims=True)

    def refine(i, lo):
        cand = lo | jnp.left_shift(jnp.int32(1), 30 - i)
        return jnp.where(count(bits >= cand) >= cap, cand, lo)

    thr = lax.fori_loop(0, 31, refine, jnp.zeros((N_EXPERTS, 1), jnp.int32))
    gt = bits > thr
    eq = bits == thr
    need = cap - count(gt)
    tri = tri_ref[...]

    def prefix(mask_of_block, write):
        off = jnp.zeros((N_EXPERTS, 1), F32)
        for j in range(n // lb):
            m = mask_of_block(j)
            excl = jnp.dot(m.astype(BF16), tri, preferred_element_type=F32) + off
            write(j, m, excl)
            off = off + jnp.sum(m.astype(F32), axis=1, keepdims=True)

    blk = lambda x, j: x[:, j * lb:(j + 1) * lb]
    sel_blocks = [None] * (n // lb)

    def write_sel(j, m, excl):
        sel_blocks[j] = blk(gt, j) | (m & (excl < need))

    prefix(lambda j: blk(eq, j), write_sel)

    def write_pos(j, m, excl):
        c = excl.astype(jnp.int32)
        cnt_o[0, :, j * lb:(j + 1) * lb] = c
        pos_o[0, :, j * lb:(j + 1) * lb] = jnp.where(m, c, -1)

    prefix(lambda j: sel_blocks[j], write_pos)


def _select(aff, cap):
    b, e, n = aff.shape
    lb = min(n, LANES)
    tri = jnp.asarray(np.triu(np.ones((lb, lb), np.float32), 1), BF16)
    blk = pl.BlockSpec((1, e, n), lambda bi: (bi, 0, 0))
    out = jax.ShapeDtypeStruct((b, e, n), jnp.int32)
    return pl.pallas_call(
        functools.partial(_select_kernel, cap=cap, lb=lb),
        out_shape=(out, out),
        grid=(b,),
        in_specs=[blk, pl.BlockSpec((lb, lb), lambda bi: (0, 0))],
        out_specs=[blk, blk],
        compiler_params=_cparams(("parallel",)),
        name="expert_select",
    )(aff, tri)


GATHER_CHUNK = 64


def _expert_kernel(starts_ref, pos_ref, aff_ref, h_ref, w1_ref, w3_ref, w2_ref, y_o, xin_s, gate_s,
                   *, cap, tile, ntile, tpg, nchunk):
    e, bi, g = pl.program_id(0), pl.program_id(1), pl.program_id(2)

    @pl.when(g == 0)
    def _():
        xin_s[...] = jnp.zeros(xin_s.shape, F32)
        gate_s[...] = jnp.zeros(gate_s.shape, F32)

    for tt in range(tpg):
        base = (e * pl.num_programs(1) + bi) * (ntile + 1) + g * tpg + tt
        start, end = starts_ref[base], starts_ref[base + 1]
        w0 = (start // SUBLANES) * SUBLANES
        cols = slice(tt * tile, (tt + 1) * tile)
        for c in range(nchunk):
            r0 = pl.multiple_of(w0 + c * GATHER_CHUNK, SUBLANES)

            @pl.when((r0 < end) & (r0 + GATHER_CHUNK > start))
            def _():
                slot = r0 + lax.broadcasted_iota(jnp.int32, (GATHER_CHUNK, tile), 0)
                onehot = pos_ref[0, 0][:, cols] == slot
                rows = pl.ds(r0, GATHER_CHUNK)
                xin_s[rows, :] += jnp.dot(onehot.astype(BF16), h_ref[0, cols, :], preferred_element_type=F32)
                gate = jnp.sum(jnp.where(onehot, aff_ref[0, 0][:, cols], 0.0), axis=1, keepdims=True)
                gate_s[rows, :] += jnp.broadcast_to(gate, (GATHER_CHUNK, gate_s.shape[1]))

    @pl.when(g == pl.num_programs(2) - 1)
    def _():
        rc = min(cap, SLOT_BLOCK)
        capp = y_o.shape[2]
        if capp > cap:
            y_o[0, 0, cap:capp, :] = jnp.zeros((capp - cap, y_o.shape[3]), BF16)
        for c in range(cap // rc):
            x = xin_s[c * rc:(c + 1) * rc, :].astype(BF16)
            hid = _silu(jnp.dot(x, w1_ref[0], preferred_element_type=F32)) * jnp.dot(x, w3_ref[0], preferred_element_type=F32)
            y = jnp.dot(hid.astype(BF16), w2_ref[0], preferred_element_type=F32)
            y_o[0, 0, c * rc:(c + 1) * rc, :] = (y * gate_s[c * rc:(c + 1) * rc, 0:1]).astype(BF16)


def _moe_tile(n):
    return _pick(n, 256)


def _experts(starts, pos4, aff4, h_bf16, w1, w3, w2, cap):
    b, n, d = h_bf16.shape
    tile = _moe_tile(n)
    ntile = n // tile
    nchunk = tile // GATHER_CHUNK + 1
    rows = cap + nchunk * GATHER_CHUNK
    capp = -(-cap // SLOT_BLOCK) * SLOT_BLOCK
    f = w1.shape[2]
    tg = _pick(n, 2048)
    tpg = tg // tile
    grid_spec = pltpu.PrefetchScalarGridSpec(
        num_scalar_prefetch=1,
        grid=(N_EXPERTS, b, n // tg),
        in_specs=[pl.BlockSpec((1, 1, 1, tg), lambda e, bi, g, s: (bi, e, 0, g)),
                  pl.BlockSpec((1, 1, 1, tg), lambda e, bi, g, s: (bi, e, 0, g)),
                  pl.BlockSpec((1, tg, d), lambda e, bi, g, s: (bi, g, 0)),
                  pl.BlockSpec((1, d, f), lambda e, bi, g, s: (e, 0, 0)),
                  pl.BlockSpec((1, d, f), lambda e, bi, g, s: (e, 0, 0)),
                  pl.BlockSpec((1, f, d), lambda e, bi, g, s: (e, 0, 0))],
        out_specs=pl.BlockSpec((1, 1, capp, d), lambda e, bi, g, s: (e, bi, 0, 0)),
        scratch_shapes=[pltpu.VMEM((rows, d), F32), pltpu.VMEM((rows, LANES), F32)])
    return pl.pallas_call(
        functools.partial(_expert_kernel, cap=cap, tile=tile, ntile=ntile, tpg=tpg, nchunk=nchunk),
        out_shape=jax.ShapeDtypeStruct((N_EXPERTS, b, capp, d), BF16),
        grid_spec=grid_spec,
        compiler_params=_cparams(("arbitrary", "arbitrary", "arbitrary")),
        name="expert_gather_ffn",
    )(starts, pos4, aff4, h_bf16, w1, w3, w2)


def _combine_kernel(starts_ref, pos_ref, *refs, tile, tpg, ntile, sb, nblk):
    y_refs, (x_ref, g2_ref, o_ref) = refs[:nblk], refs[nblk:]
    bi, gi, e = pl.program_id(0), pl.program_id(1), pl.program_id(2)

    @pl.when(e == 0)
    def _():
        o_ref[...] = jnp.zeros(o_ref.shape, F32)

    base = (e * pl.num_programs(0) + bi) * (ntile + 1) + gi * tpg
    k0 = starts_ref[base] // sb
    for tt in range(tpg):
        start, end = starts_ref[base + tt], starts_ref[base + tt + 1]
        cols = slice(tt * tile, (tt + 1) * tile)
        for j, y_ref in enumerate(y_refs):
            s0 = (k0 + j) * sb

            @pl.when((s0 < end) & (s0 + sb > start) & (end > start))
            def _():
                slot = s0 + lax.broadcasted_iota(jnp.int32, (sb, tile), 0)
                onehot_t = (pos_ref[0, 0][:, cols] == slot).astype(F32).T.astype(BF16)
                o_ref[0, cols, :] += jnp.dot(onehot_t, y_ref[0, 0], preferred_element_type=F32)

    @pl.when(e == pl.num_programs(2) - 1)
    def _():
        o_ref[0] = x_ref[0] + g2_ref[0] * o_ref[0]


def _combine(starts, pos4, y, x, g2):
    b, n, d = x.shape
    tile = _moe_tile(n)
    ntile = n // tile
    gt = _pick(n, 1024)
    tpg = gt // tile
    sb = SLOT_BLOCK
    nsb = y.shape[2] // sb
    nblk = min(gt // sb + 1, nsb)

    def yblk(j):
        def imap(bi, gi, e, s):
            k0 = s[(e * b + bi) * (ntile + 1) + gi * tpg] // sb
            return (e, bi, jnp.minimum(k0 + j, nsb - 1), 0)
        return pl.BlockSpec((1, 1, sb, d), imap)

    grid_spec = pltpu.PrefetchScalarGridSpec(
        num_scalar_prefetch=1,
        grid=(b, n // gt, N_EXPERTS),
        in_specs=[pl.BlockSpec((1, 1, 1, gt), lambda bi, gi, e, s: (bi, e, 0, gi))]
                 + [yblk(j) for j in range(nblk)]
                 + [pl.BlockSpec((1, gt, d), lambda bi, gi, e, s: (bi, gi, 0)),
                    pl.BlockSpec((1, 1, d), lambda bi, gi, e, s: (bi, 0, 0))],
        out_specs=pl.BlockSpec((1, gt, d), lambda bi, gi, e, s: (bi, gi, 0)),
        scratch_shapes=[])
    return pl.pallas_call(
        functools.partial(_combine_kernel, tile=tile, tpg=tpg, ntile=ntile, sb=sb, nblk=nblk),
        out_shape=jax.ShapeDtypeStruct((b, n, d), F32),
        grid_spec=grid_spec,
        compiler_params=_cparams(("parallel", "parallel", "arbitrary")),
        name="expert_combine",
    )(starts, pos4, *([y] * nblk), x, g2)


def _moe(h_f32, h_bf16, x, g2, lw):
    b, n, _ = x.shape
    cap = CAPACITY_FACTOR * n // N_EXPERTS
    aff = _router(h_f32, lw['router_w_t'])
    pos, cnt = _select(aff, cap)
    tile = _moe_tile(n)
    starts = jnp.concatenate([cnt[:, :, ::tile], jnp.full((b, N_EXPERTS, 1), cap, jnp.int32)], axis=2)
    starts = jnp.transpose(starts, (1, 0, 2)).reshape(-1)
    pos4 = pos.reshape(b, N_EXPERTS, 1, n)
    aff4 = aff.reshape(b, N_EXPERTS, 1, n)
    y = _experts(starts, pos4, aff4, h_bf16, lw['w1_bf16'], lw['w3_bf16'], lw['w2_bf16'], cap)
    return _combine(starts, pos4, y, x, g2)


def _pad_w_in(w_in):
    d = w_in.shape[0]
    sizes = (A_WIDTH,) * 3 + (B_WIDTH,) * 3 + (B_DECAY_RANK, B_ICL_RANK, B_GATE_RANK) + (C_WIDTH,) * 3 + (N_BRANCH * D_MODEL,)
    cuts = np.cumsum((0,) + sizes)
    piece = lambda lo, hi: w_in[:, int(cuts[lo]):int(cuts[hi])]
    zeros = jnp.zeros((d, COL_GATES - COL_LOWRANK - 256), w_in.dtype)
    out = jnp.concatenate([piece(0, 3), piece(3, 6), piece(9, 12), piece(6, 9), zeros, piece(12, 13)], axis=1)
    assert out.shape[1] == N_IN_PAD
    return out.astype(BF16)


def _rows_at(w, lo, total=256):
    pad = [(0, 0)] * w.ndim
    pad[-2] = (lo, total - lo - w.shape[-2])
    return jnp.pad(w, pad)


def _rope_tables(rows):
    half = A_HD // 2
    inv = jnp.power(ROPE_THETA, -jnp.arange(0, half, 2, dtype=F32) / half)
    r = jnp.repeat(jnp.arange(rows, dtype=F32), GRID_W)
    col = jnp.tile(jnp.arange(GRID_W, dtype=F32), rows)
    ang = jnp.concatenate([r[:, None] * inv, col[:, None] * inv], axis=-1)
    cos, sin = jnp.cos(ang), jnp.sin(ang)
    reps = A_WIDTH // A_HD
    cos_t = jnp.tile(jnp.concatenate([cos, cos], axis=-1), (1, reps))
    sin_t = jnp.tile(jnp.concatenate([-sin, sin], axis=-1), (1, reps))
    return cos_t, sin_t


def _layer_weights(i, norm2_g, w_in, q_norm_g, k_norm_g, diff_lambda, diff_subln_g, rwkv_conv_w, rwkv_w0, rwkv_w_up,
                   rwkv_a0, rwkv_a_up, rwkv_g_up, rwkv_k_k, rwkv_k_a, rwkv_r_k, rwkv_ln_g, rwkv_ln_b, conv_w,
                   w_branch, w_out, router_w, exp_w1, exp_w3, exp_w2):
    row = lambda v: v.reshape(1, -1)
    return {
        'norm2_g': row(norm2_g[i]),
        'w_in_pad': _pad_w_in(w_in[i]),
        'q_g': row(jnp.tile(q_norm_g[i], A_WIDTH // A_HD)), 'k_g': row(jnp.tile(k_norm_g[i], A_WIDTH // A_HD)),
        'diff_lambda': diff_lambda[i], 'subln_g': row(diff_subln_g[i]),
        'rwkv_conv_w': rwkv_conv_w[i], 'rwkv_w0': rwkv_w0[i], 'rwkv_a0': rwkv_a0[i],
        'w_up_pad': _rows_at(rwkv_w_up[i], 0), 'a_up_pad': _rows_at(rwkv_a_up[i], B_DECAY_RANK),
        'g_up_pad': _rows_at(rwkv_g_up[i], B_DECAY_RANK + B_ICL_RANK),
        'rwkv_k_k': row(rwkv_k_k[i]), 'rwkv_k_a': row(rwkv_k_a[i]), 'rwkv_r_k': row(rwkv_r_k[i]),
        'rwkv_ln_g': row(rwkv_ln_g[i]), 'rwkv_ln_b': row(rwkv_ln_b[i]),
        'conv_w': conv_w[i], 'w_branch_bf16': w_branch[i].astype(BF16), 'w_out_bf16': w_out[i].astype(BF16),
        'router_w_t': router_w[i].T,
        'w1_bf16': exp_w1[i].astype(BF16), 'w3_bf16': exp_w3[i].astype(BF16), 'w2_bf16': exp_w2[i].astype(BF16),
        'ones64': _group_ones(B_WIDTH, B_HD).astype(BF16),
    }


def _mixer_inputs(x, norm1_g, sc1, sh1, lw, cos_t, sin_t):
    p = _norm_mod_proj(x, norm1_g, sc1, sh1, lw['w_in_pad'])
    q, k = _qkv_prep(p, lw['q_g'], lw['k_g'], lw['ones64'], cos_t, sin_t)
    names = ('r', 'v', 'kk', 'g', 'bonus', 'dec0', 'b0', 'kd0', 'dec1', 'b1', 'kd1')
    prep = dict(zip(names, _rwkv_prep(p, lw)))
    return p, (q, k), prep


def kernel(x, c, ctx, c_ctx, norm1_g, norm2_g, ada_w, ada_b, w_in, q_norm_g, k_norm_g, diff_lambda, diff_subln_g,
           rwkv_conv_w, rwkv_w0, rwkv_w_up, rwkv_a0, rwkv_a_up, rwkv_g_up, rwkv_k_k, rwkv_k_a, rwkv_r_k, rwkv_ln_g,
           rwkv_ln_b, conv_w, w_branch, w_out, router_w, exp_w1, exp_w3, exp_w2):
    bsz, n_lat, d = x.shape
    depth = ada_w.shape[0]
    cos_t, sin_t = _rope_tables(n_lat // GRID_W)

    cond = jnp.concatenate([c, jnp.broadcast_to(c_ctx[None, :], (SUBLANES - bsz % SUBLANES, d))], axis=0)
    ada = _ada_params(cond, ada_w, ada_b)

    xc, xl = ctx, x
    for i in range(depth):
        lam_init = 0.8 - 0.6 * math.exp(-0.3 * i)
        need_ctx = i != depth - 1
        lw = _layer_weights(i, norm2_g, w_in, q_norm_g, k_norm_g, diff_lambda, diff_subln_g, rwkv_conv_w, rwkv_w0,
                            rwkv_w_up, rwkv_a0, rwkv_a_up, rwkv_g_up, rwkv_k_k, rwkv_k_a, rwkv_r_k, rwkv_ln_g,
                            rwkv_ln_b, conv_w, w_branch, w_out, router_w, exp_w1, exp_w3, exp_w2)
        mods = ada[i].reshape(-1, 6, d)
        lat = [mods[:bsz, j][:, None, :] for j in range(6)]
        cxt = [jnp.broadcast_to(mods[bsz, j][None, None, :], (bsz, 1, d)) for j in range(6)]

        pc, (qc, kc), prep_c = _mixer_inputs(xc, norm1_g[i], cxt[1], cxt[0], lw, None, None)
        pl_, (ql, kl), prep_l = _mixer_inputs(xl, norm1_g[i], lat[1], lat[0], lw, cos_t, sin_t)

        al = _flash_attention(lw['diff_lambda'], lw['subln_g'], ql, kc, pc, kl, pl_, lam_init)
        o0, o1 = _rwkv_bidirectional(prep_c, prep_l)
        n_ctx = xc.shape[1]

        xl, hl, hlb = _merge(al, o0, o1, n_ctx, prep_l['bonus'], prep_l['g'], pl_, xl, lw, lat[2], lat[4], lat[3])
        xl = _moe(hl, hlb, xl, lat[5], lw)
        if need_ctx:
            ac = _ctx_attention(lw['diff_lambda'], lw['subln_g'], qc, kc, pc, lam_init)
            xc, hc, hcb = _merge(ac, o0, o1, 0, prep_c['bonus'], prep_c['g'], pc, xc, lw, cxt[2], cxt[4], cxt[3])
            xc = _moe(hc, hcb, xc, cxt[5], lw)
    return xl
```

```python
import functools
import math

import numpy as np
import jax
import jax.numpy as jnp
from jax import lax
from jax.experimental import pallas as pl
from jax.experimental.pallas import tpu as pltpu

F32 = jnp.float32
BF16 = jnp.bfloat16
HIGHEST = lax.Precision.HIGHEST

D_MODEL = 1024
GRID_W = 64
A_HEADS = 4
A_HD = 64
A_WIDTH = A_HEADS * 2 * A_HD
B_HEADS = 8
B_HD = 64
B_WIDTH = B_HEADS * B_HD
B_DECAY_RANK = 64
B_ICL_RANK = 64
B_GATE_RANK = 128
C_WIDTH = 512
N_BRANCH = 3
N_EXPERTS = 16
EXPERT_FF = 1024
CAPACITY_FACTOR = 2
ROPE_THETA = 10000.0
NORM_EPS = 1e-6
GN_EPS = 64e-5
LOG2_E = math.log2(math.e)

COL_ATTN = 0
COL_RWKV = 1536
COL_CONV = 3072
COL_LOWRANK = 4608
COL_GATES = 5120
N_IN_PAD = 8192

VMEM_LIMIT = 56 * 1024 * 1024
LANES = 128
SUBLANES = 8
SLOT_BLOCK = 256


def _cparams(sem):
    return pltpu.CompilerParams(dimension_semantics=sem, vmem_limit_bytes=VMEM_LIMIT)


def _pick(n, pref):
    t = min(n, pref)
    while n % t:
        t -= SUBLANES
    return t


def _group_ones(width, group):
    idx = np.arange(width) // group
    return jnp.asarray((idx[:, None] == idx[None, :]).astype(np.float32))


def _silu(x):
    return x * jax.nn.sigmoid(x)


def _split2(x):
    hi = x.astype(BF16)
    return hi, (x - hi.astype(F32)).astype(BF16)


def _group_sum(x, ones_bf16):
    hi, lo = _split2(x)
    return (jnp.dot(hi, ones_bf16, preferred_element_type=F32) + jnp.dot(lo, ones_bf16, preferred_element_type=F32))


def _permute_rows(perm_bf16, x):
    hi, lo = _split2(x)
    rest = x - hi.astype(F32) - lo.astype(F32)
    dot = lambda v: jnp.dot(perm_bf16, v, preferred_element_type=F32)
    return dot(hi) + dot(lo) + dot(rest.astype(BF16))


def _dot3(a, b):
    a_hi, a_lo = _split2(a)
    b_hi, b_lo = _split2(b)
    dot = lambda u, v: jnp.dot(u, v, preferred_element_type=F32)
    return dot(a_hi, b_hi) + (dot(a_hi, b_lo) + dot(a_lo, b_hi))


def _ada_kernel(cond_ref, w_ref, b_ref, o_ref):
    c = cond_ref[...]
    o_ref[0] = jnp.dot(_silu(c), w_ref[0], precision=HIGHEST, preferred_element_type=F32) + b_ref[0]


def _ada_params(cond, ada_w, ada_b):
    depth, d, n6 = ada_w.shape
    tn = _pick(n6, 1536)
    return pl.pallas_call(
        _ada_kernel,
        out_shape=jax.ShapeDtypeStruct((depth, cond.shape[0], n6), F32),
        grid=(depth, n6 // tn),
        in_specs=[pl.BlockSpec(cond.shape, lambda i, j: (0, 0)),
                  pl.BlockSpec((1, d, tn), lambda i, j: (i, 0, j)),
                  pl.BlockSpec((1, 1, tn), lambda i, j: (i, 0, j))],
        out_specs=pl.BlockSpec((1, cond.shape[0], tn), lambda i, j: (i, 0, j)),
        compiler_params=_cparams(("parallel", "parallel")),
        name="ada_params",
    )(cond, ada_w, ada_b.reshape(depth, 1, n6))


def _proj_kernel(x_ref, g_ref, sc_ref, sh_ref, w_ref, o_ref):
    x = x_ref[0]
    ms = jnp.mean(x * x, axis=-1, keepdims=True)
    y = x * lax.rsqrt(ms + NORM_EPS) * g_ref[...]
    h = y * (1.0 + sc_ref[0]) + sh_ref[0]
    o_ref[0] = jnp.dot(h.astype(BF16), w_ref[...], preferred_element_type=F32).astype(BF16)


def _norm_mod_proj(x, g, sc, sh, w_bf16):
    b, n, d = x.shape
    npad = w_bf16.shape[1]
    tm = _pick(n, 512)
    tn = 2048
    return pl.pallas_call(
        _proj_kernel,
        out_shape=jax.ShapeDtypeStruct((b, n, npad), BF16),
        grid=(npad // tn, b, n // tm),
        in_specs=[pl.BlockSpec((1, tm, d), lambda j, bi, i: (bi, i, 0)),
                  pl.BlockSpec((1, d), lambda j, bi, i: (0, 0)),
                  pl.BlockSpec((1, 1, d), lambda j, bi, i: (bi, 0, 0)),
                  pl.BlockSpec((1, 1, d), lambda j, bi, i: (bi, 0, 0)),
                  pl.BlockSpec((d, tn), lambda j, bi, i: (0, j))],
        out_specs=pl.BlockSpec((1, tm, tn), lambda j, bi, i: (bi, i, j)),
        compiler_params=_cparams(("parallel", "parallel", "parallel")),
        name="norm_mod_proj",
    )(x, g.reshape(1, d), sc, sh, w_bf16)


def _swap_halves(x, half):
    width = x.shape[-1]
    lane = lax.broadcasted_iota(jnp.int32, (1, width), 1) % (2 * half)
    fwd = pltpu.roll(x, width - half, axis=1)
    bwd = pltpu.roll(x, half, axis=1)
    return jnp.where(lane < half, fwd, bwd)


def _qkv_prep_kernel(*refs, use_rope):
    if use_rope:
        q_ref, k_ref, qg_ref, kg_ref, ones_ref, cos_ref, sin_ref, qo_ref, ko_ref = refs
    else:
        q_ref, k_ref, qg_ref, kg_ref, ones_ref, qo_ref, ko_ref = refs

    def head_norm(x, gain):
        ms = _group_sum(x * x, ones_ref[...]) * (1.0 / A_HD)
        return x * lax.rsqrt(ms + NORM_EPS) * gain

    def rope(x):
        if not use_rope:
            return x
        return x * cos_ref[...] + _swap_halves(x, A_HD // 2) * sin_ref[...]

    q = rope(head_norm(q_ref[0].astype(F32), qg_ref[...]))
    k = rope(head_norm(k_ref[0].astype(F32), kg_ref[...]))
    qo_ref[0] = (q * (A_HD ** -0.5 * LOG2_E)).astype(BF16)
    ko_ref[0] = k.astype(BF16)


def _qkv_prep(p, qg, kg, ones, cos_t, sin_t):
    b, n, _ = p.shape
    tm = _pick(n, 512)
    use_rope = cos_t is not None
    w = A_WIDTH
    col = lambda c: pl.BlockSpec((1, tm, w), lambda bi, i, c=c: (bi, i, COL_ATTN // w + c))
    const = lambda shape: pl.BlockSpec(shape, lambda bi, i: (0,) * len(shape))
    in_specs = [col(0), col(1), const((1, w)), const((1, w)), const((w, w))]
    args = [p, p, qg, kg, ones]
    if use_rope:
        in_specs += [pl.BlockSpec((tm, w), lambda bi, i: (i, 0))] * 2
        args += [cos_t, sin_t]
    out = jax.ShapeDtypeStruct((b, n, w), BF16)
    return pl.pallas_call(
        functools.partial(_qkv_prep_kernel, use_rope=use_rope),
        out_shape=(out, out),
        grid=(b, n // tm),
        in_specs=in_specs,
        out_specs=[pl.BlockSpec((1, tm, w), lambda bi, i: (bi, i, 0))] * 2,
        compiler_params=_cparams(("parallel", "parallel")),
        name="qkv_prep_rope" if use_rope else "qkv_prep",
    )(*args)


def _map_stacked(q):
    lane = lax.broadcasted_iota(jnp.int32, (1, 2 * A_HD), 1)
    zero = jnp.zeros_like(q)
    return jnp.concatenate([jnp.where(lane < A_HD, q, zero), jnp.where(lane >= A_HD, q, zero)], axis=0)


def _lane_tiled(x, width):
    return jnp.concatenate([x] * (width // LANES), axis=1)


def _lane_block_sum(p):
    out = p[:, 0:LANES]
    for j in range(1, p.shape[1] // LANES):
        out = out + p[:, j * LANES:(j + 1) * LANES]
    return out


def _lambda_full(dl_ref, lam_init):
    dl = dl_ref[...]
    s1 = jnp.sum(dl[0:1] * dl[1:2], axis=-1, keepdims=True)
    s2 = jnp.sum(dl[2:3] * dl[3:4], axis=-1, keepdims=True)
    return jnp.exp(s1) - jnp.exp(s2) + lam_init


def _subln(o, sg_ref, lam_init):
    ms = jnp.mean(o * o, axis=-1, keepdims=True)
    return o * lax.rsqrt(ms + NORM_EPS) * sg_ref[...] * (1.0 - lam_init)


def _scores(qm, k):
    return lax.dot_general(qm, k, (((1,), (1,)), ((), ())), preferred_element_type=F32)


def _flash_kernel(dl_ref, sg_ref, q_ref, kc_ref, vc_ref, k_ref, v_ref, o_ref, m_s, l_s, acc_s, *, lam_init, nkv):
    kj = pl.program_id(3)
    tq = q_ref.shape[1]
    qs = _map_stacked(q_ref[0])

    def update(k, v):
        s = _scores(qs, k)
        m_prev = m_s[...]
        m_new = jnp.maximum(m_prev, jnp.max(s, axis=-1, keepdims=True))
        alpha = jnp.exp2(m_prev - m_new)
        p = jnp.exp2(s - _lane_tiled(m_new, s.shape[1]))
        l_s[...] = alpha * l_s[...] + _lane_block_sum(p)
        acc_s[...] = alpha * acc_s[...] + jnp.dot(p.astype(BF16), v, preferred_element_type=F32)
        m_s[...] = m_new

    @pl.when(kj == 0)
    def _():
        m_s[...] = jnp.full(m_s.shape, -jnp.inf, F32)
        l_s[...] = jnp.zeros(l_s.shape, F32)
        acc_s[...] = jnp.zeros(acc_s.shape, F32)
        update(kc_ref[0], vc_ref[0])

    update(k_ref[0], v_ref[0])

    @pl.when(kj == nkv - 1)
    def _():
        lam = _lambda_full(dl_ref, lam_init)
        o = acc_s[...] / jnp.sum(l_s[...], axis=-1, keepdims=True)
        o_ref[0] = _subln(o[0:tq] - lam * o[tq:2 * tq], sg_ref, lam_init).astype(o_ref.dtype)


V_COL_BLOCK = (COL_ATTN + 2 * A_WIDTH) // (2 * A_HD)


def _flash_attention(dl, sg, q, kc, vc, kl, vl, lam_init):
    b, s, _ = q.shape
    nc = kc.shape[1]
    hw = 2 * A_HD
    tq = _pick(s, 1024)
    tk = _pick(s, 1024)
    nkv = s // tk
    return pl.pallas_call(
        functools.partial(_flash_kernel, lam_init=lam_init, nkv=nkv),
        out_shape=jax.ShapeDtypeStruct((b, s, A_WIDTH), BF16),
        grid=(b, A_HEADS, s // tq, nkv),
        in_specs=[pl.BlockSpec((4, A_HD), lambda bi, h, i, j: (0, 0)),
                  pl.BlockSpec((1, hw), lambda bi, h, i, j: (0, 0)),
                  pl.BlockSpec((1, tq, hw), lambda bi, h, i, j: (bi, i, h)),
                  pl.BlockSpec((1, nc, hw), lambda bi, h, i, j: (bi, 0, h)),
                  pl.BlockSpec((1, nc, hw), lambda bi, h, i, j: (bi, 0, V_COL_BLOCK + h)),
                  pl.BlockSpec((1, tk, hw), lambda bi, h, i, j: (bi, j, h)),
                  pl.BlockSpec((1, tk, hw), lambda bi, h, i, j: (bi, j, V_COL_BLOCK + h))],
        out_specs=pl.BlockSpec((1, tq, hw), lambda bi, h, i, j: (bi, i, h)),
        scratch_shapes=[pltpu.VMEM((2 * tq, LANES), F32), pltpu.VMEM((2 * tq, LANES), F32), pltpu.VMEM((2 * tq, hw), F32)],
        compiler_params=_cparams(("parallel", "parallel", "parallel", "arbitrary")),
        name="diff_flash_attention",
    )(dl, sg, q, kc, vc, kl, vl)


def _ctx_attn_kernel(dl_ref, sg_ref, q_ref, k_ref, v_ref, o_ref, *, lam_init):
    n = q_ref.shape[1]
    s = _scores(_map_stacked(q_ref[0]), k_ref[0])
    p = jnp.exp2(s - jnp.max(s, axis=-1, keepdims=True))
    o = jnp.dot(p.astype(BF16), v_ref[0], preferred_element_type=F32) / jnp.sum(p, axis=-1, keepdims=True)
    lam = _lambda_full(dl_ref, lam_init)
    o_ref[0] = _subln(o[0:n] - lam * o[n:2 * n], sg_ref, lam_init).astype(o_ref.dtype)


def _ctx_attention(dl, sg, q, k, v, lam_init):
    b, n, _ = q.shape
    hw = 2 * A_HD
    blk = pl.BlockSpec((1, n, hw), lambda bi, h: (bi, 0, h))
    vblk = pl.BlockSpec((1, n, hw), lambda bi, h: (bi, 0, V_COL_BLOCK + h))
    return pl.pallas_call(
        functools.partial(_ctx_attn_kernel, lam_init=lam_init),
        out_shape=jax.ShapeDtypeStruct((b, n, A_WIDTH), BF16),
        grid=(b, A_HEADS),
        in_specs=[pl.BlockSpec((4, A_HD), lambda bi, h: (0, 0)), pl.BlockSpec((1, hw), lambda bi, h: (0, 0)),
                  blk, blk, vblk],
        out_specs=blk,
        compiler_params=_cparams(("parallel", "parallel")),
        name="ctx_attention",
    )(dl, sg, q, k, v)


HALO_ROWS = 16


def _halo_specs(tm, width, colblk, n):
    rh = tm // HALO_ROWS
    last = n // HALO_ROWS - 1
    main = pl.BlockSpec((1, tm, width), lambda bi, i: (bi, i, colblk))
    prev = pl.BlockSpec((1, HALO_ROWS, width), lambda bi, i: (bi, jnp.maximum(i * rh - 1, 0), colblk))
    nxt = pl.BlockSpec((1, HALO_ROWS, width), lambda bi, i: (bi, jnp.minimum((i + 1) * rh, last), colblk))
    return [main, prev, nxt]


def _dwconv3(x, prev_row, next_row, w_ref):
    tm = x.shape[0]
    row = lax.broadcasted_iota(jnp.int32, (tm, 1), 0)
    xp = jnp.where(row == 0, prev_row, pltpu.roll(x, 1, axis=0))
    xn = jnp.where(row == tm - 1, next_row, pltpu.roll(x, tm - 1, axis=0))
    w = w_ref[...]
    return w[0:1] * xp + w[1:2] * x + w[2:3] * xn


def _edge_rows(prev_ref, next_ref):
    i = pl.program_id(1)
    n = pl.num_programs(1)
    pr = prev_ref[0].astype(F32)[HALO_ROWS - 1:HALO_ROWS]
    nr = next_ref[0].astype(F32)[0:1]
    pr = jnp.where(i == 0, jnp.zeros_like(pr), pr)
    nr = jnp.where(i == n - 1, jnp.zeros_like(nr), nr)
    return pr, nr


def _softplus(u):
    return jnp.maximum(u, 0.0) + jnp.log(1.0 + jnp.exp(-jnp.abs(u)))


def _rwkv_prep_kernel(rkv_ref, prev_ref, next_ref, lr_ref, cw_ref, w0_ref, wup_ref, a0_ref, aup_ref, gup_ref,
                      kk_ref_w, ka_ref, rk_ref, ones_ref,
                      r_o, v_o, kk_o, g_o, bonus_o, dec0_o, b0_o, kd0_o, dec1_o, b1_o, kd1_o):
    pr, nr = _edge_rows(prev_ref, next_ref)
    rkv = _dwconv3(rkv_ref[0].astype(F32), pr, nr, cw_ref)
    w = B_WIDTH
    r, k, v = rkv[:, 0:w], rkv[:, w:2 * w], rkv[:, 2 * w:3 * w]
    gsum = lambda t: _group_sum(t, ones_ref[...])
    kkr = k * kk_ref_w[...]
    nrm = jnp.maximum(jnp.sqrt(gsum(kkr * kkr)), 1e-12)
    kk = kkr / nrm
    lr = lr_ref[0].astype(F32)
    th = jnp.tanh(lr)
    sg = jax.nn.sigmoid(lr)
    r_o[0] = r
    v_o[0] = v
    kk_o[0] = kk
    g_o[0] = _dot3(sg, gup_ref[...])
    kd_sum = jnp.zeros_like(k)
    outs = ((dec0_o, b0_o, kd0_o), (dec1_o, b1_o, kd1_o))
    for d in range(2):
        z = w0_ref[d:d + 1] + _dot3(th, wup_ref[d])
        wlog = -_softplus(-z) - 0.5
        dec = jnp.exp(-jnp.exp(wlog))
        a = jax.nn.sigmoid(a0_ref[d:d + 1] + _dot3(lr, aup_ref[d]))
        kd = k * (1.0 + (a - 1.0) * ka_ref[...])
        outs[d][0][0] = dec
        outs[d][1][0] = kk * a
        outs[d][2][0] = kd
        kd_sum = kd_sum + kd
    bonus_o[0] = gsum(r * kd_sum * rk_ref[...]) * v


def _rwkv_prep(p, lw):
    b, n, _ = p.shape
    tm = _pick(n, 256)
    w = B_WIDTH
    w3 = 3 * w
    const = lambda shape: pl.BlockSpec(shape, lambda bi, i: (0,) * len(shape))
    in_specs = _halo_specs(tm, w3, COL_RWKV // w3, n) + [
        pl.BlockSpec((1, tm, 256), lambda bi, i: (bi, i, COL_LOWRANK // 256)),
        const((3, w3)), const((2, w)), const((2, 256, w)), const((2, w)), const((2, 256, w)), const((256, w)),
        const((1, w)), const((1, w)), const((1, w)), const((w, w))]
    out = jax.ShapeDtypeStruct((b, n, w), F32)
    return pl.pallas_call(
        _rwkv_prep_kernel,
        out_shape=(out,) * 11,
        grid=(b, n // tm),
        in_specs=in_specs,
        out_specs=[pl.BlockSpec((1, tm, w), lambda bi, i: (bi, i, 0))] * 11,
        compiler_params=_cparams(("parallel", "parallel")),
        name="rwkv_prep",
    )(p, p, p, p, lw['rwkv_conv_w'], lw['rwkv_w0'], lw['w_up_pad'], lw['rwkv_a0'], lw['a_up_pad'], lw['g_up_pad'],
      lw['rwkv_k_k'], lw['rwkv_k_a'], lw['rwkv_r_k'], lw['ones64'])


V_SLABS = B_HD // 2 // SUBLANES


def _scan_kernel(w_ref, b_ref, kd_ref, r_ref, kk_ref, kkx_ref, v_ref, y_ref, s_ref, sa_ref, *, tc):
    @pl.when(pl.program_id(0) == 0)
    def _():
        s_ref[...] = jnp.zeros(s_ref.shape, F32)
        sa_ref[...] = jnp.zeros(sa_ref.shape, F32)

    lanes = s_ref.shape[-1]
    vrows = B_HD // 2
    bcast = lambda row: jnp.broadcast_to(row, (SUBLANES, lanes))

    def step(t, last_of_block):
        base = pl.multiple_of(t * vrows, vrows)
        sa = [sa_ref[s] for s in range(V_SLABS)]
        vv = [v_ref[pl.ds(base + SUBLANES * s, SUBLANES), :] for s in range(V_SLABS)]
        y = [jnp.zeros((SUBLANES, lanes), F32) for _ in range(V_SLABS)]
        san = [jnp.zeros((SUBLANES, lanes), F32) for _ in range(V_SLABS)]
        for k in range(B_HD):
            row = lambda ref: bcast(ref[k, pl.ds(t, 1), :])
            wk, bk, kdk, rk = row(w_ref), row(b_ref), row(kd_ref), row(r_ref)
            kkn = bcast(kkx_ref[k, 0:1, :]) if last_of_block else bcast(kk_ref[k, pl.ds(t + 1, 1), :])
            for s in range(V_SLABS):
                new = s_ref[s, k] * wk - sa[s] * bk + vv[s] * kdk
                s_ref[s, k] = new
                y[s] = y[s] + new * rk
                san[s] = san[s] + new * kkn
        for s in range(V_SLABS):
            y_ref[pl.ds(base + SUBLANES * s, SUBLANES), :] = y[s]
            sa_ref[s] = san[s]

    def body(t, carry):
        step(t, False)
        return carry

    lax.fori_loop(0, tc - 1, body, 0)
    step(tc - 1, True)


def _rwkv_scan(w, bb, kd, r, kk, v):
    _, t, lanes = w.shape
    tc = _pick(t, 64)
    vrows = B_HD // 2
    last8 = t // SUBLANES - 1
    kspec = pl.BlockSpec((B_HD, tc, lanes), lambda i: (0, i, 0))
    kxspec = pl.BlockSpec((B_HD, SUBLANES, lanes), lambda i: (0, jnp.minimum((i + 1) * (tc // SUBLANES), last8), 0))
    vspec = pl.BlockSpec((tc * vrows, lanes), lambda i: (i, 0))
    return pl.pallas_call(
        functools.partial(_scan_kernel, tc=tc),
        out_shape=jax.ShapeDtypeStruct(v.shape, F32),
        grid=(t // tc,),
        in_specs=[kspec] * 5 + [kxspec, vspec],
        out_specs=vspec,
        scratch_shapes=[pltpu.VMEM((V_SLABS, B_HD, SUBLANES, lanes), F32), pltpu.VMEM((V_SLABS, SUBLANES, lanes), F32)],
        compiler_params=_cparams(("arbitrary",)),
        name="rwkv_scan",
    )(w, bb, kd, r, kk, kk, v)


TIME_BLOCK = 128
HEAD_PAIRS = B_WIDTH // LANES


def _scan_block_maps(nctx, nlat):
    fwd_c = lambda i: jnp.minimum(i, nctx - 1)
    fwd_l = lambda i: jnp.clip(i - nctx, 0, nlat - 1)
    bwd_c = lambda i: jnp.clip(nctx - 1 - i, 0, nctx - 1)
    bwd_l = lambda i: jnp.clip(nlat - 1 - (i - nctx), 0, nlat - 1)
    return fwd_c, fwd_l, bwd_c, bwd_l


def _fill_transposed(u_ref, src_ref, direction, rev_ref, bsz):
    for b in range(bsz):
        for j in range(HEAD_PAIRS):
            tile = src_ref[b, :, j * LANES:(j + 1) * LANES]
            if direction == 1:
                tile = _permute_rows(rev_ref[...], tile)
            q = (direction * bsz + b) * HEAD_PAIRS + j
            u_ref[q * LANES:(q + 1) * LANES, :] = tile.T


def _fill_both(u_ref, cf_ref, lf_ref, cb_ref, lb_ref, rev_ref, nctx, bsz):
    i = pl.program_id(0)

    @pl.when(i < nctx)
    def _():
        _fill_transposed(u_ref, cf_ref, 0, rev_ref, bsz)
        _fill_transposed(u_ref, cb_ref, 1, rev_ref, bsz)

    @pl.when(i >= nctx)
    def _():
        _fill_transposed(u_ref, lf_ref, 0, rev_ref, bsz)
        _fill_transposed(u_ref, lb_ref, 1, rev_ref, bsz)


def _to_scan_k_kernel(cf_ref, lf_ref, cb_ref, lb_ref, rev_ref, o_ref, u_ref, *, nctx, bsz):
    _fill_both(u_ref, cf_ref, lf_ref, cb_ref, lb_ref, rev_ref, nctx, bsz)
    nrow = u_ref.shape[0] // B_HD
    for k in range(B_HD):
        g = u_ref[pl.ds(k, nrow, stride=B_HD), :]
        o_ref[k] = jnp.concatenate([g, g], axis=0).T


def _to_scan_v_kernel(cf_ref, lf_ref, cb_ref, lb_ref, rev_ref, o_ref, u_ref, *, nctx, bsz):
    _fill_both(u_ref, cf_ref, lf_ref, cb_ref, lb_ref, rev_ref, nctx, bsz)
    nrow = u_ref.shape[0] // B_HD
    vrows = B_HD // 2
    for vi in range(vrows):
        halves = [u_ref[pl.ds(half * vrows + vi, nrow, stride=B_HD), :] for half in range(2)]
        o_ref[pl.ds(vi, TIME_BLOCK, stride=vrows), :] = jnp.concatenate(halves, axis=0).T


def _to_scan(ctx_f, lat_f, ctx_b, lat_b, rev, values):
    bsz, nc, width = ctx_f.shape
    nl = lat_f.shape[1]
    nctx, nlat = nc // TIME_BLOCK, nl // TIME_BLOCK
    lanes = 4 * bsz * B_HEADS
    t = nc + nl
    maps = _scan_block_maps(nctx, nlat)
    src = lambda f: pl.BlockSpec((bsz, TIME_BLOCK, width), lambda i, f=f: (0, f(i), 0))
    if values:
        kern, out_shape = _to_scan_v_kernel, (t * (B_HD // 2), lanes)
        out_spec = pl.BlockSpec((TIME_BLOCK * (B_HD // 2), lanes), lambda i: (i, 0))
    else:
        kern, out_shape = _to_scan_k_kernel, (B_HD, t, lanes)
        out_spec = pl.BlockSpec((B_HD, TIME_BLOCK, lanes), lambda i: (0, i, 0))
    return pl.pallas_call(
        functools.partial(kern, nctx=nctx, bsz=bsz),
        out_shape=jax.ShapeDtypeStruct(out_shape, F32),
        grid=(nctx + nlat,),
        in_specs=[src(maps[0]), src(maps[1]), src(maps[2]), src(maps[3]),
                  pl.BlockSpec((TIME_BLOCK, TIME_BLOCK), lambda i: (0, 0))],
        out_specs=out_spec,
        scratch_shapes=[pltpu.VMEM((2 * bsz * HEAD_PAIRS * LANES, TIME_BLOCK), F32)],
        compiler_params=_cparams(("parallel",)),
        name="to_scan_v" if values else "to_scan_k",
    )(ctx_f, lat_f, ctx_b, lat_b, rev)


def _from_scan_kernel(y_ref, rev_ref, o0_ref, o1_ref, u_ref, *, bsz):
    nrow = u_ref.shape[0] // B_HD
    vrows = B_HD // 2
    for vi in range(vrows):
        mt = y_ref[pl.ds(vi, TIME_BLOCK, stride=vrows), :].T
        for half in range(2):
            u_ref[pl.ds(half * vrows + vi, nrow, stride=B_HD), :] = mt[half * nrow:(half + 1) * nrow]
    for direction, o_ref in enumerate((o0_ref, o1_ref)):
        for b in range(bsz):
            for j in range(HEAD_PAIRS):
                q = (direction * bsz + b) * HEAD_PAIRS + j
                tile = u_ref[q * LANES:(q + 1) * LANES, :].T
                if direction == 1:
                    tile = _permute_rows(rev_ref[...], tile)
                o_ref[b, :, j * LANES:(j + 1) * LANES] = tile


def _from_scan(y, rev, bsz, nc, nl):
    nctx, nlat = nc // TIME_BLOCK, nl // TIME_BLOCK
    lanes = y.shape[1]
    vrows = B_HD // 2
    bwd = lambda i: jnp.where(i < nctx, nctx - 1 - i, 2 * nctx + nlat - 1 - i)
    out = jax.ShapeDtypeStruct((bsz, nc + nl, B_WIDTH), F32)
    return pl.pallas_call(
        functools.partial(_from_scan_kernel, bsz=bsz),
        out_shape=(out, out),
        grid=(nctx + nlat,),
        in_specs=[pl.BlockSpec((TIME_BLOCK * vrows, lanes), lambda i: (i, 0)),
                  pl.BlockSpec((TIME_BLOCK, TIME_BLOCK), lambda i: (0, 0))],
        out_specs=[pl.BlockSpec((bsz, TIME_BLOCK, B_WIDTH), lambda i: (0, i, 0)),
                   pl.BlockSpec((bsz, TIME_BLOCK, B_WIDTH), lambda i: (0, bwd(i), 0))],
        scratch_shapes=[pltpu.VMEM((2 * bsz * HEAD_PAIRS * LANES, TIME_BLOCK), F32)],
        compiler_params=_cparams(("parallel",)),
        name="from_scan",
    )(y, rev)


def _rwkv_bidirectional(prep_c, prep_l):
    bsz, nc, _ = prep_c['r'].shape
    nl = prep_l['r'].shape[1]
    rev = jnp.asarray(np.eye(TIME_BLOCK, dtype=np.float32)[::-1].copy(), BF16)
    pair = lambda fwd, bwd, values=False: _to_scan(prep_c[fwd], prep_l[fwd], prep_c[bwd], prep_l[bwd], rev, values)
    y = _rwkv_scan(pair('dec0', 'dec1'), pair('b0', 'b1'), pair('kd0', 'kd1'), pair('r', 'r'), pair('kk', 'kk'),
                   pair('v', 'v', True))
    return _from_scan(y, rev, bsz, nc, nl)


def _merge_kernel(a_ref, o0_ref, o1_ref, bonus_ref, g_ref, cv_ref, cprev_ref, cnext_ref, ga_ref, gb_ref, gc_ref,
                  x_ref, lng_ref, lnb_ref, cw_ref, wb_ref, wo_ref, ones_ref, g1_ref, n2_ref, sc_ref, sh_ref,
                  x_o, hf_o, hb_o):
    gmean = lambda t: _group_sum(t, ones_ref[...]) * (1.0 / B_HD)

    def group_norm(o):
        dlt = o - gmean(o)
        return dlt * lax.rsqrt(gmean(dlt * dlt) + GN_EPS) * lng_ref[...] + lnb_ref[...]

    yb = (group_norm(o0_ref[0]) + group_norm(o1_ref[0]) + bonus_ref[0]) * g_ref[0]

    w = C_WIDTH
    u = lambda blk: blk[:, w:2 * w] * blk[:, 2 * w:3 * w]
    cv = cv_ref[0].astype(F32)
    i = pl.program_id(1)
    pr = u(cprev_ref[0].astype(F32)[HALO_ROWS - 1:HALO_ROWS])
    nr = u(cnext_ref[0].astype(F32)[0:1])
    pr = jnp.where(i == 0, jnp.zeros_like(pr), pr)
    nr = jnp.where(i == pl.num_programs(1) - 1, jnp.zeros_like(nr), nr)
    yc = cv[:, 0:w] * _dwconv3(u(cv), pr, nr, cw_ref)

    d = D_MODEL
    m = jnp.zeros((cv.shape[0], d), F32)
    for j, (yj, gate_ref) in enumerate(((a_ref[0], ga_ref), (yb, gb_ref), (yc, gc_ref))):
        gate = jax.nn.sigmoid(gate_ref[0].astype(F32))
        m = m + gate * jnp.dot(yj.astype(BF16), wb_ref[j], preferred_element_type=F32)
    ymix = jnp.dot(m.astype(BF16), wo_ref[...], preferred_element_type=F32)
    x1 = x_ref[0] + g1_ref[0] * ymix
    x_o[0] = x1
    ms = jnp.mean(x1 * x1, axis=-1, keepdims=True)
    h = x1 * lax.rsqrt(ms + NORM_EPS) * n2_ref[...] * (1.0 + sc_ref[0]) + sh_ref[0]
    hf_o[0] = h
    hb_o[0] = h.astype(BF16)


def _merge(a, o0, o1, row_off, bonus, g, p, x, lw, g1, sc2, sh2):
    b, n, d = x.shape
    tm = _pick(n, 256)
    w = C_WIDTH
    assert row_off % tm == 0
    tile = lambda width: pl.BlockSpec((1, tm, width), lambda bi, i: (bi, i, 0))
    scan_tile = pl.BlockSpec((1, tm, w), lambda bi, i: (bi, i + row_off // tm, 0))
    const = lambda shape: pl.BlockSpec(shape, lambda bi, i: (0,) * len(shape))
    per_b = pl.BlockSpec((1, 1, d), lambda bi, i: (bi, 0, 0))
    gate = lambda j: pl.BlockSpec((1, tm, d), lambda bi, i, j=j: (bi, i, COL_GATES // d + j))
    in_specs = ([tile(w), scan_tile, scan_tile, tile(w), tile(w)]
                + _halo_specs(tm, 3 * w, COL_CONV // (3 * w), n) + [gate(0), gate(1), gate(2)]
                + [tile(d), const((1, w)), const((1, w)), const((3, w)), const((N_BRANCH, w, d)), const((d, d)),
                   const((w, w)), per_b, const((1, d)), per_b, per_b])
    return pl.pallas_call(
        _merge_kernel,
        out_shape=(jax.ShapeDtypeStruct((b, n, d), F32), jax.ShapeDtypeStruct((b, n, d), F32),
                   jax.ShapeDtypeStruct((b, n, d), BF16)),
        grid=(b, n // tm),
        in_specs=in_specs,
        out_specs=[tile(d)] * 3,
        compiler_params=_cparams(("parallel", "parallel")),
        name="merge_mixers",
    )(a, o0, o1, bonus, g, p, p, p, p, p, p, x, lw['rwkv_ln_g'], lw['rwkv_ln_b'], lw['conv_w'], lw['w_branch_bf16'],
      lw['w_out_bf16'], lw['ones64'], g1, lw['norm2_g'], sc2, sh2)


def _router_kernel(h_ref, rw_ref, o_ref):
    logits = lax.dot_general(rw_ref[...], h_ref[0], (((1,), (1,)), ((), ())), precision=HIGHEST,
                             preferred_element_type=F32)
    z = jnp.exp(logits - jnp.max(logits, axis=0, keepdims=True))
    o_ref[0] = z / jnp.sum(z, axis=0, keepdims=True)


def _router(h, rw_t):
    b, n, d = h.shape
    tm = _pick(n, 512)
    return pl.pallas_call(
        _router_kernel,
        out_shape=jax.ShapeDtypeStruct((b, N_EXPERTS, n), F32),
        grid=(b, n // tm),
        in_specs=[pl.BlockSpec((1, tm, d), lambda bi, i: (bi, i, 0)), pl.BlockSpec((N_EXPERTS, d), lambda bi, i: (0, 0))],
        out_specs=pl.BlockSpec((1, N_EXPERTS, tm), lambda bi, i: (bi, 0, i)),
        compiler_params=_cparams(("parallel", "parallel")),
        name="router_affinity",
    )(h, rw_t)


def _select_kernel(aff_ref, tri_ref, pos_o, cnt_o, *, cap, lb):
    aff = aff_ref[0]
    n = aff.shape[1]
    bits = pltpu.bitcast(aff, jnp.int32)
    count = lambda mask: jnp.sum(mask.astype(F32), axis=1, keepdims=True)

    def refine(i, lo):
        cand = lo | jnp.left_shift(jnp.int32(1), 30 - i)
        return jnp.where(count(bits >= cand) >= cap, cand, lo)

    thr = lax.fori_loop(0, 31, refine, jnp.zeros((N_EXPERTS, 1), jnp.int32))
    gt = bits > thr
    eq = bits == thr
    need = cap - count(gt)
    tri = tri_ref[...]

    def prefix(mask_of_block, write):
        off = jnp.zeros((N_EXPERTS, 1), F32)
        for j in range(n // lb):
            m = mask_of_block(j)
            excl = jnp.dot(m.astype(BF16), tri, preferred_element_type=F32) + off
            write(j, m, excl)
            off = off + jnp.sum(m.astype(F32), axis=1, keepdims=True)

    blk = lambda x, j: x[:, j * lb:(j + 1) * lb]
    sel_blocks = [None] * (n // lb)

    def write_sel(j, m, excl):
        sel_blocks[j] = blk(gt, j) | (m & (excl < need))

    prefix(lambda j: blk(eq, j), write_sel)

    def write_pos(j, m, excl):
        c = excl.astype(jnp.int32)
        cnt_o[0, :, j * lb:(j + 1) * lb] = c
        pos_o[0, :, j * lb:(j + 1) * lb] = jnp.where(m, c, -1)

    prefix(lambda j: sel_blocks[j], write_pos)


def _select(aff, cap):
    b, e, n = aff.shape
    lb = min(n, LANES)
    tri = jnp.asarray(np.triu(np.ones((lb, lb), np.float32), 1), BF16)
    blk = pl.BlockSpec((1, e, n), lambda bi: (bi, 0, 0))
    out = jax.ShapeDtypeStruct((b, e, n), jnp.int32)
    return pl.pallas_call(
        functools.partial(_select_kernel, cap=cap, lb=lb),
        out_shape=(out, out),
        grid=(b,),
        in_specs=[blk, pl.BlockSpec((lb, lb), lambda bi: (0, 0))],
        out_specs=[blk, blk],
        compiler_params=_cparams(("parallel",)),
        name="expert_select",
    )(aff, tri)


GATHER_CHUNK = 64


def _expert_kernel(starts_ref, pos_ref, aff_ref, h_ref, w1_ref, w3_ref, w2_ref, y_o, xin_s, gate_s,
                   *, cap, tile, ntile, tpg, nchunk):
    e, bi, g = pl.program_id(0), pl.program_id(1), pl.program_id(2)

    @pl.when(g == 0)
    def _():
        xin_s[...] = jnp.zeros(xin_s.shape, F32)
        gate_s[...] = jnp.zeros(gate_s.shape, F32)

    for tt in range(tpg):
        base = (e * pl.num_programs(1) + bi) * (ntile + 1) + g * tpg + tt
        start, end = starts_ref[base], starts_ref[base + 1]
        w0 = (start // SUBLANES) * SUBLANES
        cols = slice(tt * tile, (tt + 1) * tile)
        for c in range(nchunk):
            r0 = pl.multiple_of(w0 + c * GATHER_CHUNK, SUBLANES)

            @pl.when((r0 < end) & (r0 + GATHER_CHUNK > start))
            def _():
                slot = r0 + lax.broadcasted_iota(jnp.int32, (GATHER_CHUNK, tile), 0)
                onehot = pos_ref[0, 0][:, cols] == slot
                rows = pl.ds(r0, GATHER_CHUNK)
                xin_s[rows, :] += jnp.dot(onehot.astype(BF16), h_ref[0, cols, :], preferred_element_type=F32)
                gate = jnp.sum(jnp.where(onehot, aff_ref[0, 0][:, cols], 0.0), axis=1, keepdims=True)
                gate_s[rows, :] += jnp.broadcast_to(gate, (GATHER_CHUNK, gate_s.shape[1]))

    @pl.when(g == pl.num_programs(2) - 1)
    def _():
        rc = min(cap, SLOT_BLOCK)
        capp = y_o.shape[2]
        if capp > cap:
            y_o[0, 0, cap:capp, :] = jnp.zeros((capp - cap, y_o.shape[3]), BF16)
        for c in range(cap // rc):
            x = xin_s[c * rc:(c + 1) * rc, :].astype(BF16)
            hid = _silu(jnp.dot(x, w1_ref[0], preferred_element_type=F32)) * jnp.dot(x, w3_ref[0], preferred_element_type=F32)
            y = jnp.dot(hid.astype(BF16), w2_ref[0], preferred_element_type=F32)
            y_o[0, 0, c * rc:(c + 1) * rc, :] = (y * gate_s[c * rc:(c + 1) * rc, 0:1]).astype(BF16)


def _moe_tile(n):
    return _pick(n, 256)


def _experts(starts, pos4, aff4, h_bf16, w1, w3, w2, cap):
    b, n, d = h_bf16.shape
    tile = _moe_tile(n)
    ntile = n // tile
    nchunk = tile // GATHER_CHUNK + 1
    rows = cap + nchunk * GATHER_CHUNK
    capp = -(-cap // SLOT_BLOCK) * SLOT_BLOCK
    f = w1.shape[2]
    tg = _pick(n, 2048)
    tpg = tg // tile
    grid_spec = pltpu.PrefetchScalarGridSpec(
        num_scalar_prefetch=1,
        grid=(N_EXPERTS, b, n // tg),
        in_specs=[pl.BlockSpec((1, 1, 1, tg), lambda e, bi, g, s: (bi, e, 0, g)),
                  pl.BlockSpec((1, 1, 1, tg), lambda e, bi, g, s: (bi, e, 0, g)),
                  pl.BlockSpec((1, tg, d), lambda e, bi, g, s: (bi, g, 0)),
                  pl.BlockSpec((1, d, f), lambda e, bi, g, s: (e, 0, 0)),
                  pl.BlockSpec((1, d, f), lambda e, bi, g, s: (e, 0, 0)),
                  pl.BlockSpec((1, f, d), lambda e, bi, g, s: (e, 0, 0))],
        out_specs=pl.BlockSpec((1, 1, capp, d), lambda e, bi, g, s: (e, bi, 0, 0)),
        scratch_shapes=[pltpu.VMEM((rows, d), F32), pltpu.VMEM((rows, LANES), F32)])
    return pl.pallas_call(
        functools.partial(_expert_kernel, cap=cap, tile=tile, ntile=ntile, tpg=tpg, nchunk=nchunk),
        out_shape=jax.ShapeDtypeStruct((N_EXPERTS, b, capp, d), BF16),
        grid_spec=grid_spec,
        compiler_params=_cparams(("arbitrary", "arbitrary", "arbitrary")),
        name="expert_gather_ffn",
    )(starts, pos4, aff4, h_bf16, w1, w3, w2)


def _combine_kernel(starts_ref, pos_ref, *refs, tile, tpg, ntile, sb, nblk):
    y_refs, (x_ref, g2_ref, o_ref) = refs[:nblk], refs[nblk:]
    bi, gi, e = pl.program_id(0), pl.program_id(1), pl.program_id(2)

    @pl.when(e == 0)
    def _():
        o_ref[...] = jnp.zeros(o_ref.shape, F32)

    base = (e * pl.num_programs(0) + bi) * (ntile + 1) + gi * tpg
    k0 = starts_ref[base] // sb
    lane = lax.broadcasted_iota(jnp.int32, (1, LANES), 1)
    for tt in range(tpg):
        start, end = starts_ref[base + tt], starts_ref[base + tt + 1]
        cols = slice(tt * tile, (tt + 1) * tile)
        pos_t = jnp.broadcast_to(pos_ref[0, 0][:, cols].astype(F32), (LANES, tile)).T
        for j, y_ref in enumerate(y_refs):
            s0 = (k0 + j) * sb

            @pl.when((s0 < end) & (s0 + sb > start) & (end > start))
            def _():
                blocks = [pos_t == (s0 + jb * LANES + lane).astype(F32) for jb in range(sb // LANES)]
                onehot_t = jnp.concatenate(blocks, axis=1).astype(BF16)
                o_ref[0, cols, :] += jnp.dot(onehot_t, y_ref[0, 0], preferred_element_type=F32)

    @pl.when(e == pl.num_programs(2) - 1)
    def _():
        o_ref[0] = x_ref[0] + g2_ref[0] * o_ref[0]


def _combine(starts, pos4, y, x, g2):
    b, n, d = x.shape
    tile = _moe_tile(n)
    ntile = n // tile
    gt = _pick(n, 1024)
    tpg = gt // tile
    sb = SLOT_BLOCK
    nsb = y.shape[2] // sb
    nblk = min(gt // sb + 1, nsb)

    def yblk(j):
        def imap(bi, gi, e, s):
            k0 = s[(e * b + bi) * (ntile + 1) + gi * tpg] // sb
            return (e, bi, jnp.minimum(k0 + j, nsb - 1), 0)
        return pl.BlockSpec((1, 1, sb, d), imap)

    grid_spec = pltpu.PrefetchScalarGridSpec(
        num_scalar_prefetch=1,
        grid=(b, n // gt, N_EXPERTS),
        in_specs=[pl.BlockSpec((1, 1, 1, gt), lambda bi, gi, e, s: (bi, e, 0, gi))]
                 + [yblk(j) for j in range(nblk)]
                 + [pl.BlockSpec((1, gt, d), lambda bi, gi, e, s: (bi, gi, 0)),
                    pl.BlockSpec((1, 1, d), lambda bi, gi, e, s: (bi, 0, 0))],
        out_specs=pl.BlockSpec((1, gt, d), lambda bi, gi, e, s: (bi, gi, 0)),
        scratch_shapes=[])
    return pl.pallas_call(
        functools.partial(_combine_kernel, tile=tile, tpg=tpg, ntile=ntile, sb=sb, nblk=nblk),
        out_shape=jax.ShapeDtypeStruct((b, n, d), F32),
        grid_spec=grid_spec,
        compiler_params=_cparams(("parallel", "parallel", "arbitrary")),
        name="expert_combine",
    )(starts, pos4, *([y] * nblk), x, g2)


def _moe(h_f32, h_bf16, x, g2, lw):
    b, n, _ = x.shape
    cap = CAPACITY_FACTOR * n // N_EXPERTS
    aff = _router(h_f32, lw['router_w_t'])
    pos, cnt = _select(aff, cap)
    tile = _moe_tile(n)
    starts = jnp.concatenate([cnt[:, :, ::tile], jnp.full((b, N_EXPERTS, 1), cap, jnp.int32)], axis=2)
    starts = jnp.transpose(starts, (1, 0, 2)).reshape(-1)
    pos4 = pos.reshape(b, N_EXPERTS, 1, n)
    aff4 = aff.reshape(b, N_EXPERTS, 1, n)
    y = _experts(starts, pos4, aff4, h_bf16, lw['w1_bf16'], lw['w3_bf16'], lw['w2_bf16'], cap)
    return _combine(starts, pos4, y, x, g2)


def _pad_w_in(w_in):
    d = w_in.shape[0]
    sizes = (A_WIDTH,) * 3 + (B_WIDTH,) * 3 + (B_DECAY_RANK, B_ICL_RANK, B_GATE_RANK) + (C_WIDTH,) * 3 + (N_BRANCH * D_MODEL,)
    cuts = np.cumsum((0,) + sizes)
    piece = lambda lo, hi: w_in[:, int(cuts[lo]):int(cuts[hi])]
    zeros = jnp.zeros((d, COL_GATES - COL_LOWRANK - 256), w_in.dtype)
    out = jnp.concatenate([piece(0, 3), piece(3, 6), piece(9, 12), piece(6, 9), zeros, piece(12, 13)], axis=1)
    assert out.shape[1] == N_IN_PAD
    return out.astype(BF16)


def _rows_at(w, lo, total=256):
    pad = [(0, 0)] * w.ndim
    pad[-2] = (lo, total - lo - w.shape[-2])
    return jnp.pad(w, pad)


def _rope_tables(rows):
    half = A_HD // 2
    inv = jnp.power(ROPE_THETA, -jnp.arange(0, half, 2, dtype=F32) / half)
    r = jnp.repeat(jnp.arange(rows, dtype=F32), GRID_W)
    col = jnp.tile(jnp.arange(GRID_W, dtype=F32), rows)
    ang = jnp.concatenate([r[:, None] * inv, col[:, None] * inv], axis=-1)
    cos, sin = jnp.cos(ang), jnp.sin(ang)
    reps = A_WIDTH // A_HD
    cos_t = jnp.tile(jnp.concatenate([cos, cos], axis=-1), (1, reps))
    sin_t = jnp.tile(jnp.concatenate([-sin, sin], axis=-1), (1, reps))
    return cos_t, sin_t


def _layer_weights(i, norm2_g, w_in, q_norm_g, k_norm_g, diff_lambda, diff_subln_g, rwkv_conv_w, rwkv_w0, rwkv_w_up,
                   rwkv_a0, rwkv_a_up, rwkv_g_up, rwkv_k_k, rwkv_k_a, rwkv_r_k, rwkv_ln_g, rwkv_ln_b, conv_w,
                   w_branch, w_out, router_w, exp_w1, exp_w3, exp_w2):
    row = lambda v: v.reshape(1, -1)
    return {
        'norm2_g': row(norm2_g[i]),
        'w_in_pad': _pad_w_in(w_in[i]),
        'q_g': row(jnp.tile(q_norm_g[i], A_WIDTH // A_HD)), 'k_g': row(jnp.tile(k_norm_g[i], A_WIDTH // A_HD)),
        'diff_lambda': diff_lambda[i], 'subln_g': row(diff_subln_g[i]),
        'rwkv_conv_w': rwkv_conv_w[i], 'rwkv_w0': rwkv_w0[i], 'rwkv_a0': rwkv_a0[i],
        'w_up_pad': _rows_at(rwkv_w_up[i], 0), 'a_up_pad': _rows_at(rwkv_a_up[i], B_DECAY_RANK),
        'g_up_pad': _rows_at(rwkv_g_up[i], B_DECAY_RANK + B_ICL_RANK),
        'rwkv_k_k': row(rwkv_k_k[i]), 'rwkv_k_a': row(rwkv_k_a[i]), 'rwkv_r_k': row(rwkv_r_k[i]),
        'rwkv_ln_g': row(rwkv_ln_g[i]), 'rwkv_ln_b': row(rwkv_ln_b[i]),
        'conv_w': conv_w[i], 'w_branch_bf16': w_branch[i].astype(BF16), 'w_out_bf16': w_out[i].astype(BF16),
        'router_w_t': router_w[i].T,
        'w1_bf16': exp_w1[i].astype(BF16), 'w3_bf16': exp_w3[i].astype(BF16), 'w2_bf16': exp_w2[i].astype(BF16),
        'ones64': _group_ones(B_WIDTH, B_HD).astype(BF16),
    }


def _mixer_inputs(x, norm1_g, sc1, sh1, lw, cos_t, sin_t):
    p = _norm_mod_proj(x, norm1_g, sc1, sh1, lw['w_in_pad'])
    q, k = _qkv_prep(p, lw['q_g'], lw['k_g'], lw['ones64'], cos_t, sin_t)
    names = ('r', 'v', 'kk', 'g', 'bonus', 'dec0', 'b0', 'kd0', 'dec1', 'b1', 'kd1')
    prep = dict(zip(names, _rwkv_prep(p, lw)))
    return p, (q, k), prep


def kernel(x, c, ctx, c_ctx, norm1_g, norm2_g, ada_w, ada_b, w_in, q_norm_g, k_norm_g, diff_lambda, diff_subln_g,
           rwkv_conv_w, rwkv_w0, rwkv_w_up, rwkv_a0, rwkv_a_up, rwkv_g_up, rwkv_k_k, rwkv_k_a, rwkv_r_k, rwkv_ln_g,
           rwkv_ln_b, conv_w, w_branch, w_out, router_w, exp_w1, exp_w3, exp_w2):
    bsz, n_lat, d = x.shape
    depth = ada_w.shape[0]
    cos_t, sin_t = _rope_tables(n_lat // GRID_W)

    cond = jnp.concatenate([c, jnp.broadcast_to(c_ctx[None, :], (SUBLANES - bsz % SUBLANES, d))], axis=0)
    ada = _ada_params(cond, ada_w, ada_b)

    xc, xl = ctx, x
    for i in range(depth):
        lam_init = 0.8 - 0.6 * math.exp(-0.3 * i)
        need_ctx = i != depth - 1
        lw = _layer_weights(i, norm2_g, w_in, q_norm_g, k_norm_g, diff_lambda, diff_subln_g, rwkv_conv_w, rwkv_w0,
                            rwkv_w_up, rwkv_a0, rwkv_a_up, rwkv_g_up, rwkv_k_k, rwkv_k_a, rwkv_r_k, rwkv_ln_g,
                            rwkv_ln_b, conv_w, w_branch, w_out, router_w, exp_w1, exp_w3, exp_w2)
        mods = ada[i].reshape(-1, 6, d)
        lat = [mods[:bsz, j][:, None, :] for j in range(6)]
        cxt = [jnp.broadcast_to(mods[bsz, j][None, None, :], (bsz, 1, d)) for j in range(6)]

        pc, (qc, kc), prep_c = _mixer_inputs(xc, norm1_g[i], cxt[1], cxt[0], lw, None, None)
        pl_, (ql, kl), prep_l = _mixer_inputs(xl, norm1_g[i], lat[1], lat[0], lw, cos_t, sin_t)

        al = _flash_attention(lw['diff_lambda'], lw['subln_g'], ql, kc, pc, kl, pl_, lam_init)
        o0, o1 = _rwkv_bidirectional(prep_c, prep_l)
        n_ctx = xc.shape[1]

        xl, hl, hlb = _merge(al, o0, o1, n_ctx, prep_l['bonus'], prep_l['g'], pl_, xl, lw, lat[2], lat[4], lat[3])
        xl = _moe(hl, hlb, xl, lat[5], lw)
        if need_ctx:
            ac = _ctx_attention(lw['diff_lambda'], lw['subln_g'], qc, kc, pc, lam_init)
            xc, hc, hcb = _merge(ac, o0, o1, 0, prep_c['bonus'], prep_c['g'], pc, xc, lw, cxt[2], cxt[4], cxt[3])
            xc = _moe(hc, hcb, xc, cxt[5], lw)
    return xl
```

```python
import functools
import math

import numpy as np
import jax
import jax.numpy as jnp
from jax import lax
from jax.experimental import pallas as pl
from jax.experimental.pallas import tpu as pltpu

F32 = jnp.float32
BF16 = jnp.bfloat16
HIGHEST = lax.Precision.HIGHEST

D_MODEL = 1024
GRID_W = 64
A_HEADS = 4
A_HD = 64
A_WIDTH = A_HEADS * 2 * A_HD
B_HEADS = 8
B_HD = 64
B_WIDTH = B_HEADS * B_HD
B_DECAY_RANK = 64
B_ICL_RANK = 64
B_GATE_RANK = 128
C_WIDTH = 512
N_BRANCH = 3
N_EXPERTS = 16
EXPERT_FF = 1024
CAPACITY_FACTOR = 2
ROPE_THETA = 10000.0
NORM_EPS = 1e-6
GN_EPS = 64e-5
LOG2_E = math.log2(math.e)

COL_ATTN = 0
COL_RWKV = 1536
COL_CONV = 3072
COL_LOWRANK = 4608
COL_GATES = 5120
N_IN_PAD = 8192

VMEM_LIMIT = 56 * 1024 * 1024
LANES = 128
SUBLANES = 8
SLOT_BLOCK = 256


def _cparams(sem):
    return pltpu.CompilerParams(dimension_semantics=sem, vmem_limit_bytes=VMEM_LIMIT)


def _pick(n, pref):
    t = min(n, pref)
    while n % t:
        t -= SUBLANES
    return t


def _group_ones(width, group):
    idx = np.arange(width) // group
    return jnp.asarray((idx[:, None] == idx[None, :]).astype(np.float32))


def _silu(x):
    return x * jax.nn.sigmoid(x)


def _split2(x):
    hi = x.astype(BF16)
    return hi, (x - hi.astype(F32)).astype(BF16)


def _group_sum(x, ones_bf16):
    hi, lo = _split2(x)
    return (jnp.dot(hi, ones_bf16, preferred_element_type=F32) + jnp.dot(lo, ones_bf16, preferred_element_type=F32))


def _permute_rows(perm_bf16, x):
    hi, lo = _split2(x)
    rest = x - hi.astype(F32) - lo.astype(F32)
    dot = lambda v: jnp.dot(perm_bf16, v, preferred_element_type=F32)
    return dot(hi) + dot(lo) + dot(rest.astype(BF16))


def _dot3(a, b):
    a_hi, a_lo = _split2(a)
    b_hi, b_lo = _split2(b)
    dot = lambda u, v: jnp.dot(u, v, preferred_element_type=F32)
    return dot(a_hi, b_hi) + (dot(a_hi, b_lo) + dot(a_lo, b_hi))


def _ada_kernel(cond_ref, w_ref, b_ref, o_ref):
    c = cond_ref[...]
    o_ref[0] = jnp.dot(_silu(c), w_ref[0], precision=HIGHEST, preferred_element_type=F32) + b_ref[0]


def _ada_params(cond, ada_w, ada_b):
    depth, d, n6 = ada_w.shape
    tn = _pick(n6, 1536)
    return pl.pallas_call(
        _ada_kernel,
        out_shape=jax.ShapeDtypeStruct((depth, cond.shape[0], n6), F32),
        grid=(depth, n6 // tn),
        in_specs=[pl.BlockSpec(cond.shape, lambda i, j: (0, 0)),
                  pl.BlockSpec((1, d, tn), lambda i, j: (i, 0, j)),
                  pl.BlockSpec((1, 1, tn), lambda i, j: (i, 0, j))],
        out_specs=pl.BlockSpec((1, cond.shape[0], tn), lambda i, j: (i, 0, j)),
        compiler_params=_cparams(("parallel", "parallel")),
        name="ada_params",
    )(cond, ada_w, ada_b.reshape(depth, 1, n6))


def _proj_kernel(x_ref, g_ref, sc_ref, sh_ref, w_ref, o_ref):
    x = x_ref[0]
    ms = jnp.mean(x * x, axis=-1, keepdims=True)
    y = x * lax.rsqrt(ms + NORM_EPS) * g_ref[...]
    h = y * (1.0 + sc_ref[0]) + sh_ref[0]
    o_ref[0] = jnp.dot(h.astype(BF16), w_ref[...], preferred_element_type=F32).astype(BF16)


def _norm_mod_proj(x, g, sc, sh, w_bf16):
    b, n, d = x.shape
    npad = w_bf16.shape[1]
    tm = _pick(n, 512)
    tn = 2048
    return pl.pallas_call(
        _proj_kernel,
        out_shape=jax.ShapeDtypeStruct((b, n, npad), BF16),
        grid=(npad // tn, b, n // tm),
        in_specs=[pl.BlockSpec((1, tm, d), lambda j, bi, i: (bi, i, 0)),
                  pl.BlockSpec((1, d), lambda j, bi, i: (0, 0)),
                  pl.BlockSpec((1, 1, d), lambda j, bi, i: (bi, 0, 0)),
                  pl.BlockSpec((1, 1, d), lambda j, bi, i: (bi, 0, 0)),
                  pl.BlockSpec((d, tn), lambda j, bi, i: (0, j))],
        out_specs=pl.BlockSpec((1, tm, tn), lambda j, bi, i: (bi, i, j)),
        compiler_params=_cparams(("parallel", "parallel", "parallel")),
        name="norm_mod_proj",
    )(x, g.reshape(1, d), sc, sh, w_bf16)


def _swap_halves(x, half):
    width = x.shape[-1]
    lane = lax.broadcasted_iota(jnp.int32, (1, width), 1) % (2 * half)
    fwd = pltpu.roll(x, width - half, axis=1)
    bwd = pltpu.roll(x, half, axis=1)
    return jnp.where(lane < half, fwd, bwd)


def _qkv_prep_kernel(*refs, use_rope):
    if use_rope:
        q_ref, k_ref, qg_ref, kg_ref, ones_ref, cos_ref, sin_ref, qo_ref, ko_ref = refs
    else:
        q_ref, k_ref, qg_ref, kg_ref, ones_ref, qo_ref, ko_ref = refs

    def head_norm(x, gain):
        ms = _group_sum(x * x, ones_ref[...]) * (1.0 / A_HD)
        return x * lax.rsqrt(ms + NORM_EPS) * gain

    def rope(x):
        if not use_rope:
            return x
        return x * cos_ref[...] + _swap_halves(x, A_HD // 2) * sin_ref[...]

    q = rope(head_norm(q_ref[0].astype(F32), qg_ref[...]))
    k = rope(head_norm(k_ref[0].astype(F32), kg_ref[...]))
    qo_ref[0] = (q * (A_HD ** -0.5 * LOG2_E)).astype(BF16)
    ko_ref[0] = k.astype(BF16)


def _qkv_prep(p, qg, kg, ones, cos_t, sin_t):
    b, n, _ = p.shape
    tm = _pick(n, 512)
    use_rope = cos_t is not None
    w = A_WIDTH
    col = lambda c: pl.BlockSpec((1, tm, w), lambda bi, i, c=c: (bi, i, COL_ATTN // w + c))
    const = lambda shape: pl.BlockSpec(shape, lambda bi, i: (0,) * len(shape))
    in_specs = [col(0), col(1), const((1, w)), const((1, w)), const((w, w))]
    args = [p, p, qg, kg, ones]
    if use_rope:
        in_specs += [pl.BlockSpec((tm, w), lambda bi, i: (i, 0))] * 2
        args += [cos_t, sin_t]
    out = jax.ShapeDtypeStruct((b, n, w), BF16)
    return pl.pallas_call(
        functools.partial(_qkv_prep_kernel, use_rope=use_rope),
        out_shape=(out, out),
        grid=(b, n // tm),
        in_specs=in_specs,
        out_specs=[pl.BlockSpec((1, tm, w), lambda bi, i: (bi, i, 0))] * 2,
        compiler_params=_cparams(("parallel", "parallel")),
        name="qkv_prep_rope" if use_rope else "qkv_prep",
    )(*args)


def _map_stacked(q):
    lane = lax.broadcasted_iota(jnp.int32, (1, 2 * A_HD), 1)
    zero = jnp.zeros_like(q)
    return jnp.concatenate([jnp.where(lane < A_HD, q, zero), jnp.where(lane >= A_HD, q, zero)], axis=0)


def _lane_tiled(x, width):
    return jnp.concatenate([x] * (width // LANES), axis=1)


def _lane_block_sum(p):
    out = p[:, 0:LANES]
    for j in range(1, p.shape[1] // LANES):
        out = out + p[:, j * LANES:(j + 1) * LANES]
    return out


def _lambda_full(dl_ref, lam_init):
    dl = dl_ref[...]
    s1 = jnp.sum(dl[0:1] * dl[1:2], axis=-1, keepdims=True)
    s2 = jnp.sum(dl[2:3] * dl[3:4], axis=-1, keepdims=True)
    return jnp.exp(s1) - jnp.exp(s2) + lam_init


def _subln(o, sg_ref, lam_init):
    ms = jnp.mean(o * o, axis=-1, keepdims=True)
    return o * lax.rsqrt(ms + NORM_EPS) * sg_ref[...] * (1.0 - lam_init)


def _scores(qm, k):
    return lax.dot_general(qm, k, (((1,), (1,)), ((), ())), preferred_element_type=F32)


def _flash_kernel(dl_ref, sg_ref, q_ref, k_ref, v_ref, o_ref, m_s, l_s, acc_s, *, lam_init, nkv):
    kj = pl.program_id(3)
    tq = q_ref.shape[1]

    @pl.when(kj == 0)
    def _():
        m_s[...] = jnp.full(m_s.shape, -jnp.inf, F32)
        l_s[...] = jnp.zeros(l_s.shape, F32)
        acc_s[...] = jnp.zeros(acc_s.shape, F32)

    s = _scores(_map_stacked(q_ref[0]), k_ref[0])
    m_prev = m_s[...]
    m_new = jnp.maximum(m_prev, jnp.max(s, axis=-1, keepdims=True))
    alpha = jnp.exp2(m_prev - m_new)
    p = jnp.exp2(s - _lane_tiled(m_new, s.shape[1]))
    l_s[...] = alpha * l_s[...] + _lane_block_sum(p)
    acc_s[...] = alpha * acc_s[...] + jnp.dot(p.astype(BF16), v_ref[0], preferred_element_type=F32)
    m_s[...] = m_new

    @pl.when(kj == nkv - 1)
    def _():
        lam = _lambda_full(dl_ref, lam_init)
        o = acc_s[...] / jnp.sum(l_s[...], axis=-1, keepdims=True)
        o_ref[0] = _subln(o[0:tq] - lam * o[tq:2 * tq], sg_ref, lam_init).astype(o_ref.dtype)


V_COL_BLOCK = (COL_ATTN + 2 * A_WIDTH) // (2 * A_HD)


def _pick_lanes(n, pref):
    t = min(n, pref) // LANES * LANES
    while n % t:
        t -= LANES
    return t


def _flash_attention(dl, sg, q, k_all, v_all, lam_init):
    b, s, _ = q.shape
    t = k_all.shape[1]
    hw = 2 * A_HD
    tq = _pick(s, 1024)
    tk = _pick_lanes(t, 3072)
    nkv = t // tk
    return pl.pallas_call(
        functools.partial(_flash_kernel, lam_init=lam_init, nkv=nkv),
        out_shape=jax.ShapeDtypeStruct((b, s, A_WIDTH), BF16),
        grid=(b, A_HEADS, s // tq, nkv),
        in_specs=[pl.BlockSpec((4, A_HD), lambda bi, h, i, j: (0, 0)),
                  pl.BlockSpec((1, hw), lambda bi, h, i, j: (0, 0)),
                  pl.BlockSpec((1, tq, hw), lambda bi, h, i, j: (bi, i, h)),
                  pl.BlockSpec((1, tk, hw), lambda bi, h, i, j: (bi, j, h)),
                  pl.BlockSpec((1, tk, hw), lambda bi, h, i, j: (bi, j, h))],
        out_specs=pl.BlockSpec((1, tq, hw), lambda bi, h, i, j: (bi, i, h)),
        scratch_shapes=[pltpu.VMEM((2 * tq, LANES), F32), pltpu.VMEM((2 * tq, LANES), F32), pltpu.VMEM((2 * tq, hw), F32)],
        compiler_params=_cparams(("parallel", "parallel", "parallel", "arbitrary")),
        name="diff_flash_attention",
    )(dl, sg, q, k_all, v_all)


def _ctx_attn_kernel(dl_ref, sg_ref, q_ref, k_ref, v_ref, o_ref, *, lam_init):
    n = q_ref.shape[1]
    s = _scores(_map_stacked(q_ref[0]), k_ref[0])
    p = jnp.exp2(s - jnp.max(s, axis=-1, keepdims=True))
    o = jnp.dot(p.astype(BF16), v_ref[0], preferred_element_type=F32) / jnp.sum(p, axis=-1, keepdims=True)
    lam = _lambda_full(dl_ref, lam_init)
    o_ref[0] = _subln(o[0:n] - lam * o[n:2 * n], sg_ref, lam_init).astype(o_ref.dtype)


def _ctx_attention(dl, sg, q, k, v, lam_init):
    b, n, _ = q.shape
    hw = 2 * A_HD
    blk = pl.BlockSpec((1, n, hw), lambda bi, h: (bi, 0, h))
    vblk = pl.BlockSpec((1, n, hw), lambda bi, h: (bi, 0, V_COL_BLOCK + h))
    return pl.pallas_call(
        functools.partial(_ctx_attn_kernel, lam_init=lam_init),
        out_shape=jax.ShapeDtypeStruct((b, n, A_WIDTH), BF16),
        grid=(b, A_HEADS),
        in_specs=[pl.BlockSpec((4, A_HD), lambda bi, h: (0, 0)), pl.BlockSpec((1, hw), lambda bi, h: (0, 0)),
                  blk, blk, vblk],
        out_specs=blk,
        compiler_params=_cparams(("parallel", "parallel")),
        name="ctx_attention",
    )(dl, sg, q, k, v)


HALO_ROWS = 16


def _halo_specs(tm, width, colblk, n):
    rh = tm // HALO_ROWS
    last = n // HALO_ROWS - 1
    main = pl.BlockSpec((1, tm, width), lambda bi, i: (bi, i, colblk))
    prev = pl.BlockSpec((1, HALO_ROWS, width), lambda bi, i: (bi, jnp.maximum(i * rh - 1, 0), colblk))
    nxt = pl.BlockSpec((1, HALO_ROWS, width), lambda bi, i: (bi, jnp.minimum((i + 1) * rh, last), colblk))
    return [main, prev, nxt]


def _dwconv3(x, prev_row, next_row, w_ref):
    tm = x.shape[0]
    row = lax.broadcasted_iota(jnp.int32, (tm, 1), 0)
    xp = jnp.where(row == 0, prev_row, pltpu.roll(x, 1, axis=0))
    xn = jnp.where(row == tm - 1, next_row, pltpu.roll(x, tm - 1, axis=0))
    w = w_ref[...]
    return w[0:1] * xp + w[1:2] * x + w[2:3] * xn


def _edge_rows(prev_ref, next_ref):
    i = pl.program_id(1)
    n = pl.num_programs(1)
    pr = prev_ref[0].astype(F32)[HALO_ROWS - 1:HALO_ROWS]
    nr = next_ref[0].astype(F32)[0:1]
    pr = jnp.where(i == 0, jnp.zeros_like(pr), pr)
    nr = jnp.where(i == n - 1, jnp.zeros_like(nr), nr)
    return pr, nr


def _softplus(u):
    return jnp.maximum(u, 0.0) + jnp.log(1.0 + jnp.exp(-jnp.abs(u)))


def _rwkv_prep_kernel(rkv_ref, prev_ref, next_ref, lr_ref, cw_ref, w0_ref, wup_ref, a0_ref, aup_ref, gup_ref,
                      kk_ref_w, ka_ref, rk_ref, ones_ref,
                      r_o, v_o, kk_o, g_o, bonus_o, dec0_o, b0_o, kd0_o, dec1_o, b1_o, kd1_o):
    pr, nr = _edge_rows(prev_ref, next_ref)
    rkv = _dwconv3(rkv_ref[0].astype(F32), pr, nr, cw_ref)
    w = B_WIDTH
    r, k, v = rkv[:, 0:w], rkv[:, w:2 * w], rkv[:, 2 * w:3 * w]
    gsum = lambda t: _group_sum(t, ones_ref[...])
    kkr = k * kk_ref_w[...]
    nrm = jnp.maximum(jnp.sqrt(gsum(kkr * kkr)), 1e-12)
    kk = kkr / nrm
    lr = lr_ref[0].astype(F32)
    th = jnp.tanh(lr)
    sg = jax.nn.sigmoid(lr)
    r_o[0] = r
    v_o[0] = v
    kk_o[0] = kk
    g_o[0] = _dot3(sg, gup_ref[...])
    kd_sum = jnp.zeros_like(k)
    outs = ((dec0_o, b0_o, kd0_o), (dec1_o, b1_o, kd1_o))
    for d in range(2):
        z = w0_ref[d:d + 1] + _dot3(th, wup_ref[d])
        wlog = -_softplus(-z) - 0.5
        dec = jnp.exp(-jnp.exp(wlog))
        a = jax.nn.sigmoid(a0_ref[d:d + 1] + _dot3(lr, aup_ref[d]))
        kd = k * (1.0 + (a - 1.0) * ka_ref[...])
        outs[d][0][0] = dec
        outs[d][1][0] = kk * a
        outs[d][2][0] = kd
        kd_sum = kd_sum + kd
    bonus_o[0] = gsum(r * kd_sum * rk_ref[...]) * v


def _rwkv_prep(p, lw):
    b, n, _ = p.shape
    tm = _pick(n, 256)
    w = B_WIDTH
    w3 = 3 * w
    const = lambda shape: pl.BlockSpec(shape, lambda bi, i: (0,) * len(shape))
    in_specs = _halo_specs(tm, w3, COL_RWKV // w3, n) + [
        pl.BlockSpec((1, tm, 256), lambda bi, i: (bi, i, COL_LOWRANK // 256)),
        const((3, w3)), const((2, w)), const((2, 256, w)), const((2, w)), const((2, 256, w)), const((256, w)),
        const((1, w)), const((1, w)), const((1, w)), const((w, w))]
    out = jax.ShapeDtypeStruct((b, n, w), F32)
    return pl.pallas_call(
        _rwkv_prep_kernel,
        out_shape=(out,) * 11,
        grid=(b, n // tm),
        in_specs=in_specs,
        out_specs=[pl.BlockSpec((1, tm, w), lambda bi, i: (bi, i, 0))] * 11,
        compiler_params=_cparams(("parallel", "parallel")),
        name="rwkv_prep",
    )(p, p, p, p, lw['rwkv_conv_w'], lw['rwkv_w0'], lw['w_up_pad'], lw['rwkv_a0'], lw['a_up_pad'], lw['g_up_pad'],
      lw['rwkv_k_k'], lw['rwkv_k_a'], lw['rwkv_r_k'], lw['ones64'])


V_SLABS = B_HD // 2 // SUBLANES


def _scan_kernel(w_ref, b_ref, kd_ref, r_ref, kk_ref, kkx_ref, v_ref, y_ref, s_ref, sa_ref, *, tc):
    @pl.when(pl.program_id(0) == 0)
    def _():
        s_ref[...] = jnp.zeros(s_ref.shape, F32)
        sa_ref[...] = jnp.zeros(sa_ref.shape, F32)

    lanes = s_ref.shape[-1]
    vrows = B_HD // 2
    bcast = lambda row: jnp.broadcast_to(row, (SUBLANES, lanes))

    def step(t, last_of_block):
        base = pl.multiple_of(t * vrows, vrows)
        sa = [sa_ref[s] for s in range(V_SLABS)]
        vv = [v_ref[pl.ds(base + SUBLANES * s, SUBLANES), :] for s in range(V_SLABS)]
        y = [jnp.zeros((SUBLANES, lanes), F32) for _ in range(V_SLABS)]
        san = [jnp.zeros((SUBLANES, lanes), F32) for _ in range(V_SLABS)]
        for k in range(B_HD):
            row = lambda ref: bcast(ref[k, pl.ds(t, 1), :])
            wk, bk, kdk, rk = row(w_ref), row(b_ref), row(kd_ref), row(r_ref)
            kkn = bcast(kkx_ref[k, 0:1, :]) if last_of_block else bcast(kk_ref[k, pl.ds(t + 1, 1), :])
            for s in range(V_SLABS):
                new = s_ref[s, k] * wk - sa[s] * bk + vv[s] * kdk
                s_ref[s, k] = new
                y[s] = y[s] + new * rk
                san[s] = san[s] + new * kkn
        for s in range(V_SLABS):
            y_ref[pl.ds(base + SUBLANES * s, SUBLANES), :] = y[s]
            sa_ref[s] = san[s]

    def body(t, carry):
        step(t, False)
        return carry

    lax.fori_loop(0, tc - 1, body, 0)
    step(tc - 1, True)


def _rwkv_scan(w, bb, kd, r, kk, v):
    _, t, lanes = w.shape
    tc = _pick(t, 64)
    vrows = B_HD // 2
    last8 = t // SUBLANES - 1
    kspec = pl.BlockSpec((B_HD, tc, lanes), lambda i: (0, i, 0))
    kxspec = pl.BlockSpec((B_HD, SUBLANES, lanes), lambda i: (0, jnp.minimum((i + 1) * (tc // SUBLANES), last8), 0))
    vspec = pl.BlockSpec((tc * vrows, lanes), lambda i: (i, 0))
    return pl.pallas_call(
        functools.partial(_scan_kernel, tc=tc),
        out_shape=jax.ShapeDtypeStruct(v.shape, F32),
        grid=(t // tc,),
        in_specs=[kspec] * 5 + [kxspec, vspec],
        out_specs=vspec,
        scratch_shapes=[pltpu.VMEM((V_SLABS, B_HD, SUBLANES, lanes), F32), pltpu.VMEM((V_SLABS, SUBLANES, lanes), F32)],
        compiler_params=_cparams(("arbitrary",)),
        name="rwkv_scan",
    )(w, bb, kd, r, kk, kk, v)


TIME_BLOCK = 128
HEAD_PAIRS = B_WIDTH // LANES


def _scan_block_maps(nctx, nlat):
    fwd_c = lambda i: jnp.minimum(i, nctx - 1)
    fwd_l = lambda i: jnp.clip(i - nctx, 0, nlat - 1)
    bwd_c = lambda i: jnp.clip(nctx - 1 - i, 0, nctx - 1)
    bwd_l = lambda i: jnp.clip(nlat - 1 - (i - nctx), 0, nlat - 1)
    return fwd_c, fwd_l, bwd_c, bwd_l


def _fill_transposed(u_ref, src_ref, direction, rev_ref, bsz):
    for b in range(bsz):
        for j in range(HEAD_PAIRS):
            tile = src_ref[b, :, j * LANES:(j + 1) * LANES]
            if direction == 1:
                tile = _permute_rows(rev_ref[...], tile)
            q = (direction * bsz + b) * HEAD_PAIRS + j
            u_ref[q * LANES:(q + 1) * LANES, :] = tile.T


def _fill_both(u_ref, cf_ref, lf_ref, cb_ref, lb_ref, rev_ref, nctx, bsz):
    i = pl.program_id(0)

    @pl.when(i < nctx)
    def _():
        _fill_transposed(u_ref, cf_ref, 0, rev_ref, bsz)
        _fill_transposed(u_ref, cb_ref, 1, rev_ref, bsz)

    @pl.when(i >= nctx)
    def _():
        _fill_transposed(u_ref, lf_ref, 0, rev_ref, bsz)
        _fill_transposed(u_ref, lb_ref, 1, rev_ref, bsz)


def _to_scan_k_kernel(cf_ref, lf_ref, cb_ref, lb_ref, rev_ref, o_ref, u_ref, *, nctx, bsz):
    _fill_both(u_ref, cf_ref, lf_ref, cb_ref, lb_ref, rev_ref, nctx, bsz)
    nrow = u_ref.shape[0] // B_HD
    for k in range(B_HD):
        g = u_ref[pl.ds(k, nrow, stride=B_HD), :]
        o_ref[k] = jnp.concatenate([g, g], axis=0).T


def _to_scan_v_kernel(cf_ref, lf_ref, cb_ref, lb_ref, rev_ref, o_ref, u_ref, *, nctx, bsz):
    _fill_both(u_ref, cf_ref, lf_ref, cb_ref, lb_ref, rev_ref, nctx, bsz)
    nrow = u_ref.shape[0] // B_HD
    vrows = B_HD // 2
    for vi in range(vrows):
        halves = [u_ref[pl.ds(half * vrows + vi, nrow, stride=B_HD), :] for half in range(2)]
        o_ref[pl.ds(vi, TIME_BLOCK, stride=vrows), :] = jnp.concatenate(halves, axis=0).T


def _to_scan(ctx_f, lat_f, ctx_b, lat_b, rev, values):
    bsz, nc, width = ctx_f.shape
    nl = lat_f.shape[1]
    nctx, nlat = nc // TIME_BLOCK, nl // TIME_BLOCK
    lanes = 4 * bsz * B_HEADS
    t = nc + nl
    maps = _scan_block_maps(nctx, nlat)
    src = lambda f: pl.BlockSpec((bsz, TIME_BLOCK, width), lambda i, f=f: (0, f(i), 0))
    if values:
        kern, out_shape = _to_scan_v_kernel, (t * (B_HD // 2), lanes)
        out_spec = pl.BlockSpec((TIME_BLOCK * (B_HD // 2), lanes), lambda i: (i, 0))
    else:
        kern, out_shape = _to_scan_k_kernel, (B_HD, t, lanes)
        out_spec = pl.BlockSpec((B_HD, TIME_BLOCK, lanes), lambda i: (0, i, 0))
    return pl.pallas_call(
        functools.partial(kern, nctx=nctx, bsz=bsz),
        out_shape=jax.ShapeDtypeStruct(out_shape, F32),
        grid=(nctx + nlat,),
        in_specs=[src(maps[0]), src(maps[1]), src(maps[2]), src(maps[3]),
                  pl.BlockSpec((TIME_BLOCK, TIME_BLOCK), lambda i: (0, 0))],
        out_specs=out_spec,
        scratch_shapes=[pltpu.VMEM((2 * bsz * HEAD_PAIRS * LANES, TIME_BLOCK), F32)],
        compiler_params=_cparams(("parallel",)),
        name="to_scan_v" if values else "to_scan_k",
    )(ctx_f, lat_f, ctx_b, lat_b, rev)


def _from_scan_kernel(y_ref, rev_ref, o0_ref, o1_ref, u_ref, *, bsz):
    nrow = u_ref.shape[0] // B_HD
    vrows = B_HD // 2
    for vi in range(vrows):
        mt = y_ref[pl.ds(vi, TIME_BLOCK, stride=vrows), :].T
        for half in range(2):
            u_ref[pl.ds(half * vrows + vi, nrow, stride=B_HD), :] = mt[half * nrow:(half + 1) * nrow]
    for direction, o_ref in enumerate((o0_ref, o1_ref)):
        for b in range(bsz):
            for j in range(HEAD_PAIRS):
                q = (direction * bsz + b) * HEAD_PAIRS + j
                tile = u_ref[q * LANES:(q + 1) * LANES, :].T
                if direction == 1:
                    tile = _permute_rows(rev_ref[...], tile)
                o_ref[b, :, j * LANES:(j + 1) * LANES] = tile


def _from_scan(y, rev, bsz, nc, nl):
    nctx, nlat = nc // TIME_BLOCK, nl // TIME_BLOCK
    lanes = y.shape[1]
    vrows = B_HD // 2
    bwd = lambda i: jnp.where(i < nctx, nctx - 1 - i, 2 * nctx + nlat - 1 - i)
    out = jax.ShapeDtypeStruct((bsz, nc + nl, B_WIDTH), F32)
    return pl.pallas_call(
        functools.partial(_from_scan_kernel, bsz=bsz),
        out_shape=(out, out),
        grid=(nctx + nlat,),
        in_specs=[pl.BlockSpec((TIME_BLOCK * vrows, lanes), lambda i: (i, 0)),
                  pl.BlockSpec((TIME_BLOCK, TIME_BLOCK), lambda i: (0, 0))],
        out_specs=[pl.BlockSpec((bsz, TIME_BLOCK, B_WIDTH), lambda i: (0, i, 0)),
                   pl.BlockSpec((bsz, TIME_BLOCK, B_WIDTH), lambda i: (0, bwd(i), 0))],
        scratch_shapes=[pltpu.VMEM((2 * bsz * HEAD_PAIRS * LANES, TIME_BLOCK), F32)],
        compiler_params=_cparams(("parallel",)),
        name="from_scan",
    )(y, rev)


def _rwkv_bidirectional(prep_c, prep_l):
    bsz, nc, _ = prep_c['r'].shape
    nl = prep_l['r'].shape[1]
    rev = jnp.asarray(np.eye(TIME_BLOCK, dtype=np.float32)[::-1].copy(), BF16)
    pair = lambda fwd, bwd, values=False: _to_scan(prep_c[fwd], prep_l[fwd], prep_c[bwd], prep_l[bwd], rev, values)
    y = _rwkv_scan(pair('dec0', 'dec1'), pair('b0', 'b1'), pair('kd0', 'kd1'), pair('r', 'r'), pair('kk', 'kk'),
                   pair('v', 'v', True))
    return _from_scan(y, rev, bsz, nc, nl)


def _merge_kernel(a_ref, o0_ref, o1_ref, bonus_ref, g_ref, cv_ref, cprev_ref, cnext_ref, ga_ref, gb_ref, gc_ref,
                  x_ref, lng_ref, lnb_ref, cw_ref, wb_ref, wo_ref, ones_ref, g1_ref, n2_ref, sc_ref, sh_ref,
                  x_o, hf_o, hb_o):
    gmean = lambda t: _group_sum(t, ones_ref[...]) * (1.0 / B_HD)

    def group_norm(o):
        dlt = o - gmean(o)
        return dlt * lax.rsqrt(gmean(dlt * dlt) + GN_EPS) * lng_ref[...] + lnb_ref[...]

    yb = (group_norm(o0_ref[0]) + group_norm(o1_ref[0]) + bonus_ref[0]) * g_ref[0]

    w = C_WIDTH
    u = lambda blk: blk[:, w:2 * w] * blk[:, 2 * w:3 * w]
    cv = cv_ref[0].astype(F32)
    i = pl.program_id(1)
    pr = u(cprev_ref[0].astype(F32)[HALO_ROWS - 1:HALO_ROWS])
    nr = u(cnext_ref[0].astype(F32)[0:1])
    pr = jnp.where(i == 0, jnp.zeros_like(pr), pr)
    nr = jnp.where(i == pl.num_programs(1) - 1, jnp.zeros_like(nr), nr)
    yc = cv[:, 0:w] * _dwconv3(u(cv), pr, nr, cw_ref)

    d = D_MODEL
    m = jnp.zeros((cv.shape[0], d), F32)
    for j, (yj, gate_ref) in enumerate(((a_ref[0], ga_ref), (yb, gb_ref), (yc, gc_ref))):
        gate = jax.nn.sigmoid(gate_ref[0].astype(F32))
        m = m + gate * jnp.dot(yj.astype(BF16), wb_ref[j], preferred_element_type=F32)
    ymix = jnp.dot(m.astype(BF16), wo_ref[...], preferred_element_type=F32)
    x1 = x_ref[0] + g1_ref[0] * ymix
    x_o[0] = x1
    ms = jnp.mean(x1 * x1, axis=-1, keepdims=True)
    h = x1 * lax.rsqrt(ms + NORM_EPS) * n2_ref[...] * (1.0 + sc_ref[0]) + sh_ref[0]
    hf_o[0] = h
    hb_o[0] = h.astype(BF16)


def _merge(a, o0, o1, row_off, bonus, g, p, x, lw, g1, sc2, sh2):
    b, n, d = x.shape
    tm = _pick(n, 256)
    w = C_WIDTH
    assert row_off % tm == 0
    tile = lambda width: pl.BlockSpec((1, tm, width), lambda bi, i: (bi, i, 0))
    scan_tile = pl.BlockSpec((1, tm, w), lambda bi, i: (bi, i + row_off // tm, 0))
    const = lambda shape: pl.BlockSpec(shape, lambda bi, i: (0,) * len(shape))
    per_b = pl.BlockSpec((1, 1, d), lambda bi, i: (bi, 0, 0))
    gate = lambda j: pl.BlockSpec((1, tm, d), lambda bi, i, j=j: (bi, i, COL_GATES // d + j))
    in_specs = ([tile(w), scan_tile, scan_tile, tile(w), tile(w)]
                + _halo_specs(tm, 3 * w, COL_CONV // (3 * w), n) + [gate(0), gate(1), gate(2)]
                + [tile(d), const((1, w)), const((1, w)), const((3, w)), const((N_BRANCH, w, d)), const((d, d)),
                   const((w, w)), per_b, const((1, d)), per_b, per_b])
    return pl.pallas_call(
        _merge_kernel,
        out_shape=(jax.ShapeDtypeStruct((b, n, d), F32), jax.ShapeDtypeStruct((b, n, d), F32),
                   jax.ShapeDtypeStruct((b, n, d), BF16)),
        grid=(b, n // tm),
        in_specs=in_specs,
        out_specs=[tile(d)] * 3,
        compiler_params=_cparams(("parallel", "parallel")),
        name="merge_mixers",
    )(a, o0, o1, bonus, g, p, p, p, p, p, p, x, lw['rwkv_ln_g'], lw['rwkv_ln_b'], lw['conv_w'], lw['w_branch_bf16'],
      lw['w_out_bf16'], lw['ones64'], g1, lw['norm2_g'], sc2, sh2)


def _router_kernel(h_ref, rw_ref, o_ref):
    logits = lax.dot_general(rw_ref[...], h_ref[0], (((1,), (1,)), ((), ())), precision=HIGHEST,
                             preferred_element_type=F32)
    z = jnp.exp(logits - jnp.max(logits, axis=0, keepdims=True))
    o_ref[0] = z / jnp.sum(z, axis=0, keepdims=True)


def _router(h, rw_t):
    b, n, d = h.shape
    tm = _pick(n, 512)
    return pl.pallas_call(
        _router_kernel,
        out_shape=jax.ShapeDtypeStruct((b, N_EXPERTS, n), F32),
        grid=(b, n // tm),
        in_specs=[pl.BlockSpec((1, tm, d), lambda bi, i: (bi, i, 0)), pl.BlockSpec((N_EXPERTS, d), lambda bi, i: (0, 0))],
        out_specs=pl.BlockSpec((1, N_EXPERTS, tm), lambda bi, i: (bi, 0, i)),
        compiler_params=_cparams(("parallel", "parallel")),
        name="router_affinity",
    )(h, rw_t)


def _select_kernel(aff_ref, tri_ref, pos_o, cnt_o, *, cap, lb):
    aff = aff_ref[0]
    n = aff.shape[1]
    bits = pltpu.bitcast(aff, jnp.int32)
    count = lambda mask: jnp.sum(mask.astype(F32), axis=1, keepdims=True)

    def refine(i, lo):
        cand = lo | jnp.left_shift(jnp.int32(1), 30 - i)
        return jnp.where(count(bits >= cand) >= cap, cand, lo)

    thr = lax.fori_loop(0, 31, refine, jnp.zeros((N_EXPERTS, 1), jnp.int32))
    gt = bits > thr
    eq = bits == thr
    need = cap - count(gt)
    tri = tri_ref[...]

    def prefix(mask_of_block, write):
        off = jnp.zeros((N_EXPERTS, 1), F32)
        for j in range(n // lb):
            m = mask_of_block(j)
            excl = jnp.dot(m.astype(BF16), tri, preferred_element_type=F32) + off
            write(j, m, excl)
            off = off + jnp.sum(m.astype(F32), axis=1, keepdims=True)

    blk = lambda x, j: x[:, j * lb:(j + 1) * lb]
    sel_blocks = [None] * (n // lb)

    def write_sel(j, m, excl):
        sel_blocks[j] = blk(gt, j) | (m & (excl < need))

    prefix(lambda j: blk(eq, j), write_sel)

    def write_pos(j, m, excl):
        c = excl.astype(jnp.int32)
        cnt_o[0, :, j * lb:(j + 1) * lb] = c
        pos_o[0, :, j * lb:(j + 1) * lb] = jnp.where(m, c, -1)

    prefix(lambda j: sel_blocks[j], write_pos)


def _select(aff, cap):
    b, e, n = aff.shape
    lb = min(n, LANES)
    tri = jnp.asarray(np.triu(np.ones((lb, lb), np.float32), 1), BF16)
    blk = pl.BlockSpec((1, e, n), lambda bi: (bi, 0, 0))
    out = jax.ShapeDtypeStruct((b, e, n), jnp.int32)
    return pl.pallas_call(
        functools.partial(_select_kernel, cap=cap, lb=lb),
        out_shape=(out, out),
        grid=(b,),
        in_specs=[blk, pl.BlockSpec((lb, lb), lambda bi: (0, 0))],
        out_specs=[blk, blk],
        compiler_params=_cparams(("parallel",)),
        name="expert_select",
    )(aff, tri)


GATHER_CHUNK = 64


def _expert_kernel(starts_ref, pos_ref, aff_ref, h_ref, w1_ref, w3_ref, w2_ref, y_o, xin_s, gate_s,
                   *, cap, tile, ntile, tpg, nchunk):
    e, bi, g = pl.program_id(0), pl.program_id(1), pl.program_id(2)

    @pl.when(g == 0)
    def _():
        xin_s[...] = jnp.zeros(xin_s.shape, F32)
        gate_s[...] = jnp.zeros(gate_s.shape, F32)

    for tt in range(tpg):
        base = (e * pl.num_programs(1) + bi) * (ntile + 1) + g * tpg + tt
        start, end = starts_ref[base], starts_ref[base + 1]
        w0 = (start // SUBLANES) * SUBLANES
        cols = slice(tt * tile, (tt + 1) * tile)
        for c in range(nchunk):
            r0 = pl.multiple_of(w0 + c * GATHER_CHUNK, SUBLANES)

            @pl.when((r0 < end) & (r0 + GATHER_CHUNK > start))
            def _():
                slot = r0 + lax.broadcasted_iota(jnp.int32, (GATHER_CHUNK, tile), 0)
                onehot = pos_ref[0, 0][:, cols] == slot
                rows = pl.ds(r0, GATHER_CHUNK)
                xin_s[rows, :] += jnp.dot(onehot.astype(BF16), h_ref[0, cols, :], preferred_element_type=F32)
                gate = jnp.sum(jnp.where(onehot, aff_ref[0, 0][:, cols], 0.0), axis=1, keepdims=True)
                gate_s[rows, :] += jnp.broadcast_to(gate, (GATHER_CHUNK, gate_s.shape[1]))

    @pl.when(g == pl.num_programs(2) - 1)
    def _():
        rc = min(cap, SLOT_BLOCK)
        capp = y_o.shape[2]
        if capp > cap:
            y_o[0, 0, cap:capp, :] = jnp.zeros((capp - cap, y_o.shape[3]), BF16)
        w1, w3, w2 = (w_ref[0, 0].astype(BF16) for w_ref in (w1_ref, w3_ref, w2_ref))
        for c in range(cap // rc):
            x = xin_s[c * rc:(c + 1) * rc, :].astype(BF16)
            hid = _silu(jnp.dot(x, w1, preferred_element_type=F32)) * jnp.dot(x, w3, preferred_element_type=F32)
            y = jnp.dot(hid.astype(BF16), w2, preferred_element_type=F32)
            y_o[0, 0, c * rc:(c + 1) * rc, :] = (y * gate_s[c * rc:(c + 1) * rc, 0:1]).astype(BF16)


def _moe_tile(n):
    return _pick(n, 256)


def _experts(starts, pos4, aff4, h_bf16, w1, w3, w2, layer, cap):
    b, n, d = h_bf16.shape
    tile = _moe_tile(n)
    ntile = n // tile
    nchunk = tile // GATHER_CHUNK + 1
    rows = cap + nchunk * GATHER_CHUNK
    capp = -(-cap // SLOT_BLOCK) * SLOT_BLOCK
    f = w1.shape[3]
    tg = _pick(n, 2048)
    tpg = tg // tile
    grid_spec = pltpu.PrefetchScalarGridSpec(
        num_scalar_prefetch=1,
        grid=(N_EXPERTS, b, n // tg),
        in_specs=[pl.BlockSpec((1, 1, 1, tg), lambda e, bi, g, s: (bi, e, 0, g)),
                  pl.BlockSpec((1, 1, 1, tg), lambda e, bi, g, s: (bi, e, 0, g)),
                  pl.BlockSpec((1, tg, d), lambda e, bi, g, s: (bi, g, 0)),
                  pl.BlockSpec((1, 1, d, f), lambda e, bi, g, s: (layer, e, 0, 0), pipeline_mode=pl.Buffered(1)),
                  pl.BlockSpec((1, 1, d, f), lambda e, bi, g, s: (layer, e, 0, 0), pipeline_mode=pl.Buffered(1)),
                  pl.BlockSpec((1, 1, f, d), lambda e, bi, g, s: (layer, e, 0, 0), pipeline_mode=pl.Buffered(1))],
        out_specs=pl.BlockSpec((1, 1, capp, d), lambda e, bi, g, s: (e, bi, 0, 0)),
        scratch_shapes=[pltpu.VMEM((rows, d), F32), pltpu.VMEM((rows, LANES), F32)])
    return pl.pallas_call(
        functools.partial(_expert_kernel, cap=cap, tile=tile, ntile=ntile, tpg=tpg, nchunk=nchunk),
        out_shape=jax.ShapeDtypeStruct((N_EXPERTS, b, capp, d), BF16),
        grid_spec=grid_spec,
        compiler_params=_cparams(("arbitrary", "arbitrary", "arbitrary")),
        name="expert_gather_ffn",
    )(starts, pos4, aff4, h_bf16, w1, w3, w2)


def _combine_kernel(starts_ref, pos_ref, *refs, tile, tpg, ntile, sb, nblk):
    y_refs, (x_ref, g2_ref, o_ref) = refs[:nblk], refs[nblk:]
    bi, gi, e = pl.program_id(0), pl.program_id(1), pl.program_id(2)

    @pl.when(e == 0)
    def _():
        o_ref[...] = jnp.zeros(o_ref.shape, F32)

    base = (e * pl.num_programs(0) + bi) * (ntile + 1) + gi * tpg
    k0 = starts_ref[base] // sb
    lane = lax.broadcasted_iota(jnp.int32, (1, LANES), 1)
    for tt in range(tpg):
        start, end = starts_ref[base + tt], starts_ref[base + tt + 1]
        cols = slice(tt * tile, (tt + 1) * tile)
        pos_t = jnp.broadcast_to(pos_ref[0, 0][:, cols].astype(F32), (LANES, tile)).T
        for j, y_ref in enumerate(y_refs):
            s0 = (k0 + j) * sb

            @pl.when((s0 < end) & (s0 + sb > start) & (end > start))
            def _():
                blocks = [pos_t == (s0 + jb * LANES + lane).astype(F32) for jb in range(sb // LANES)]
                onehot_t = jnp.concatenate(blocks, axis=1).astype(BF16)
                o_ref[0, cols, :] += jnp.dot(onehot_t, y_ref[0, 0], preferred_element_type=F32)

    @pl.when(e == pl.num_programs(2) - 1)
    def _():
        o_ref[0] = x_ref[0] + g2_ref[0] * o_ref[0]


def _combine(starts, pos4, y, x, g2):
    b, n, d = x.shape
    tile = _moe_tile(n)
    ntile = n // tile
    gt = _pick(n, 1024)
    tpg = gt // tile
    sb = SLOT_BLOCK
    nsb = y.shape[2] // sb
    nblk = min(gt // sb + 1, nsb)

    def yblk(j):
        def imap(bi, gi, e, s):
            k0 = s[(e * b + bi) * (ntile + 1) + gi * tpg] // sb
            return (e, bi, jnp.minimum(k0 + j, nsb - 1), 0)
        return pl.BlockSpec((1, 1, sb, d), imap)

    grid_spec = pltpu.PrefetchScalarGridSpec(
        num_scalar_prefetch=1,
        grid=(b, n // gt, N_EXPERTS),
        in_specs=[pl.BlockSpec((1, 1, 1, gt), lambda bi, gi, e, s: (bi, e, 0, gi))]
                 + [yblk(j) for j in range(nblk)]
                 + [pl.BlockSpec((1, gt, d), lambda bi, gi, e, s: (bi, gi, 0)),
                    pl.BlockSpec((1, 1, d), lambda bi, gi, e, s: (bi, 0, 0))],
        out_specs=pl.BlockSpec((1, gt, d), lambda bi, gi, e, s: (bi, gi, 0)),
        scratch_shapes=[])
    return pl.pallas_call(
        functools.partial(_combine_kernel, tile=tile, tpg=tpg, ntile=ntile, sb=sb, nblk=nblk),
        out_shape=jax.ShapeDtypeStruct((b, n, d), F32),
        grid_spec=grid_spec,
        compiler_params=_cparams(("parallel", "parallel", "arbitrary")),
        name="expert_combine",
    )(starts, pos4, *([y] * nblk), x, g2)


def _moe(h_f32, h_bf16, x, g2, lw):
    b, n, _ = x.shape
    cap = CAPACITY_FACTOR * n // N_EXPERTS
    aff = _router(h_f32, lw['router_w_t'])
    pos, cnt = _select(aff, cap)
    tile = _moe_tile(n)
    starts = jnp.concatenate([cnt[:, :, ::tile], jnp.full((b, N_EXPERTS, 1), cap, jnp.int32)], axis=2)
    starts = jnp.transpose(starts, (1, 0, 2)).reshape(-1)
    pos4 = pos.reshape(b, N_EXPERTS, 1, n)
    aff4 = aff.reshape(b, N_EXPERTS, 1, n)
    y = _experts(starts, pos4, aff4, h_bf16, lw['exp_w1'], lw['exp_w3'], lw['exp_w2'], lw['layer'], cap)
    return _combine(starts, pos4, y, x, g2)


def _pad_w_in(w_in):
    d = w_in.shape[0]
    sizes = (A_WIDTH,) * 3 + (B_WIDTH,) * 3 + (B_DECAY_RANK, B_ICL_RANK, B_GATE_RANK) + (C_WIDTH,) * 3 + (N_BRANCH * D_MODEL,)
    cuts = np.cumsum((0,) + sizes)
    piece = lambda lo, hi: w_in[:, int(cuts[lo]):int(cuts[hi])]
    zeros = jnp.zeros((d, COL_GATES - COL_LOWRANK - 256), w_in.dtype)
    out = jnp.concatenate([piece(0, 3), piece(3, 6), piece(9, 12), piece(6, 9), zeros, piece(12, 13)], axis=1)
    assert out.shape[1] == N_IN_PAD
    return out.astype(BF16)


def _rows_at(w, lo, total=256):
    pad = [(0, 0)] * w.ndim
    pad[-2] = (lo, total - lo - w.shape[-2])
    return jnp.pad(w, pad)


def _rope_tables(rows):
    half = A_HD // 2
    inv = jnp.power(ROPE_THETA, -jnp.arange(0, half, 2, dtype=F32) / half)
    r = jnp.repeat(jnp.arange(rows, dtype=F32), GRID_W)
    col = jnp.tile(jnp.arange(GRID_W, dtype=F32), rows)
    ang = jnp.concatenate([r[:, None] * inv, col[:, None] * inv], axis=-1)
    cos, sin = jnp.cos(ang), jnp.sin(ang)
    reps = A_WIDTH // A_HD
    cos_t = jnp.tile(jnp.concatenate([cos, cos], axis=-1), (1, reps))
    sin_t = jnp.tile(jnp.concatenate([-sin, sin], axis=-1), (1, reps))
    return cos_t, sin_t


def _layer_weights(i, norm2_g, w_in, q_norm_g, k_norm_g, diff_lambda, diff_subln_g, rwkv_conv_w, rwkv_w0, rwkv_w_up,
                   rwkv_a0, rwkv_a_up, rwkv_g_up, rwkv_k_k, rwkv_k_a, rwkv_r_k, rwkv_ln_g, rwkv_ln_b, conv_w,
                   w_branch, w_out, router_w, exp_w1, exp_w3, exp_w2):
    row = lambda v: v.reshape(1, -1)
    return {
        'norm2_g': row(norm2_g[i]),
        'w_in_pad': _pad_w_in(w_in[i]),
        'q_g': row(jnp.tile(q_norm_g[i], A_WIDTH // A_HD)), 'k_g': row(jnp.tile(k_norm_g[i], A_WIDTH // A_HD)),
        'diff_lambda': diff_lambda[i], 'subln_g': row(diff_subln_g[i]),
        'rwkv_conv_w': rwkv_conv_w[i], 'rwkv_w0': rwkv_w0[i], 'rwkv_a0': rwkv_a0[i],
        'w_up_pad': _rows_at(rwkv_w_up[i], 0), 'a_up_pad': _rows_at(rwkv_a_up[i], B_DECAY_RANK),
        'g_up_pad': _rows_at(rwkv_g_up[i], B_DECAY_RANK + B_ICL_RANK),
        'rwkv_k_k': row(rwkv_k_k[i]), 'rwkv_k_a': row(rwkv_k_a[i]), 'rwkv_r_k': row(rwkv_r_k[i]),
        'rwkv_ln_g': row(rwkv_ln_g[i]), 'rwkv_ln_b': row(rwkv_ln_b[i]),
        'conv_w': conv_w[i], 'w_branch_bf16': w_branch[i].astype(BF16), 'w_out_bf16': w_out[i].astype(BF16),
        'router_w_t': router_w[i].T,
        'exp_w1': exp_w1, 'exp_w3': exp_w3, 'exp_w2': exp_w2, 'layer': i,
        'ones64': _group_ones(B_WIDTH, B_HD).astype(BF16),
    }


def _mixer_inputs(x, norm1_g, sc1, sh1, lw, cos_t, sin_t):
    p = _norm_mod_proj(x, norm1_g, sc1, sh1, lw['w_in_pad'])
    q, k = _qkv_prep(p, lw['q_g'], lw['k_g'], lw['ones64'], cos_t, sin_t)
    names = ('r', 'v', 'kk', 'g', 'bonus', 'dec0', 'b0', 'kd0', 'dec1', 'b1', 'kd1')
    prep = dict(zip(names, _rwkv_prep(p, lw)))
    return p, (q, k), prep


def kernel(x, c, ctx, c_ctx, norm1_g, norm2_g, ada_w, ada_b, w_in, q_norm_g, k_norm_g, diff_lambda, diff_subln_g,
           rwkv_conv_w, rwkv_w0, rwkv_w_up, rwkv_a0, rwkv_a_up, rwkv_g_up, rwkv_k_k, rwkv_k_a, rwkv_r_k, rwkv_ln_g,
           rwkv_ln_b, conv_w, w_branch, w_out, router_w, exp_w1, exp_w3, exp_w2):
    bsz, n_lat, d = x.shape
    depth = ada_w.shape[0]
    cos_t, sin_t = _rope_tables(n_lat // GRID_W)

    cond = jnp.concatenate([c, jnp.broadcast_to(c_ctx[None, :], (SUBLANES - bsz % SUBLANES, d))], axis=0)
    ada = _ada_params(cond, ada_w, ada_b)

    xc, xl = ctx, x
    for i in range(depth):
        lam_init = 0.8 - 0.6 * math.exp(-0.3 * i)
        need_ctx = i != depth - 1
        lw = _layer_weights(i, norm2_g, w_in, q_norm_g, k_norm_g, diff_lambda, diff_subln_g, rwkv_conv_w, rwkv_w0,
                            rwkv_w_up, rwkv_a0, rwkv_a_up, rwkv_g_up, rwkv_k_k, rwkv_k_a, rwkv_r_k, rwkv_ln_g,
                            rwkv_ln_b, conv_w, w_branch, w_out, router_w, exp_w1, exp_w3, exp_w2)
        mods = ada[i].reshape(-1, 6, d)
        lat = [mods[:bsz, j][:, None, :] for j in range(6)]
        cxt = [jnp.broadcast_to(mods[bsz, j][None, None, :], (bsz, 1, d)) for j in range(6)]

        pc, (qc, kc), prep_c = _mixer_inputs(xc, norm1_g[i], cxt[1], cxt[0], lw, None, None)
        pl_, (ql, kl), prep_l = _mixer_inputs(xl, norm1_g[i], lat[1], lat[0], lw, cos_t, sin_t)

        v_cols = slice(COL_ATTN + 2 * A_WIDTH, COL_ATTN + 3 * A_WIDTH)
        k_all = jnp.concatenate([kc, kl], axis=1)
        v_all = jnp.concatenate([pc[:, :, v_cols], pl_[:, :, v_cols]], axis=1)
        al = _flash_attention(lw['diff_lambda'], lw['subln_g'], ql, k_all, v_all, lam_init)
        o0, o1 = _rwkv_bidirectional(prep_c, prep_l)
        n_ctx = xc.shape[1]

        xl, hl, hlb = _merge(al, o0, o1, n_ctx, prep_l['bonus'], prep_l['g'], pl_, xl, lw, lat[2], lat[4], lat[3])
        xl = _moe(hl, hlb, xl, lat[5], lw)
        if need_ctx:
            ac = _ctx_attention(lw['diff_lambda'], lw['subln_g'], qc, kc, pc, lam_init)
            xc, hc, hcb = _merge(ac, o0, o1, 0, prep_c['bonus'], prep_c['g'], pc, xc, lw, cxt[2], cxt[4], cxt[3])
            xc = _moe(hc, hcb, xc, cxt[5], lw)
    return xl
```

```python
import functools
import math

import numpy as np
import jax
import jax.numpy as jnp
from jax import lax
from jax.experimental import pallas as pl
from jax.experimental.pallas import tpu as pltpu

F32 = jnp.float32
BF16 = jnp.bfloat16
HIGHEST = lax.Precision.HIGHEST

D_MODEL = 1024
GRID_W = 64
A_HEADS = 4
A_HD = 64
A_WIDTH = A_HEADS * 2 * A_HD
B_HEADS = 8
B_HD = 64
B_WIDTH = B_HEADS * B_HD
B_DECAY_RANK = 64
B_ICL_RANK = 64
B_GATE_RANK = 128
C_WIDTH = 512
N_BRANCH = 3
N_EXPERTS = 16
EXPERT_FF = 1024
CAPACITY_FACTOR = 2
ROPE_THETA = 10000.0
NORM_EPS = 1e-6
GN_EPS = 64e-5
LOG2_E = math.log2(math.e)

COL_ATTN = 0
COL_RWKV = 1536
COL_CONV = 3072
COL_LOWRANK = 4608
COL_GATES = 5120
N_IN_PAD = 8192

VMEM_LIMIT = 56 * 1024 * 1024
LANES = 128
SUBLANES = 8
SLOT_BLOCK = 256


def _cparams(sem):
    return pltpu.CompilerParams(dimension_semantics=sem, vmem_limit_bytes=VMEM_LIMIT)


def _pick(n, pref):
    t = min(n, pref)
    while n % t:
        t -= SUBLANES
    return t


def _group_ones(width, group):
    idx = np.arange(width) // group
    return jnp.asarray((idx[:, None] == idx[None, :]).astype(np.float32))


def _silu(x):
    return x * jax.nn.sigmoid(x)


def _split2(x):
    hi = x.astype(BF16)
    return hi, (x - hi.astype(F32)).astype(BF16)


def _group_sum(x, ones_bf16):
    hi, lo = _split2(x)
    return (jnp.dot(hi, ones_bf16, preferred_element_type=F32) + jnp.dot(lo, ones_bf16, preferred_element_type=F32))


def _permute_rows(perm_bf16, x):
    hi, lo = _split2(x)
    rest = x - hi.astype(F32) - lo.astype(F32)
    dot = lambda v: jnp.dot(perm_bf16, v, preferred_element_type=F32)
    return dot(hi) + dot(lo) + dot(rest.astype(BF16))


def _dot3(a, b):
    a_hi, a_lo = _split2(a)
    b_hi, b_lo = _split2(b)
    dot = lambda u, v: jnp.dot(u, v, preferred_element_type=F32)
    return dot(a_hi, b_hi) + (dot(a_hi, b_lo) + dot(a_lo, b_hi))


def _ada_kernel(cond_ref, w_ref, b_ref, o_ref):
    c = cond_ref[...]
    o_ref[0] = jnp.dot(_silu(c), w_ref[0], precision=HIGHEST, preferred_element_type=F32) + b_ref[0]


def _ada_params(cond, ada_w, ada_b):
    depth, d, n6 = ada_w.shape
    tn = _pick(n6, 1536)
    return pl.pallas_call(
        _ada_kernel,
        out_shape=jax.ShapeDtypeStruct((depth, cond.shape[0], n6), F32),
        grid=(depth, n6 // tn),
        in_specs=[pl.BlockSpec(cond.shape, lambda i, j: (0, 0)),
                  pl.BlockSpec((1, d, tn), lambda i, j: (i, 0, j)),
                  pl.BlockSpec((1, 1, tn), lambda i, j: (i, 0, j))],
        out_specs=pl.BlockSpec((1, cond.shape[0], tn), lambda i, j: (i, 0, j)),
        compiler_params=_cparams(("parallel", "parallel")),
        name="ada_params",
    )(cond, ada_w, ada_b.reshape(depth, 1, n6))


def _proj_kernel(x_ref, g_ref, sc_ref, sh_ref, w_ref, o_ref):
    x = x_ref[0]
    ms = jnp.mean(x * x, axis=-1, keepdims=True)
    y = x * lax.rsqrt(ms + NORM_EPS) * g_ref[...]
    h = y * (1.0 + sc_ref[0]) + sh_ref[0]
    o_ref[0] = jnp.dot(h.astype(BF16), w_ref[...], preferred_element_type=F32).astype(BF16)


def _norm_mod_proj(x, g, sc, sh, w_bf16):
    b, n, d = x.shape
    npad = w_bf16.shape[1]
    tm = _pick(n, 512)
    tn = 2048
    return pl.pallas_call(
        _proj_kernel,
        out_shape=jax.ShapeDtypeStruct((b, n, npad), BF16),
        grid=(npad // tn, b, n // tm),
        in_specs=[pl.BlockSpec((1, tm, d), lambda j, bi, i: (bi, i, 0)),
                  pl.BlockSpec((1, d), lambda j, bi, i: (0, 0)),
                  pl.BlockSpec((1, 1, d), lambda j, bi, i: (bi, 0, 0)),
                  pl.BlockSpec((1, 1, d), lambda j, bi, i: (bi, 0, 0)),
                  pl.BlockSpec((d, tn), lambda j, bi, i: (0, j))],
        out_specs=pl.BlockSpec((1, tm, tn), lambda j, bi, i: (bi, i, j)),
        compiler_params=_cparams(("parallel", "parallel", "parallel")),
        name="norm_mod_proj",
    )(x, g.reshape(1, d), sc, sh, w_bf16)


def _swap_halves(x, half):
    width = x.shape[-1]
    lane = lax.broadcasted_iota(jnp.int32, (1, width), 1) % (2 * half)
    fwd = pltpu.roll(x, width - half, axis=1)
    bwd = pltpu.roll(x, half, axis=1)
    return jnp.where(lane < half, fwd, bwd)


def _qkv_prep_kernel(*refs, use_rope):
    if use_rope:
        q_ref, k_ref, qg_ref, kg_ref, ones_ref, cos_ref, sin_ref, qo_ref, ko_ref = refs
    else:
        q_ref, k_ref, qg_ref, kg_ref, ones_ref, qo_ref, ko_ref = refs

    def head_norm(x, gain):
        ms = _group_sum(x * x, ones_ref[...]) * (1.0 / A_HD)
        return x * lax.rsqrt(ms + NORM_EPS) * gain

    def rope(x):
        if not use_rope:
            return x
        return x * cos_ref[...] + _swap_halves(x, A_HD // 2) * sin_ref[...]

    q = rope(head_norm(q_ref[0].astype(F32), qg_ref[...]))
    k = rope(head_norm(k_ref[0].astype(F32), kg_ref[...]))
    qo_ref[0] = (q * (A_HD ** -0.5 * LOG2_E)).astype(BF16)
    ko_ref[0] = k.astype(BF16)


def _qkv_prep(p, qg, kg, ones, cos_t, sin_t):
    b, n, _ = p.shape
    tm = _pick(n, 512)
    use_rope = cos_t is not None
    w = A_WIDTH
    col = lambda c: pl.BlockSpec((1, tm, w), lambda bi, i, c=c: (bi, i, COL_ATTN // w + c))
    const = lambda shape: pl.BlockSpec(shape, lambda bi, i: (0,) * len(shape))
    in_specs = [col(0), col(1), const((1, w)), const((1, w)), const((w, w))]
    args = [p, p, qg, kg, ones]
    if use_rope:
        in_specs += [pl.BlockSpec((tm, w), lambda bi, i: (i, 0))] * 2
        args += [cos_t, sin_t]
    out = jax.ShapeDtypeStruct((b, n, w), BF16)
    return pl.pallas_call(
        functools.partial(_qkv_prep_kernel, use_rope=use_rope),
        out_shape=(out, out),
        grid=(b, n // tm),
        in_specs=in_specs,
        out_specs=[pl.BlockSpec((1, tm, w), lambda bi, i: (bi, i, 0))] * 2,
        compiler_params=_cparams(("parallel", "parallel")),
        name="qkv_prep_rope" if use_rope else "qkv_prep",
    )(*args)


def _map_stacked(q):
    lane = lax.broadcasted_iota(jnp.int32, (1, 2 * A_HD), 1)
    zero = jnp.zeros_like(q)
    return jnp.concatenate([jnp.where(lane < A_HD, q, zero), jnp.where(lane >= A_HD, q, zero)], axis=0)


def _lane_tiled(x, width):
    return jnp.concatenate([x] * (width // LANES), axis=1)


def _lane_block_sum(p):
    out = p[:, 0:LANES]
    for j in range(1, p.shape[1] // LANES):
        out = out + p[:, j * LANES:(j + 1) * LANES]
    return out


def _lambda_full(dl_ref, lam_init):
    dl = dl_ref[...]
    s1 = jnp.sum(dl[0:1] * dl[1:2], axis=-1, keepdims=True)
    s2 = jnp.sum(dl[2:3] * dl[3:4], axis=-1, keepdims=True)
    return jnp.exp(s1) - jnp.exp(s2) + lam_init


def _subln(o, sg_ref, lam_init):
    ms = jnp.mean(o * o, axis=-1, keepdims=True)
    return o * lax.rsqrt(ms + NORM_EPS) * sg_ref[...] * (1.0 - lam_init)


def _scores(qm, k):
    return lax.dot_general(qm, k, (((1,), (1,)), ((), ())), preferred_element_type=F32)


def _flash_kernel(dl_ref, sg_ref, q_ref, k_ref, v_ref, o_ref, m_s, l_s, acc_s, *, lam_init, nkv):
    kj = pl.program_id(3)
    tq = q_ref.shape[1]

    @pl.when(kj == 0)
    def _():
        m_s[...] = jnp.full(m_s.shape, -jnp.inf, F32)
        l_s[...] = jnp.zeros(l_s.shape, F32)
        acc_s[...] = jnp.zeros(acc_s.shape, F32)

    s = _scores(_map_stacked(q_ref[0]), k_ref[0])
    m_prev = m_s[...]
    m_new = jnp.maximum(m_prev, jnp.max(s, axis=-1, keepdims=True))
    alpha = jnp.exp2(m_prev - m_new)
    p = jnp.exp2(s - _lane_tiled(m_new, s.shape[1]))
    l_s[...] = alpha * l_s[...] + _lane_block_sum(p)
    acc_s[...] = alpha * acc_s[...] + jnp.dot(p.astype(BF16), v_ref[0], preferred_element_type=F32)
    m_s[...] = m_new

    @pl.when(kj == nkv - 1)
    def _():
        lam = _lambda_full(dl_ref, lam_init)
        o = acc_s[...] / jnp.sum(l_s[...], axis=-1, keepdims=True)
        o_ref[0] = _subln(o[0:tq] - lam * o[tq:2 * tq], sg_ref, lam_init).astype(o_ref.dtype)


V_COL_BLOCK = (COL_ATTN + 2 * A_WIDTH) // (2 * A_HD)


def _pick_lanes(n, pref):
    t = min(n, pref) // LANES * LANES
    while n % t:
        t -= LANES
    return t


def _flash_attention(dl, sg, q, k_all, v_all, lam_init):
    b, s, _ = q.shape
    t = k_all.shape[1]
    hw = 2 * A_HD
    tq = _pick(s, 1024)
    tk = _pick_lanes(t, 3072)
    nkv = t // tk
    return pl.pallas_call(
        functools.partial(_flash_kernel, lam_init=lam_init, nkv=nkv),
        out_shape=jax.ShapeDtypeStruct((b, s, A_WIDTH), BF16),
        grid=(b, A_HEADS, s // tq, nkv),
        in_specs=[pl.BlockSpec((4, A_HD), lambda bi, h, i, j: (0, 0)),
                  pl.BlockSpec((1, hw), lambda bi, h, i, j: (0, 0)),
                  pl.BlockSpec((1, tq, hw), lambda bi, h, i, j: (bi, i, h)),
                  pl.BlockSpec((1, tk, hw), lambda bi, h, i, j: (bi, j, h)),
                  pl.BlockSpec((1, tk, hw), lambda bi, h, i, j: (bi, j, h))],
        out_specs=pl.BlockSpec((1, tq, hw), lambda bi, h, i, j: (bi, i, h)),
        scratch_shapes=[pltpu.VMEM((2 * tq, LANES), F32), pltpu.VMEM((2 * tq, LANES), F32), pltpu.VMEM((2 * tq, hw), F32)],
        compiler_params=_cparams(("parallel", "parallel", "parallel", "arbitrary")),
        name="diff_flash_attention",
    )(dl, sg, q, k_all, v_all)


def _ctx_attn_kernel(dl_ref, sg_ref, q_ref, k_ref, v_ref, o_ref, *, lam_init):
    n = q_ref.shape[1]
    s = _scores(_map_stacked(q_ref[0]), k_ref[0])
    p = jnp.exp2(s - jnp.max(s, axis=-1, keepdims=True))
    o = jnp.dot(p.astype(BF16), v_ref[0], preferred_element_type=F32) / jnp.sum(p, axis=-1, keepdims=True)
    lam = _lambda_full(dl_ref, lam_init)
    o_ref[0] = _subln(o[0:n] - lam * o[n:2 * n], sg_ref, lam_init).astype(o_ref.dtype)


def _ctx_attention(dl, sg, q, k, v, lam_init):
    b, n, _ = q.shape
    hw = 2 * A_HD
    blk = pl.BlockSpec((1, n, hw), lambda bi, h: (bi, 0, h))
    vblk = pl.BlockSpec((1, n, hw), lambda bi, h: (bi, 0, V_COL_BLOCK + h))
    return pl.pallas_call(
        functools.partial(_ctx_attn_kernel, lam_init=lam_init),
        out_shape=jax.ShapeDtypeStruct((b, n, A_WIDTH), BF16),
        grid=(b, A_HEADS),
        in_specs=[pl.BlockSpec((4, A_HD), lambda bi, h: (0, 0)), pl.BlockSpec((1, hw), lambda bi, h: (0, 0)),
                  blk, blk, vblk],
        out_specs=blk,
        compiler_params=_cparams(("parallel", "parallel")),
        name="ctx_attention",
    )(dl, sg, q, k, v)


HALO_ROWS = 16


def _halo_specs(tm, width, colblk, n):
    rh = tm // HALO_ROWS
    last = n // HALO_ROWS - 1
    main = pl.BlockSpec((1, tm, width), lambda bi, i: (bi, i, colblk))
    prev = pl.BlockSpec((1, HALO_ROWS, width), lambda bi, i: (bi, jnp.maximum(i * rh - 1, 0), colblk))
    nxt = pl.BlockSpec((1, HALO_ROWS, width), lambda bi, i: (bi, jnp.minimum((i + 1) * rh, last), colblk))
    return [main, prev, nxt]


def _dwconv3(x, prev_row, next_row, w_ref):
    tm = x.shape[0]
    row = lax.broadcasted_iota(jnp.int32, (tm, 1), 0)
    xp = jnp.where(row == 0, prev_row, pltpu.roll(x, 1, axis=0))
    xn = jnp.where(row == tm - 1, next_row, pltpu.roll(x, tm - 1, axis=0))
    w = w_ref[...]
    return w[0:1] * xp + w[1:2] * x + w[2:3] * xn


def _edge_rows(prev_ref, next_ref):
    i = pl.program_id(1)
    n = pl.num_programs(1)
    pr = prev_ref[0].astype(F32)[HALO_ROWS - 1:HALO_ROWS]
    nr = next_ref[0].astype(F32)[0:1]
    pr = jnp.where(i == 0, jnp.zeros_like(pr), pr)
    nr = jnp.where(i == n - 1, jnp.zeros_like(nr), nr)
    return pr, nr


def _softplus(u):
    return jnp.maximum(u, 0.0) + jnp.log(1.0 + jnp.exp(-jnp.abs(u)))


SCAN_BLOCK = 64


def _block_triangles(tm):
    i = np.arange(tm)
    same = (i[:, None] // SCAN_BLOCK) == (i[None, :] // SCAN_BLOCK)
    return jnp.asarray(np.stack([same & (i[None, :] <= i[:, None]), same & (i[None, :] >= i[:, None])]), BF16)


def _rwkv_prep_kernel(rkv_ref, prev_ref, next_ref, lr_ref, cw_ref, w0_ref, wup_ref, a0_ref, aup_ref, gup_ref,
                      kk_ref_w, ka_ref, rk_ref, ones_ref, tri_ref,
                      v_o, kk_o, g_o, bonus_o, gam0_o, b0_o, kd0_o, r0_o, gam1_o, b1_o, kd1_o, r1_o):
    pr, nr = _edge_rows(prev_ref, next_ref)
    rkv = _dwconv3(rkv_ref[0].astype(F32), pr, nr, cw_ref)
    w = B_WIDTH
    r, k, v = rkv[:, 0:w], rkv[:, w:2 * w], rkv[:, 2 * w:3 * w]
    gsum = lambda t: _group_sum(t, ones_ref[...])
    kkr = k * kk_ref_w[...]
    nrm = jnp.maximum(jnp.sqrt(gsum(kkr * kkr)), 1e-12)
    kk = kkr / nrm
    lr = lr_ref[0].astype(F32)
    th = jnp.tanh(lr)
    sg = jax.nn.sigmoid(lr)
    v_o[0] = v
    kk_o[0] = kk
    g_o[0] = _dot3(sg, gup_ref[...])
    kd_sum = jnp.zeros_like(k)
    outs = ((gam0_o, b0_o, kd0_o, r0_o), (gam1_o, b1_o, kd1_o, r1_o))
    for d in range(2):
        z = w0_ref[d:d + 1] + _dot3(th, wup_ref[d])
        wlog = -_softplus(-z) - 0.5
        log_decay = -jnp.exp(wlog)
        log_gamma = _permute_rows(tri_ref[d], log_decay)
        gamma, inv_gamma = jnp.exp(log_gamma), jnp.exp(-log_gamma)
        a = jax.nn.sigmoid(a0_ref[d:d + 1] + _dot3(lr, aup_ref[d]))
        kd = k * (1.0 + (a - 1.0) * ka_ref[...])
        outs[d][0][0] = gamma
        outs[d][1][0] = kk * a * inv_gamma
        outs[d][2][0] = kd * inv_gamma
        outs[d][3][0] = r * gamma
        kd_sum = kd_sum + kd
    bonus_o[0] = gsum(r * kd_sum * rk_ref[...]) * v


def _rwkv_prep(p, lw):
    b, n, _ = p.shape
    tm = _pick(n, 256)
    w = B_WIDTH
    w3 = 3 * w
    const = lambda shape: pl.BlockSpec(shape, lambda bi, i: (0,) * len(shape))
    in_specs = _halo_specs(tm, w3, COL_RWKV // w3, n) + [
        pl.BlockSpec((1, tm, 256), lambda bi, i: (bi, i, COL_LOWRANK // 256)),
        const((3, w3)), const((2, w)), const((2, 256, w)), const((2, w)), const((2, 256, w)), const((256, w)),
        const((1, w)), const((1, w)), const((1, w)), const((w, w)), const((2, tm, tm))]
    assert tm % SCAN_BLOCK == 0
    out = jax.ShapeDtypeStruct((b, n, w), F32)
    return pl.pallas_call(
        _rwkv_prep_kernel,
        out_shape=(out,) * len(PREP_NAMES),
        grid=(b, n // tm),
        in_specs=in_specs,
        out_specs=[pl.BlockSpec((1, tm, w), lambda bi, i: (bi, i, 0))] * len(PREP_NAMES),
        compiler_params=_cparams(("parallel", "parallel")),
        name="rwkv_prep",
    )(p, p, p, p, lw['rwkv_conv_w'], lw['rwkv_w0'], lw['w_up_pad'], lw['rwkv_a0'], lw['a_up_pad'], lw['g_up_pad'],
      lw['rwkv_k_k'], lw['rwkv_k_a'], lw['rwkv_r_k'], lw['ones64'], _block_triangles(tm))


PREP_NAMES = ('v', 'kk', 'g', 'bonus', 'gam0', 'b0', 'kd0', 'r0', 'gam1', 'b1', 'kd1', 'r1')


V_SLABS = B_HD // 2 // SUBLANES


def _scan_kernel(gam_ref, b_ref, kd_ref, r_ref, kk_ref, kkx_ref, v_ref, y_ref, s_ref, sa_ref, *, tc):
    @pl.when(pl.program_id(0) == 0)
    def _():
        s_ref[...] = jnp.zeros(s_ref.shape, F32)
        sa_ref[...] = jnp.zeros(sa_ref.shape, F32)

    lanes = s_ref.shape[-1]
    vrows = B_HD // 2
    bcast = lambda row: jnp.broadcast_to(row, (SUBLANES, lanes))

    def step(t, last_of_block):
        base = pl.multiple_of(t * vrows, vrows)
        sa = [sa_ref[s] for s in range(V_SLABS)]
        vv = [v_ref[pl.ds(base + SUBLANES * s, SUBLANES), :] for s in range(V_SLABS)]
        y = [jnp.zeros((SUBLANES, lanes), F32) for _ in range(V_SLABS)]
        san = [jnp.zeros((SUBLANES, lanes), F32) for _ in range(V_SLABS)]
        for k in range(B_HD):
            row = lambda ref: bcast(ref[k, pl.ds(t, 1), :])
            gk, bk, kdk, rk = row(gam_ref), row(b_ref), row(kd_ref), row(r_ref)
            kk_next = bcast(kkx_ref[k, 0:1, :]) if last_of_block else bcast(kk_ref[k, pl.ds(t + 1, 1), :])
            kkn = kk_next * gk
            for s in range(V_SLABS):
                new = s_ref[s, k] - sa[s] * bk + vv[s] * kdk
                s_ref[s, k] = new * gk if last_of_block else new
                y[s] = y[s] + new * rk
                san[s] = san[s] + new * kkn
        for s in range(V_SLABS):
            y_ref[pl.ds(base + SUBLANES * s, SUBLANES), :] = y[s]
            sa_ref[s] = san[s]

    def body(t, carry):
        step(t, False)
        return carry

    lax.fori_loop(0, tc - 1, body, 0)
    step(tc - 1, True)


def _rwkv_scan(w, bb, kd, r, kk, v):
    _, t, lanes = w.shape
    tc = SCAN_BLOCK
    assert t % tc == 0
    vrows = B_HD // 2
    last8 = t // SUBLANES - 1
    kspec = pl.BlockSpec((B_HD, tc, lanes), lambda i: (0, i, 0))
    kxspec = pl.BlockSpec((B_HD, SUBLANES, lanes), lambda i: (0, jnp.minimum((i + 1) * (tc // SUBLANES), last8), 0))
    vspec = pl.BlockSpec((tc * vrows, lanes), lambda i: (i, 0))
    return pl.pallas_call(
        functools.partial(_scan_kernel, tc=tc),
        out_shape=jax.ShapeDtypeStruct(v.shape, F32),
        grid=(t // tc,),
        in_specs=[kspec] * 5 + [kxspec, vspec],
        out_specs=vspec,
        scratch_shapes=[pltpu.VMEM((V_SLABS, B_HD, SUBLANES, lanes), F32), pltpu.VMEM((V_SLABS, SUBLANES, lanes), F32)],
        compiler_params=_cparams(("arbitrary",)),
        name="rwkv_scan",
    )(w, bb, kd, r, kk, kk, v)


TIME_BLOCK = 128
HEAD_PAIRS = B_WIDTH // LANES


def _scan_block_maps(nctx, nlat):
    fwd_c = lambda i: jnp.minimum(i, nctx - 1)
    fwd_l = lambda i: jnp.clip(i - nctx, 0, nlat - 1)
    bwd_c = lambda i: jnp.clip(nctx - 1 - i, 0, nctx - 1)
    bwd_l = lambda i: jnp.clip(nlat - 1 - (i - nctx), 0, nlat - 1)
    return fwd_c, fwd_l, bwd_c, bwd_l


def _fill_transposed(u_ref, src_ref, direction, rev_ref, bsz):
    for b in range(bsz):
        for j in range(HEAD_PAIRS):
            tile = src_ref[b, :, j * LANES:(j + 1) * LANES]
            if direction == 1:
                tile = _permute_rows(rev_ref[...], tile)
            q = (direction * bsz + b) * HEAD_PAIRS + j
            u_ref[q * LANES:(q + 1) * LANES, :] = tile.T


def _fill_both(u_ref, cf_ref, lf_ref, cb_ref, lb_ref, rev_ref, nctx, bsz):
    i = pl.program_id(0)

    @pl.when(i < nctx)
    def _():
        _fill_transposed(u_ref, cf_ref, 0, rev_ref, bsz)
        _fill_transposed(u_ref, cb_ref, 1, rev_ref, bsz)

    @pl.when(i >= nctx)
    def _():
        _fill_transposed(u_ref, lf_ref, 0, rev_ref, bsz)
        _fill_transposed(u_ref, lb_ref, 1, rev_ref, bsz)


def _to_scan_k_kernel(cf_ref, lf_ref, cb_ref, lb_ref, rev_ref, o_ref, u_ref, *, nctx, bsz):
    _fill_both(u_ref, cf_ref, lf_ref, cb_ref, lb_ref, rev_ref, nctx, bsz)
    nrow = u_ref.shape[0] // B_HD
    for k in range(B_HD):
        g = u_ref[pl.ds(k, nrow, stride=B_HD), :]
        o_ref[k] = jnp.concatenate([g, g], axis=0).T


def _to_scan_v_kernel(cf_ref, lf_ref, cb_ref, lb_ref, rev_ref, o_ref, u_ref, *, nctx, bsz):
    _fill_both(u_ref, cf_ref, lf_ref, cb_ref, lb_ref, rev_ref, nctx, bsz)
    nrow = u_ref.shape[0] // B_HD
    vrows = B_HD // 2
    for vi in range(vrows):
        halves = [u_ref[pl.ds(half * vrows + vi, nrow, stride=B_HD), :] for half in range(2)]
        o_ref[pl.ds(vi, TIME_BLOCK, stride=vrows), :] = jnp.concatenate(halves, axis=0).T


def _to_scan(ctx_f, lat_f, ctx_b, lat_b, rev, values):
    bsz, nc, width = ctx_f.shape
    nl = lat_f.shape[1]
    nctx, nlat = nc // TIME_BLOCK, nl // TIME_BLOCK
    lanes = 4 * bsz * B_HEADS
    t = nc + nl
    maps = _scan_block_maps(nctx, nlat)
    src = lambda f: pl.BlockSpec((bsz, TIME_BLOCK, width), lambda i, f=f: (0, f(i), 0))
    if values:
        kern, out_shape = _to_scan_v_kernel, (t * (B_HD // 2), lanes)
        out_spec = pl.BlockSpec((TIME_BLOCK * (B_HD // 2), lanes), lambda i: (i, 0))
    else:
        kern, out_shape = _to_scan_k_kernel, (B_HD, t, lanes)
        out_spec = pl.BlockSpec((B_HD, TIME_BLOCK, lanes), lambda i: (0, i, 0))
    return pl.pallas_call(
        functools.partial(kern, nctx=nctx, bsz=bsz),
        out_shape=jax.ShapeDtypeStruct(out_shape, F32),
        grid=(nctx + nlat,),
        in_specs=[src(maps[0]), src(maps[1]), src(maps[2]), src(maps[3]),
                  pl.BlockSpec((TIME_BLOCK, TIME_BLOCK), lambda i: (0, 0))],
        out_specs=out_spec,
        scratch_shapes=[pltpu.VMEM((2 * bsz * HEAD_PAIRS * LANES, TIME_BLOCK), F32)],
        compiler_params=_cparams(("parallel",)),
        name="to_scan_v" if values else "to_scan_k",
    )(ctx_f, lat_f, ctx_b, lat_b, rev)


def _from_scan_kernel(y_ref, rev_ref, o0_ref, o1_ref, u_ref, *, bsz):
    nrow = u_ref.shape[0] // B_HD
    vrows = B_HD // 2
    for vi in range(vrows):
        mt = y_ref[pl.ds(vi, TIME_BLOCK, stride=vrows), :].T
        for half in range(2):
            u_ref[pl.ds(half * vrows + vi, nrow, stride=B_HD), :] = mt[half * nrow:(half + 1) * nrow]
    for direction, o_ref in enumerate((o0_ref, o1_ref)):
        for b in range(bsz):
            for j in range(HEAD_PAIRS):
                q = (direction * bsz + b) * HEAD_PAIRS + j
                tile = u_ref[q * LANES:(q + 1) * LANES, :].T
                if direction == 1:
                    tile = _permute_rows(rev_ref[...], tile)
                o_ref[b, :, j * LANES:(j + 1) * LANES] = tile


def _from_scan(y, rev, bsz, nc, nl):
    nctx, nlat = nc // TIME_BLOCK, nl // TIME_BLOCK
    lanes = y.shape[1]
    vrows = B_HD // 2
    bwd = lambda i: jnp.where(i < nctx, nctx - 1 - i, 2 * nctx + nlat - 1 - i)
    out = jax.ShapeDtypeStruct((bsz, nc + nl, B_WIDTH), F32)
    return pl.pallas_call(
        functools.partial(_from_scan_kernel, bsz=bsz),
        out_shape=(out, out),
        grid=(nctx + nlat,),
        in_specs=[pl.BlockSpec((TIME_BLOCK * vrows, lanes), lambda i: (i, 0)),
                  pl.BlockSpec((TIME_BLOCK, TIME_BLOCK), lambda i: (0, 0))],
        out_specs=[pl.BlockSpec((bsz, TIME_BLOCK, B_WIDTH), lambda i: (0, i, 0)),
                   pl.BlockSpec((bsz, TIME_BLOCK, B_WIDTH), lambda i: (0, bwd(i), 0))],
        scratch_shapes=[pltpu.VMEM((2 * bsz * HEAD_PAIRS * LANES, TIME_BLOCK), F32)],
        compiler_params=_cparams(("parallel",)),
        name="from_scan",
    )(y, rev)


def _rwkv_bidirectional(prep_c, prep_l):
    bsz, nc, _ = prep_c['v'].shape
    nl = prep_l['v'].shape[1]
    rev = jnp.asarray(np.eye(TIME_BLOCK, dtype=np.float32)[::-1].copy(), BF16)
    pair = lambda fwd, bwd, values=False: _to_scan(prep_c[fwd], prep_l[fwd], prep_c[bwd], prep_l[bwd], rev, values)
    y = _rwkv_scan(pair('gam0', 'gam1'), pair('b0', 'b1'), pair('kd0', 'kd1'), pair('r0', 'r1'), pair('kk', 'kk'),
                   pair('v', 'v', True))
    return _from_scan(y, rev, bsz, nc, nl)


def _merge_kernel(a_ref, o0_ref, o1_ref, bonus_ref, g_ref, cv_ref, cprev_ref, cnext_ref, ga_ref, gb_ref, gc_ref,
                  x_ref, lng_ref, lnb_ref, cw_ref, wb_ref, wo_ref, ones_ref, g1_ref, n2_ref, sc_ref, sh_ref,
                  x_o, hf_o, hb_o):
    gmean = lambda t: _group_sum(t, ones_ref[...]) * (1.0 / B_HD)

    def group_norm(o):
        dlt = o - gmean(o)
        return dlt * lax.rsqrt(gmean(dlt * dlt) + GN_EPS) * lng_ref[...] + lnb_ref[...]

    yb = (group_norm(o0_ref[0]) + group_norm(o1_ref[0]) + bonus_ref[0]) * g_ref[0]

    w = C_WIDTH
    u = lambda blk: blk[:, w:2 * w] * blk[:, 2 * w:3 * w]
    cv = cv_ref[0].astype(F32)
    i = pl.program_id(1)
    pr = u(cprev_ref[0].astype(F32)[HALO_ROWS - 1:HALO_ROWS])
    nr = u(cnext_ref[0].astype(F32)[0:1])
    pr = jnp.where(i == 0, jnp.zeros_like(pr), pr)
    nr = jnp.where(i == pl.num_programs(1) - 1, jnp.zeros_like(nr), nr)
    yc = cv[:, 0:w] * _dwconv3(u(cv), pr, nr, cw_ref)

    d = D_MODEL
    m = jnp.zeros((cv.shape[0], d), F32)
    for j, (yj, gate_ref) in enumerate(((a_ref[0], ga_ref), (yb, gb_ref), (yc, gc_ref))):
        gate = jax.nn.sigmoid(gate_ref[0].astype(F32))
        m = m + gate * jnp.dot(yj.astype(BF16), wb_ref[j], preferred_element_type=F32)
    ymix = jnp.dot(m.astype(BF16), wo_ref[...], preferred_element_type=F32)
    x1 = x_ref[0] + g1_ref[0] * ymix
    x_o[0] = x1
    ms = jnp.mean(x1 * x1, axis=-1, keepdims=True)
    h = x1 * lax.rsqrt(ms + NORM_EPS) * n2_ref[...] * (1.0 + sc_ref[0]) + sh_ref[0]
    hf_o[0] = h
    hb_o[0] = h.astype(BF16)


def _merge(a, o0, o1, row_off, bonus, g, p, x, lw, g1, sc2, sh2):
    b, n, d = x.shape
    tm = _pick(n, 256)
    w = C_WIDTH
    assert row_off % tm == 0
    tile = lambda width: pl.BlockSpec((1, tm, width), lambda bi, i: (bi, i, 0))
    scan_tile = pl.BlockSpec((1, tm, w), lambda bi, i: (bi, i + row_off // tm, 0))
    const = lambda shape: pl.BlockSpec(shape, lambda bi, i: (0,) * len(shape))
    per_b = pl.BlockSpec((1, 1, d), lambda bi, i: (bi, 0, 0))
    gate = lambda j: pl.BlockSpec((1, tm, d), lambda bi, i, j=j: (bi, i, COL_GATES // d + j))
    in_specs = ([tile(w), scan_tile, scan_tile, tile(w), tile(w)]
                + _halo_specs(tm, 3 * w, COL_CONV // (3 * w), n) + [gate(0), gate(1), gate(2)]
                + [tile(d), const((1, w)), const((1, w)), const((3, w)), const((N_BRANCH, w, d)), const((d, d)),
                   const((w, w)), per_b, const((1, d)), per_b, per_b])
    return pl.pallas_call(
        _merge_kernel,
        out_shape=(jax.ShapeDtypeStruct((b, n, d), F32), jax.ShapeDtypeStruct((b, n, d), F32),
                   jax.ShapeDtypeStruct((b, n, d), BF16)),
        grid=(b, n // tm),
        in_specs=in_specs,
        out_specs=[tile(d)] * 3,
        compiler_params=_cparams(("parallel", "parallel")),
        name="merge_mixers",
    )(a, o0, o1, bonus, g, p, p, p, p, p, p, x, lw['rwkv_ln_g'], lw['rwkv_ln_b'], lw['conv_w'], lw['w_branch_bf16'],
      lw['w_out_bf16'], lw['ones64'], g1, lw['norm2_g'], sc2, sh2)


def _router_kernel(h_ref, rw_ref, o_ref):
    logits = lax.dot_general(rw_ref[...], h_ref[0], (((1,), (1,)), ((), ())), precision=HIGHEST,
                             preferred_element_type=F32)
    z = jnp.exp(logits - jnp.max(logits, axis=0, keepdims=True))
    o_ref[0] = z / jnp.sum(z, axis=0, keepdims=True)


def _router(h, rw_t):
    b, n, d = h.shape
    tm = _pick(n, 512)
    return pl.pallas_call(
        _router_kernel,
        out_shape=jax.ShapeDtypeStruct((b, N_EXPERTS, n), F32),
        grid=(b, n // tm),
        in_specs=[pl.BlockSpec((1, tm, d), lambda bi, i: (bi, i, 0)), pl.BlockSpec((N_EXPERTS, d), lambda bi, i: (0, 0))],
        out_specs=pl.BlockSpec((1, N_EXPERTS, tm), lambda bi, i: (bi, 0, i)),
        compiler_params=_cparams(("parallel", "parallel")),
        name="router_affinity",
    )(h, rw_t)


def _select_kernel(aff_ref, tri_ref, pos_o, cnt_o, *, cap, lb):
    aff = aff_ref[0]
    n = aff.shape[1]
    bits = pltpu.bitcast(aff, jnp.int32)
    count = lambda mask: jnp.sum(mask.astype(F32), axis=1, keepdims=True)

    def refine(i, lo):
        cand = lo | jnp.left_shift(jnp.int32(1), 30 - i)
        return jnp.where(count(bits >= cand) >= cap, cand, lo)

    thr = lax.fori_loop(0, 31, refine, jnp.zeros((N_EXPERTS, 1), jnp.int32))
    gt = bits > thr
    eq = bits == thr
    need = cap - count(gt)
    tri = tri_ref[...]

    def prefix(mask_of_block, write):
        off = jnp.zeros((N_EXPERTS, 1), F32)
        for j in range(n // lb):
            m = mask_of_block(j)
            excl = jnp.dot(m.astype(BF16), tri, preferred_element_type=F32) + off
            write(j, m, excl)
            off = off + jnp.sum(m.astype(F32), axis=1, keepdims=True)

    blk = lambda x, j: x[:, j * lb:(j + 1) * lb]
    sel_blocks = [None] * (n // lb)

    def write_sel(j, m, excl):
        sel_blocks[j] = blk(gt, j) | (m & (excl < need))

    prefix(lambda j: blk(eq, j), write_sel)

    def write_pos(j, m, excl):
        c = excl.astype(jnp.int32)
        cnt_o[0, :, j * lb:(j + 1) * lb] = c
        pos_o[0, :, j * lb:(j + 1) * lb] = jnp.where(m, c, -1)

    prefix(lambda j: sel_blocks[j], write_pos)


def _select(aff, cap):
    b, e, n = aff.shape
    lb = min(n, LANES)
    tri = jnp.asarray(np.triu(np.ones((lb, lb), np.float32), 1), BF16)
    blk = pl.BlockSpec((1, e, n), lambda bi: (bi, 0, 0))
    out = jax.ShapeDtypeStruct((b, e, n), jnp.int32)
    return pl.pallas_call(
        functools.partial(_select_kernel, cap=cap, lb=lb),
        out_shape=(out, out),
        grid=(b,),
        in_specs=[blk, pl.BlockSpec((lb, lb), lambda bi: (0, 0))],
        out_specs=[blk, blk],
        compiler_params=_cparams(("parallel",)),
        name="expert_select",
    )(aff, tri)


GATHER_CHUNK = 64


def _expert_kernel(starts_ref, pos_ref, aff_ref, h_ref, w1_ref, w3_ref, w2_ref, y_o, xin_s, gate_s, w1_s, w3_s, w2_s,
                   *, cap, tile, ntile, tpg, nchunk):
    e, bi, g = pl.program_id(0), pl.program_id(1), pl.program_id(2)

    @pl.when((bi == 0) & (g == 0))
    def _():
        w1_s[...] = w1_ref[0, 0].astype(BF16)
        w3_s[...] = w3_ref[0, 0].astype(BF16)
        w2_s[...] = w2_ref[0, 0].astype(BF16)

    @pl.when(g == 0)
    def _():
        xin_s[...] = jnp.zeros(xin_s.shape, F32)
        gate_s[...] = jnp.zeros(gate_s.shape, F32)

    for tt in range(tpg):
        base = (e * pl.num_programs(1) + bi) * (ntile + 1) + g * tpg + tt
        start, end = starts_ref[base], starts_ref[base + 1]
        w0 = (start // SUBLANES) * SUBLANES
        cols = slice(tt * tile, (tt + 1) * tile)
        for c in range(nchunk):
            r0 = pl.multiple_of(w0 + c * GATHER_CHUNK, SUBLANES)

            @pl.when((r0 < end) & (r0 + GATHER_CHUNK > start))
            def _():
                slot = r0 + lax.broadcasted_iota(jnp.int32, (GATHER_CHUNK, tile), 0)
                onehot = pos_ref[0, 0][:, cols] == slot
                rows = pl.ds(r0, GATHER_CHUNK)
                xin_s[rows, :] += jnp.dot(onehot.astype(BF16), h_ref[0, cols, :], preferred_element_type=F32)
                gate = jnp.sum(jnp.where(onehot, aff_ref[0, 0][:, cols], 0.0), axis=1, keepdims=True)
                gate_s[rows, :] += jnp.broadcast_to(gate, (GATHER_CHUNK, gate_s.shape[1]))

    @pl.when(g == pl.num_programs(2) - 1)
    def _():
        rc = min(cap, SLOT_BLOCK)
        capp = y_o.shape[2]
        if capp > cap:
            y_o[0, 0, cap:capp, :] = jnp.zeros((capp - cap, y_o.shape[3]), BF16)
        for c in range(cap // rc):
            x = xin_s[c * rc:(c + 1) * rc, :].astype(BF16)
            hid = (_silu(jnp.dot(x, w1_s[...], preferred_element_type=F32))
                   * jnp.dot(x, w3_s[...], preferred_element_type=F32))
            y = jnp.dot(hid.astype(BF16), w2_s[...], preferred_element_type=F32)
            y_o[0, 0, c * rc:(c + 1) * rc, :] = (y * gate_s[c * rc:(c + 1) * rc, 0:1]).astype(BF16)


def _moe_tile(n):
    return _pick(n, 256)


def _experts(starts, pos4, aff4, h_bf16, w1, w3, w2, layer, cap):
    b, n, d = h_bf16.shape
    tile = _moe_tile(n)
    ntile = n // tile
    nchunk = tile // GATHER_CHUNK + 1
    rows = cap + nchunk * GATHER_CHUNK
    capp = -(-cap // SLOT_BLOCK) * SLOT_BLOCK
    f = w1.shape[3]
    tg = _pick(n, 2048)
    tpg = tg // tile
    grid_spec = pltpu.PrefetchScalarGridSpec(
        num_scalar_prefetch=1,
        grid=(N_EXPERTS, b, n // tg),
        in_specs=[pl.BlockSpec((1, 1, 1, tg), lambda e, bi, g, s: (bi, e, 0, g)),
                  pl.BlockSpec((1, 1, 1, tg), lambda e, bi, g, s: (bi, e, 0, g)),
                  pl.BlockSpec((1, tg, d), lambda e, bi, g, s: (bi, g, 0)),
                  pl.BlockSpec((1, 1, d, f), lambda e, bi, g, s: (layer, e, 0, 0), pipeline_mode=pl.Buffered(1)),
                  pl.BlockSpec((1, 1, d, f), lambda e, bi, g, s: (layer, e, 0, 0), pipeline_mode=pl.Buffered(1)),
                  pl.BlockSpec((1, 1, f, d), lambda e, bi, g, s: (layer, e, 0, 0), pipeline_mode=pl.Buffered(1))],
        out_specs=pl.BlockSpec((1, 1, capp, d), lambda e, bi, g, s: (e, bi, 0, 0)),
        scratch_shapes=[pltpu.VMEM((rows, d), F32), pltpu.VMEM((rows, LANES), F32),
                        pltpu.VMEM((d, f), BF16), pltpu.VMEM((d, f), BF16), pltpu.VMEM((f, d), BF16)])
    return pl.pallas_call(
        functools.partial(_expert_kernel, cap=cap, tile=tile, ntile=ntile, tpg=tpg, nchunk=nchunk),
        out_shape=jax.ShapeDtypeStruct((N_EXPERTS, b, capp, d), BF16),
        grid_spec=grid_spec,
        compiler_params=_cparams(("arbitrary", "arbitrary", "arbitrary")),
        name="expert_gather_ffn",
    )(starts, pos4, aff4, h_bf16, w1, w3, w2)


def _combine_kernel(starts_ref, pos_ref, *refs, tile, tpg, ntile, sb, nblk):
    y_refs, (x_ref, g2_ref, o_ref) = refs[:nblk], refs[nblk:]
    bi, gi, e = pl.program_id(0), pl.program_id(1), pl.program_id(2)

    @pl.when(e == 0)
    def _():
        o_ref[...] = jnp.zeros(o_ref.shape, F32)

    base = (e * pl.num_programs(0) + bi) * (ntile + 1) + gi * tpg
    k0 = starts_ref[base] // sb
    lane = lax.broadcasted_iota(jnp.int32, (1, LANES), 1)
    for tt in range(tpg):
        start, end = starts_ref[base + tt], starts_ref[base + tt + 1]
        cols = slice(tt * tile, (tt + 1) * tile)
        pos_t = jnp.broadcast_to(pos_ref[0, 0][:, cols].astype(F32), (LANES, tile)).T
        for j, y_ref in enumerate(y_refs):
            s0 = (k0 + j) * sb

            @pl.when((s0 < end) & (s0 + sb > start) & (end > start))
            def _():
                blocks = [pos_t == (s0 + jb * LANES + lane).astype(F32) for jb in range(sb // LANES)]
                onehot_t = jnp.concatenate(blocks, axis=1).astype(BF16)
                o_ref[0, cols, :] += jnp.dot(onehot_t, y_ref[0, 0], preferred_element_type=F32)

    @pl.when(e == pl.num_programs(2) - 1)
    def _():
        o_ref[0] = x_ref[0] + g2_ref[0] * o_ref[0]


def _combine(starts, pos4, y, x, g2):
    b, n, d = x.shape
    tile = _moe_tile(n)
    ntile = n // tile
    gt = _pick(n, 1024)
    tpg = gt // tile
    sb = SLOT_BLOCK
    nsb = y.shape[2] // sb
    nblk = min(gt // sb + 1, nsb)

    def yblk(j):
        def imap(bi, gi, e, s):
            k0 = s[(e * b + bi) * (ntile + 1) + gi * tpg] // sb
            return (e, bi, jnp.minimum(k0 + j, nsb - 1), 0)
        return pl.BlockSpec((1, 1, sb, d), imap)

    grid_spec = pltpu.PrefetchScalarGridSpec(
        num_scalar_prefetch=1,
        grid=(b, n // gt, N_EXPERTS),
        in_specs=[pl.BlockSpec((1, 1, 1, gt), lambda bi, gi, e, s: (bi, e, 0, gi))]
                 + [yblk(j) for j in range(nblk)]
                 + [pl.BlockSpec((1, gt, d), lambda bi, gi, e, s: (bi, gi, 0)),
                    pl.BlockSpec((1, 1, d), lambda bi, gi, e, s: (bi, 0, 0))],
        out_specs=pl.BlockSpec((1, gt, d), lambda bi, gi, e, s: (bi, gi, 0)),
        scratch_shapes=[])
    return pl.pallas_call(
        functools.partial(_combine_kernel, tile=tile, tpg=tpg, ntile=ntile, sb=sb, nblk=nblk),
        out_shape=jax.ShapeDtypeStruct((b, n, d), F32),
        grid_spec=grid_spec,
        compiler_params=_cparams(("parallel", "parallel", "arbitrary")),
        name="expert_combine",
    )(starts, pos4, *([y] * nblk), x, g2)


def _moe(h_f32, h_bf16, x, g2, lw):
    b, n, _ = x.shape
    cap = CAPACITY_FACTOR * n // N_EXPERTS
    aff = _router(h_f32, lw['router_w_t'])
    pos, cnt = _select(aff, cap)
    tile = _moe_tile(n)
    starts = jnp.concatenate([cnt[:, :, ::tile], jnp.full((b, N_EXPERTS, 1), cap, jnp.int32)], axis=2)
    starts = jnp.transpose(starts, (1, 0, 2)).reshape(-1)
    pos4 = pos.reshape(b, N_EXPERTS, 1, n)
    aff4 = aff.reshape(b, N_EXPERTS, 1, n)
    y = _experts(starts, pos4, aff4, h_bf16, lw['exp_w1'], lw['exp_w3'], lw['exp_w2'], lw['layer'], cap)
    return _combine(starts, pos4, y, x, g2)


def _pad_w_in(w_in):
    d = w_in.shape[0]
    sizes = (A_WIDTH,) * 3 + (B_WIDTH,) * 3 + (B_DECAY_RANK, B_ICL_RANK, B_GATE_RANK) + (C_WIDTH,) * 3 + (N_BRANCH * D_MODEL,)
    cuts = np.cumsum((0,) + sizes)
    piece = lambda lo, hi: w_in[:, int(cuts[lo]):int(cuts[hi])]
    zeros = jnp.zeros((d, COL_GATES - COL_LOWRANK - 256), w_in.dtype)
    out = jnp.concatenate([piece(0, 3), piece(3, 6), piece(9, 12), piece(6, 9), zeros, piece(12, 13)], axis=1)
    assert out.shape[1] == N_IN_PAD
    return out.astype(BF16)


def _rows_at(w, lo, total=256):
    pad = [(0, 0)] * w.ndim
    pad[-2] = (lo, total - lo - w.shape[-2])
    return jnp.pad(w, pad)


def _rope_tables(rows):
    half = A_HD // 2
    inv = jnp.power(ROPE_THETA, -jnp.arange(0, half, 2, dtype=F32) / half)
    r = jnp.repeat(jnp.arange(rows, dtype=F32), GRID_W)
    col = jnp.tile(jnp.arange(GRID_W, dtype=F32), rows)
    ang = jnp.concatenate([r[:, None] * inv, col[:, None] * inv], axis=-1)
    cos, sin = jnp.cos(ang), jnp.sin(ang)
    reps = A_WIDTH // A_HD
    cos_t = jnp.tile(jnp.concatenate([cos, cos], axis=-1), (1, reps))
    sin_t = jnp.tile(jnp.concatenate([-sin, sin], axis=-1), (1, reps))
    return cos_t, sin_t


def _layer_weights(i, norm2_g, w_in, q_norm_g, k_norm_g, diff_lambda, diff_subln_g, rwkv_conv_w, rwkv_w0, rwkv_w_up,
                   rwkv_a0, rwkv_a_up, rwkv_g_up, rwkv_k_k, rwkv_k_a, rwkv_r_k, rwkv_ln_g, rwkv_ln_b, conv_w,
                   w_branch, w_out, router_w, exp_w1, exp_w3, exp_w2):
    row = lambda v: v.reshape(1, -1)
    return {
        'norm2_g': row(norm2_g[i]),
        'w_in_pad': _pad_w_in(w_in[i]),
        'q_g': row(jnp.tile(q_norm_g[i], A_WIDTH // A_HD)), 'k_g': row(jnp.tile(k_norm_g[i], A_WIDTH // A_HD)),
        'diff_lambda': diff_lambda[i], 'subln_g': row(diff_subln_g[i]),
        'rwkv_conv_w': rwkv_conv_w[i], 'rwkv_w0': rwkv_w0[i], 'rwkv_a0': rwkv_a0[i],
        'w_up_pad': _rows_at(rwkv_w_up[i], 0), 'a_up_pad': _rows_at(rwkv_a_up[i], B_DECAY_RANK),
        'g_up_pad': _rows_at(rwkv_g_up[i], B_DECAY_RANK + B_ICL_RANK),
        'rwkv_k_k': row(rwkv_k_k[i]), 'rwkv_k_a': row(rwkv_k_a[i]), 'rwkv_r_k': row(rwkv_r_k[i]),
        'rwkv_ln_g': row(rwkv_ln_g[i]), 'rwkv_ln_b': row(rwkv_ln_b[i]),
        'conv_w': conv_w[i], 'w_branch_bf16': w_branch[i].astype(BF16), 'w_out_bf16': w_out[i].astype(BF16),
        'router_w_t': router_w[i].T,
        'exp_w1': exp_w1, 'exp_w3': exp_w3, 'exp_w2': exp_w2, 'layer': i,
        'ones64': _group_ones(B_WIDTH, B_HD).astype(BF16),
    }


def _mixer_inputs(x, norm1_g, sc1, sh1, lw, cos_t, sin_t):
    p = _norm_mod_proj(x, norm1_g, sc1, sh1, lw['w_in_pad'])
    q, k = _qkv_prep(p, lw['q_g'], lw['k_g'], lw['ones64'], cos_t, sin_t)
    prep = dict(zip(PREP_NAMES, _rwkv_prep(p, lw)))
    return p, (q, k), prep


def kernel(x, c, ctx, c_ctx, norm1_g, norm2_g, ada_w, ada_b, w_in, q_norm_g, k_norm_g, diff_lambda, diff_subln_g,
           rwkv_conv_w, rwkv_w0, rwkv_w_up, rwkv_a0, rwkv_a_up, rwkv_g_up, rwkv_k_k, rwkv_k_a, rwkv_r_k, rwkv_ln_g,
           rwkv_ln_b, conv_w, w_branch, w_out, router_w, exp_w1, exp_w3, exp_w2):
    bsz, n_lat, d = x.shape
    depth = ada_w.shape[0]
    cos_t, sin_t = _rope_tables(n_lat // GRID_W)

    cond = jnp.concatenate([c, jnp.broadcast_to(c_ctx[None, :], (SUBLANES - bsz % SUBLANES, d))], axis=0)
    ada = _ada_params(cond, ada_w, ada_b)

    xc, xl = ctx, x
    for i in range(depth):
        lam_init = 0.8 - 0.6 * math.exp(-0.3 * i)
        need_ctx = i != depth - 1
        lw = _layer_weights(i, norm2_g, w_in, q_norm_g, k_norm_g, diff_lambda, diff_subln_g, rwkv_conv_w, rwkv_w0,
                            rwkv_w_up, rwkv_a0, rwkv_a_up, rwkv_g_up, rwkv_k_k, rwkv_k_a, rwkv_r_k, rwkv_ln_g,
                            rwkv_ln_b, conv_w, w_branch, w_out, router_w, exp_w1, exp_w3, exp_w2)
        mods = ada[i].reshape(-1, 6, d)
        lat = [mods[:bsz, j][:, None, :] for j in range(6)]
        cxt = [jnp.broadcast_to(mods[bsz, j][None, None, :], (bsz, 1, d)) for j in range(6)]

        pc, (qc, kc), prep_c = _mixer_inputs(xc, norm1_g[i], cxt[1], cxt[0], lw, None, None)
        pl_, (ql, kl), prep_l = _mixer_inputs(xl, norm1_g[i], lat[1], lat[0], lw, cos_t, sin_t)

        v_cols = slice(COL_ATTN + 2 * A_WIDTH, COL_ATTN + 3 * A_WIDTH)
        k_all = jnp.concatenate([kc, kl], axis=1)
        v_all = jnp.concatenate([pc[:, :, v_cols], pl_[:, :, v_cols]], axis=1)
        al = _flash_attention(lw['diff_lambda'], lw['subln_g'], ql, k_all, v_all, lam_init)
        o0, o1 = _rwkv_bidirectional(prep_c, prep_l)
        n_ctx = xc.shape[1]

        xl, hl, hlb = _merge(al, o0, o1, n_ctx, prep_l['bonus'], prep_l['g'], pl_, xl, lw, lat[2], lat[4], lat[3])
        xl = _moe(hl, hlb, xl, lat[5], lw)
        if need_ctx:
            ac = _ctx_attention(lw['diff_lambda'], lw['subln_g'], qc, kc, pc, lam_init)
            xc, hc, hcb = _merge(ac, o0, o1, 0, prep_c['bonus'], prep_c['g'], pc, xc, lw, cxt[2], cxt[4], cxt[3])
            xc = _moe(hc, hcb, xc, cxt[5], lw)
    return xl
```

```python
import functools
import math

import numpy as np
import jax
import jax.numpy as jnp
from jax import lax
from jax.experimental import pallas as pl
from jax.experimental.pallas import tpu as pltpu

F32 = jnp.float32
BF16 = jnp.bfloat16
HIGHEST = lax.Precision.HIGHEST

D_MODEL = 1024
GRID_W = 64
A_HEADS = 4
A_HD = 64
A_WIDTH = A_HEADS * 2 * A_HD
B_HEADS = 8
B_HD = 64
B_WIDTH = B_HEADS * B_HD
B_DECAY_RANK = 64
B_ICL_RANK = 64
B_GATE_RANK = 128
C_WIDTH = 512
N_BRANCH = 3
N_EXPERTS = 16
EXPERT_FF = 1024
CAPACITY_FACTOR = 2
ROPE_THETA = 10000.0
NORM_EPS = 1e-6
GN_EPS = 64e-5
LOG2_E = math.log2(math.e)

COL_ATTN = 0
COL_RWKV = 1536
COL_CONV = 3072
COL_LOWRANK = 4608
COL_GATES = 5120
N_IN_PAD = 8192

VMEM_LIMIT = 56 * 1024 * 1024
LANES = 128
SUBLANES = 8
SLOT_BLOCK = 256


def _cparams(sem):
    return pltpu.CompilerParams(dimension_semantics=sem, vmem_limit_bytes=VMEM_LIMIT)


def _pick(n, pref):
    t = min(n, pref)
    while n % t:
        t -= SUBLANES
    return t


def _group_ones(width, group):
    idx = np.arange(width) // group
    return jnp.asarray((idx[:, None] == idx[None, :]).astype(np.float32))


def _silu(x):
    return x * jax.nn.sigmoid(x)


def _split2(x):
    hi = x.astype(BF16)
    return hi, (x - hi.astype(F32)).astype(BF16)


def _group_sum(x, ones_bf16):
    hi, lo = _split2(x)
    return (jnp.dot(hi, ones_bf16, preferred_element_type=F32) + jnp.dot(lo, ones_bf16, preferred_element_type=F32))


def _permute_rows(perm_bf16, x):
    hi, lo = _split2(x)
    rest = x - hi.astype(F32) - lo.astype(F32)
    dot = lambda v: jnp.dot(perm_bf16, v, preferred_element_type=F32)
    return dot(hi) + dot(lo) + dot(rest.astype(BF16))


def _dot3(a, b):
    a_hi, a_lo = _split2(a)
    b_hi, b_lo = _split2(b)
    dot = lambda u, v: jnp.dot(u, v, preferred_element_type=F32)
    return dot(a_hi, b_hi) + (dot(a_hi, b_lo) + dot(a_lo, b_hi))


def _ada_kernel(cond_ref, w_ref, b_ref, o_ref):
    c = cond_ref[...]
    o_ref[0] = jnp.dot(_silu(c), w_ref[0], precision=HIGHEST, preferred_element_type=F32) + b_ref[0]


def _ada_params(cond, ada_w, ada_b):
    depth, d, n6 = ada_w.shape
    tn = _pick(n6, 1536)
    return pl.pallas_call(
        _ada_kernel,
        out_shape=jax.ShapeDtypeStruct((depth, cond.shape[0], n6), F32),
        grid=(depth, n6 // tn),
        in_specs=[pl.BlockSpec(cond.shape, lambda i, j: (0, 0)),
                  pl.BlockSpec((1, d, tn), lambda i, j: (i, 0, j)),
                  pl.BlockSpec((1, 1, tn), lambda i, j: (i, 0, j))],
        out_specs=pl.BlockSpec((1, cond.shape[0], tn), lambda i, j: (i, 0, j)),
        compiler_params=_cparams(("parallel", "parallel")),
        name="ada_params",
    )(cond, ada_w, ada_b.reshape(depth, 1, n6))


def _proj_kernel(x_ref, g_ref, sc_ref, sh_ref, w_ref, o_ref):
    x = x_ref[0]
    ms = jnp.mean(x * x, axis=-1, keepdims=True)
    y = x * lax.rsqrt(ms + NORM_EPS) * g_ref[...]
    h = y * (1.0 + sc_ref[0]) + sh_ref[0]
    o_ref[0] = jnp.dot(h.astype(BF16), w_ref[...], preferred_element_type=F32).astype(BF16)


def _norm_mod_proj(x, g, sc, sh, w_bf16):
    b, n, d = x.shape
    npad = w_bf16.shape[1]
    tm = _pick(n, 512)
    tn = 2048
    return pl.pallas_call(
        _proj_kernel,
        out_shape=jax.ShapeDtypeStruct((b, n, npad), BF16),
        grid=(npad // tn, b, n // tm),
        in_specs=[pl.BlockSpec((1, tm, d), lambda j, bi, i: (bi, i, 0)),
                  pl.BlockSpec((1, d), lambda j, bi, i: (0, 0)),
                  pl.BlockSpec((1, 1, d), lambda j, bi, i: (bi, 0, 0)),
                  pl.BlockSpec((1, 1, d), lambda j, bi, i: (bi, 0, 0)),
                  pl.BlockSpec((d, tn), lambda j, bi, i: (0, j))],
        out_specs=pl.BlockSpec((1, tm, tn), lambda j, bi, i: (bi, i, j)),
        compiler_params=_cparams(("parallel", "parallel", "parallel")),
        name="norm_mod_proj",
    )(x, g.reshape(1, d), sc, sh, w_bf16)


def _swap_halves(x, half):
    width = x.shape[-1]
    lane = lax.broadcasted_iota(jnp.int32, (1, width), 1) % (2 * half)
    fwd = pltpu.roll(x, width - half, axis=1)
    bwd = pltpu.roll(x, half, axis=1)
    return jnp.where(lane < half, fwd, bwd)


def _qkv_prep_kernel(*refs, use_rope):
    if use_rope:
        q_ref, k_ref, qg_ref, kg_ref, ones_ref, cos_ref, sin_ref, qo_ref, ko_ref = refs
    else:
        q_ref, k_ref, qg_ref, kg_ref, ones_ref, qo_ref, ko_ref = refs

    def head_norm(x, gain):
        ms = _group_sum(x * x, ones_ref[...]) * (1.0 / A_HD)
        return x * lax.rsqrt(ms + NORM_EPS) * gain

    def rope(x):
        if not use_rope:
            return x
        return x * cos_ref[...] + _swap_halves(x, A_HD // 2) * sin_ref[...]

    q = rope(head_norm(q_ref[0].astype(F32), qg_ref[...]))
    k = rope(head_norm(k_ref[0].astype(F32), kg_ref[...]))
    qo_ref[0] = (q * (A_HD ** -0.5 * LOG2_E)).astype(BF16)
    ko_ref[0] = k.astype(BF16)


def _qkv_prep(p, qg, kg, ones, cos_t, sin_t):
    b, n, _ = p.shape
    tm = _pick(n, 512)
    use_rope = cos_t is not None
    w = A_WIDTH
    col = lambda c: pl.BlockSpec((1, tm, w), lambda bi, i, c=c: (bi, i, COL_ATTN // w + c))
    const = lambda shape: pl.BlockSpec(shape, lambda bi, i: (0,) * len(shape))
    in_specs = [col(0), col(1), const((1, w)), const((1, w)), const((w, w))]
    args = [p, p, qg, kg, ones]
    if use_rope:
        in_specs += [pl.BlockSpec((tm, w), lambda bi, i: (i, 0))] * 2
        args += [cos_t, sin_t]
    out = jax.ShapeDtypeStruct((b, n, w), BF16)
    return pl.pallas_call(
        functools.partial(_qkv_prep_kernel, use_rope=use_rope),
        out_shape=(out, out),
        grid=(b, n // tm),
        in_specs=in_specs,
        out_specs=[pl.BlockSpec((1, tm, w), lambda bi, i: (bi, i, 0))] * 2,
        compiler_params=_cparams(("parallel", "parallel")),
        name="qkv_prep_rope" if use_rope else "qkv_prep",
    )(*args)


def _map_stacked(q):
    lane = lax.broadcasted_iota(jnp.int32, (1, 2 * A_HD), 1)
    zero = jnp.zeros_like(q)
    return jnp.concatenate([jnp.where(lane < A_HD, q, zero), jnp.where(lane >= A_HD, q, zero)], axis=0)


def _lane_tiled(x, width):
    return jnp.concatenate([x] * (width // LANES), axis=1)


def _lane_block_sum(p):
    out = p[:, 0:LANES]
    for j in range(1, p.shape[1] // LANES):
        out = out + p[:, j * LANES:(j + 1) * LANES]
    return out


def _lambda_full(dl_ref, lam_init):
    dl = dl_ref[...]
    s1 = jnp.sum(dl[0:1] * dl[1:2], axis=-1, keepdims=True)
    s2 = jnp.sum(dl[2:3] * dl[3:4], axis=-1, keepdims=True)
    return jnp.exp(s1) - jnp.exp(s2) + lam_init


def _subln(o, sg_ref, lam_init):
    ms = jnp.mean(o * o, axis=-1, keepdims=True)
    return o * lax.rsqrt(ms + NORM_EPS) * sg_ref[...] * (1.0 - lam_init)


def _scores(qm, k):
    return lax.dot_general(qm, k, (((1,), (1,)), ((), ())), preferred_element_type=F32)


def _flash_kernel(dl_ref, sg_ref, q_ref, k_ref, v_ref, o_ref, m_s, l_s, acc_s, *, lam_init, nkv):
    kj = pl.program_id(3)
    tq = q_ref.shape[1]

    @pl.when(kj == 0)
    def _():
        m_s[...] = jnp.full(m_s.shape, -jnp.inf, F32)
        l_s[...] = jnp.zeros(l_s.shape, F32)
        acc_s[...] = jnp.zeros(acc_s.shape, F32)

    s = _scores(_map_stacked(q_ref[0]), k_ref[0])
    m_prev = m_s[...]
    m_new = jnp.maximum(m_prev, jnp.max(s, axis=-1, keepdims=True))
    alpha = jnp.exp2(m_prev - m_new)
    p = jnp.exp2(s - _lane_tiled(m_new, s.shape[1]))
    l_s[...] = alpha * l_s[...] + _lane_block_sum(p)
    acc_s[...] = alpha * acc_s[...] + jnp.dot(p.astype(BF16), v_ref[0], preferred_element_type=F32)
    m_s[...] = m_new

    @pl.when(kj == nkv - 1)
    def _():
        lam = _lambda_full(dl_ref, lam_init)
        o = acc_s[...] / jnp.sum(l_s[...], axis=-1, keepdims=True)
        o_ref[0] = _subln(o[0:tq] - lam * o[tq:2 * tq], sg_ref, lam_init).astype(o_ref.dtype)


V_COL_BLOCK = (COL_ATTN + 2 * A_WIDTH) // (2 * A_HD)


def _pick_lanes(n, pref):
    t = min(n, pref) // LANES * LANES
    while n % t:
        t -= LANES
    return t


def _flash_attention(dl, sg, q, k_all, v_all, lam_init):
    b, s, _ = q.shape
    t = k_all.shape[1]
    hw = 2 * A_HD
    tq = _pick(s, 1024)
    tk = _pick_lanes(t, 3072)
    nkv = t // tk
    return pl.pallas_call(
        functools.partial(_flash_kernel, lam_init=lam_init, nkv=nkv),
        out_shape=jax.ShapeDtypeStruct((b, s, A_WIDTH), BF16),
        grid=(b, A_HEADS, s // tq, nkv),
        in_specs=[pl.BlockSpec((4, A_HD), lambda bi, h, i, j: (0, 0)),
                  pl.BlockSpec((1, hw), lambda bi, h, i, j: (0, 0)),
                  pl.BlockSpec((1, tq, hw), lambda bi, h, i, j: (bi, i, h)),
                  pl.BlockSpec((1, tk, hw), lambda bi, h, i, j: (bi, j, h)),
                  pl.BlockSpec((1, tk, hw), lambda bi, h, i, j: (bi, j, h))],
        out_specs=pl.BlockSpec((1, tq, hw), lambda bi, h, i, j: (bi, i, h)),
        scratch_shapes=[pltpu.VMEM((2 * tq, LANES), F32), pltpu.VMEM((2 * tq, LANES), F32), pltpu.VMEM((2 * tq, hw), F32)],
        compiler_params=_cparams(("parallel", "parallel", "parallel", "arbitrary")),
        name="diff_flash_attention",
    )(dl, sg, q, k_all, v_all)


def _ctx_attn_kernel(dl_ref, sg_ref, q_ref, k_ref, v_ref, o_ref, *, lam_init):
    n = q_ref.shape[1]
    s = _scores(_map_stacked(q_ref[0]), k_ref[0])
    p = jnp.exp2(s - jnp.max(s, axis=-1, keepdims=True))
    o = jnp.dot(p.astype(BF16), v_ref[0], preferred_element_type=F32) / jnp.sum(p, axis=-1, keepdims=True)
    lam = _lambda_full(dl_ref, lam_init)
    o_ref[0] = _subln(o[0:n] - lam * o[n:2 * n], sg_ref, lam_init).astype(o_ref.dtype)


def _ctx_attention(dl, sg, q, k, v, lam_init):
    b, n, _ = q.shape
    hw = 2 * A_HD
    blk = pl.BlockSpec((1, n, hw), lambda bi, h: (bi, 0, h))
    vblk = pl.BlockSpec((1, n, hw), lambda bi, h: (bi, 0, V_COL_BLOCK + h))
    return pl.pallas_call(
        functools.partial(_ctx_attn_kernel, lam_init=lam_init),
        out_shape=jax.ShapeDtypeStruct((b, n, A_WIDTH), BF16),
        grid=(b, A_HEADS),
        in_specs=[pl.BlockSpec((4, A_HD), lambda bi, h: (0, 0)), pl.BlockSpec((1, hw), lambda bi, h: (0, 0)),
                  blk, blk, vblk],
        out_specs=blk,
        compiler_params=_cparams(("parallel", "parallel")),
        name="ctx_attention",
    )(dl, sg, q, k, v)


HALO_ROWS = 16


def _halo_specs(tm, width, colblk, n):
    rh = tm // HALO_ROWS
    last = n // HALO_ROWS - 1
    main = pl.BlockSpec((1, tm, width), lambda bi, i: (bi, i, colblk))
    prev = pl.BlockSpec((1, HALO_ROWS, width), lambda bi, i: (bi, jnp.maximum(i * rh - 1, 0), colblk))
    nxt = pl.BlockSpec((1, HALO_ROWS, width), lambda bi, i: (bi, jnp.minimum((i + 1) * rh, last), colblk))
    return [main, prev, nxt]


def _dwconv3(x, prev_row, next_row, w_ref):
    tm = x.shape[0]
    row = lax.broadcasted_iota(jnp.int32, (tm, 1), 0)
    xp = jnp.where(row == 0, prev_row, pltpu.roll(x, 1, axis=0))
    xn = jnp.where(row == tm - 1, next_row, pltpu.roll(x, tm - 1, axis=0))
    w = w_ref[...]
    return w[0:1] * xp + w[1:2] * x + w[2:3] * xn


def _edge_rows(prev_ref, next_ref):
    i = pl.program_id(1)
    n = pl.num_programs(1)
    pr = prev_ref[0].astype(F32)[HALO_ROWS - 1:HALO_ROWS]
    nr = next_ref[0].astype(F32)[0:1]
    pr = jnp.where(i == 0, jnp.zeros_like(pr), pr)
    nr = jnp.where(i == n - 1, jnp.zeros_like(nr), nr)
    return pr, nr


def _softplus(u):
    return jnp.maximum(u, 0.0) + jnp.log(1.0 + jnp.exp(-jnp.abs(u)))


SCAN_BLOCK = 64


def _block_triangles(tm):
    i = np.arange(tm)
    same = (i[:, None] // SCAN_BLOCK) == (i[None, :] // SCAN_BLOCK)
    return jnp.asarray(np.stack([same & (i[None, :] <= i[:, None]), same & (i[None, :] >= i[:, None])]), BF16)


def _rwkv_prep_kernel(rkv_ref, prev_ref, next_ref, lr_ref, cw_ref, w0_ref, wup_ref, a0_ref, aup_ref, gup_ref,
                      kk_ref_w, ka_ref, rk_ref, ones_ref, tri_ref,
                      v_o, kk_o, g_o, bonus_o, gam0_o, b0_o, kd0_o, r0_o, gam1_o, b1_o, kd1_o, r1_o):
    pr, nr = _edge_rows(prev_ref, next_ref)
    rkv = _dwconv3(rkv_ref[0].astype(F32), pr, nr, cw_ref)
    w = B_WIDTH
    r, k, v = rkv[:, 0:w], rkv[:, w:2 * w], rkv[:, 2 * w:3 * w]
    gsum = lambda t: _group_sum(t, ones_ref[...])
    kkr = k * kk_ref_w[...]
    nrm = jnp.maximum(jnp.sqrt(gsum(kkr * kkr)), 1e-12)
    kk = kkr / nrm
    lr = lr_ref[0].astype(F32)
    th = jnp.tanh(lr)
    sg = jax.nn.sigmoid(lr)
    v_o[0] = v
    kk_o[0] = kk
    g_o[0] = _dot3(sg, gup_ref[...])
    kd_sum = jnp.zeros_like(k)
    outs = ((gam0_o, b0_o, kd0_o, r0_o), (gam1_o, b1_o, kd1_o, r1_o))
    for d in range(2):
        z = w0_ref[d:d + 1] + _dot3(th, wup_ref[d])
        wlog = -_softplus(-z) - 0.5
        log_decay = -jnp.exp(wlog)
        log_gamma = _permute_rows(tri_ref[d], log_decay)
        gamma, inv_gamma = jnp.exp(log_gamma), jnp.exp(-log_gamma)
        a = jax.nn.sigmoid(a0_ref[d:d + 1] + _dot3(lr, aup_ref[d]))
        kd = k * (1.0 + (a - 1.0) * ka_ref[...])
        outs[d][0][0] = gamma
        outs[d][1][0] = kk * a * inv_gamma
        outs[d][2][0] = kd * inv_gamma
        outs[d][3][0] = r * gamma
        kd_sum = kd_sum + kd
    bonus_o[0] = gsum(r * kd_sum * rk_ref[...]) * v


def _rwkv_prep(p, lw):
    b, n, _ = p.shape
    tm = _pick(n, 256)
    w = B_WIDTH
    w3 = 3 * w
    const = lambda shape: pl.BlockSpec(shape, lambda bi, i: (0,) * len(shape))
    in_specs = _halo_specs(tm, w3, COL_RWKV // w3, n) + [
        pl.BlockSpec((1, tm, 256), lambda bi, i: (bi, i, COL_LOWRANK // 256)),
        const((3, w3)), const((2, w)), const((2, 256, w)), const((2, w)), const((2, 256, w)), const((256, w)),
        const((1, w)), const((1, w)), const((1, w)), const((w, w)), const((2, tm, tm))]
    assert tm % SCAN_BLOCK == 0
    out = jax.ShapeDtypeStruct((b, n, w), F32)
    return pl.pallas_call(
        _rwkv_prep_kernel,
        out_shape=(out,) * len(PREP_NAMES),
        grid=(b, n // tm),
        in_specs=in_specs,
        out_specs=[pl.BlockSpec((1, tm, w), lambda bi, i: (bi, i, 0))] * len(PREP_NAMES),
        compiler_params=_cparams(("parallel", "parallel")),
        name="rwkv_prep",
    )(p, p, p, p, lw['rwkv_conv_w'], lw['rwkv_w0'], lw['w_up_pad'], lw['rwkv_a0'], lw['a_up_pad'], lw['g_up_pad'],
      lw['rwkv_k_k'], lw['rwkv_k_a'], lw['rwkv_r_k'], lw['ones64'], _block_triangles(tm))


PREP_NAMES = ('v', 'kk', 'g', 'bonus', 'gam0', 'b0', 'kd0', 'r0', 'gam1', 'b1', 'kd1', 'r1')


V_SLABS = B_HD // 2 // SUBLANES


def _scan_kernel(gam_ref, b_ref, kd_ref, r_ref, kk_ref, kkx_ref, v_ref, y_ref, s_ref, sa_ref, *, tc):
    @pl.when(pl.program_id(0) == 0)
    def _():
        s_ref[...] = jnp.zeros(s_ref.shape, F32)
        sa_ref[...] = jnp.zeros(sa_ref.shape, F32)

    lanes = s_ref.shape[-1]
    vrows = B_HD // 2
    bcast = lambda row: jnp.broadcast_to(row, (SUBLANES, lanes))

    def step(t, last_of_block):
        base = pl.multiple_of(t * vrows, vrows)
        sa = [sa_ref[s] for s in range(V_SLABS)]
        vv = [v_ref[pl.ds(base + SUBLANES * s, SUBLANES), :] for s in range(V_SLABS)]
        y = [jnp.zeros((SUBLANES, lanes), F32) for _ in range(V_SLABS)]
        san = [jnp.zeros((SUBLANES, lanes), F32) for _ in range(V_SLABS)]
        for k in range(B_HD):
            row = lambda ref: bcast(ref[k, pl.ds(t, 1), :])
            gk, bk, kdk, rk = row(gam_ref), row(b_ref), row(kd_ref), row(r_ref)
            kk_next = bcast(kkx_ref[k, 0:1, :]) if last_of_block else bcast(kk_ref[k, pl.ds(t + 1, 1), :])
            kkn = kk_next * gk
            for s in range(V_SLABS):
                new = s_ref[s, k] - sa[s] * bk + vv[s] * kdk
                s_ref[s, k] = new * gk if last_of_block else new
                y[s] = y[s] + new * rk
                san[s] = san[s] + new * kkn
        for s in range(V_SLABS):
            y_ref[pl.ds(base + SUBLANES * s, SUBLANES), :] = y[s]
            sa_ref[s] = san[s]

    def body(t, carry):
        step(t, False)
        return carry

    lax.fori_loop(0, tc - 1, body, 0)
    step(tc - 1, True)


def _rwkv_scan(w, bb, kd, r, kk, v):
    _, t, lanes = w.shape
    tc = SCAN_BLOCK
    assert t % tc == 0
    vrows = B_HD // 2
    last8 = t // SUBLANES - 1
    kspec = pl.BlockSpec((B_HD, tc, lanes), lambda i: (0, i, 0))
    kxspec = pl.BlockSpec((B_HD, SUBLANES, lanes), lambda i: (0, jnp.minimum((i + 1) * (tc // SUBLANES), last8), 0))
    vspec = pl.BlockSpec((tc * vrows, lanes), lambda i: (i, 0))
    return pl.pallas_call(
        functools.partial(_scan_kernel, tc=tc),
        out_shape=jax.ShapeDtypeStruct(v.shape, F32),
        grid=(t // tc,),
        in_specs=[kspec] * 5 + [kxspec, vspec],
        out_specs=vspec,
        scratch_shapes=[pltpu.VMEM((V_SLABS, B_HD, SUBLANES, lanes), F32), pltpu.VMEM((V_SLABS, SUBLANES, lanes), F32)],
        compiler_params=_cparams(("arbitrary",)),
        name="rwkv_scan",
    )(w, bb, kd, r, kk, kk, v)


TIME_BLOCK = 128
HEAD_PAIRS = B_WIDTH // LANES


def _scan_block_maps(nctx, nlat):
    fwd_c = lambda i: jnp.minimum(i, nctx - 1)
    fwd_l = lambda i: jnp.clip(i - nctx, 0, nlat - 1)
    bwd_c = lambda i: jnp.clip(nctx - 1 - i, 0, nctx - 1)
    bwd_l = lambda i: jnp.clip(nlat - 1 - (i - nctx), 0, nlat - 1)
    return fwd_c, fwd_l, bwd_c, bwd_l


def _fill_transposed(u_ref, src_ref, direction, rev_ref, bsz):
    for b in range(bsz):
        for j in range(HEAD_PAIRS):
            tile = src_ref[b, :, j * LANES:(j + 1) * LANES]
            if direction == 1:
                tile = _permute_rows(rev_ref[...], tile)
            q = (direction * bsz + b) * HEAD_PAIRS + j
            u_ref[q * LANES:(q + 1) * LANES, :] = tile.T


def _fill_both(u_ref, cf_ref, lf_ref, cb_ref, lb_ref, rev_ref, nctx, bsz):
    i = pl.program_id(0)

    @pl.when(i < nctx)
    def _():
        _fill_transposed(u_ref, cf_ref, 0, rev_ref, bsz)
        _fill_transposed(u_ref, cb_ref, 1, rev_ref, bsz)

    @pl.when(i >= nctx)
    def _():
        _fill_transposed(u_ref, lf_ref, 0, rev_ref, bsz)
        _fill_transposed(u_ref, lb_ref, 1, rev_ref, bsz)


def _to_scan_k_kernel(cf_ref, lf_ref, cb_ref, lb_ref, rev_ref, o_ref, u_ref, *, nctx, bsz):
    _fill_both(u_ref, cf_ref, lf_ref, cb_ref, lb_ref, rev_ref, nctx, bsz)
    nrow = u_ref.shape[0] // B_HD
    for k in range(B_HD):
        g = u_ref[pl.ds(k, nrow, stride=B_HD), :]
        o_ref[k] = jnp.concatenate([g, g], axis=0).T


def _to_scan_v_kernel(cf_ref, lf_ref, cb_ref, lb_ref, rev_ref, o_ref, u_ref, *, nctx, bsz):
    _fill_both(u_ref, cf_ref, lf_ref, cb_ref, lb_ref, rev_ref, nctx, bsz)
    nrow = u_ref.shape[0] // B_HD
    vrows = B_HD // 2
    for vi in range(vrows):
        halves = [u_ref[pl.ds(half * vrows + vi, nrow, stride=B_HD), :] for half in range(2)]
        o_ref[pl.ds(vi, TIME_BLOCK, stride=vrows), :] = jnp.concatenate(halves, axis=0).T


def _to_scan(ctx_f, lat_f, ctx_b, lat_b, rev, values):
    bsz, nc, width = ctx_f.shape
    nl = lat_f.shape[1]
    nctx, nlat = nc // TIME_BLOCK, nl // TIME_BLOCK
    lanes = 4 * bsz * B_HEADS
    t = nc + nl
    maps = _scan_block_maps(nctx, nlat)
    src = lambda f: pl.BlockSpec((bsz, TIME_BLOCK, width), lambda i, f=f: (0, f(i), 0))
    if values:
        kern, out_shape = _to_scan_v_kernel, (t * (B_HD // 2), lanes)
        out_spec = pl.BlockSpec((TIME_BLOCK * (B_HD // 2), lanes), lambda i: (i, 0))
    else:
        kern, out_shape = _to_scan_k_kernel, (B_HD, t, lanes)
        out_spec = pl.BlockSpec((B_HD, TIME_BLOCK, lanes), lambda i: (0, i, 0))
    return pl.pallas_call(
        functools.partial(kern, nctx=nctx, bsz=bsz),
        out_shape=jax.ShapeDtypeStruct(out_shape, F32),
        grid=(nctx + nlat,),
        in_specs=[src(maps[0]), src(maps[1]), src(maps[2]), src(maps[3]),
                  pl.BlockSpec((TIME_BLOCK, TIME_BLOCK), lambda i: (0, 0))],
        out_specs=out_spec,
        scratch_shapes=[pltpu.VMEM((2 * bsz * HEAD_PAIRS * LANES, TIME_BLOCK), F32)],
        compiler_params=_cparams(("parallel",)),
        name="to_scan_v" if values else "to_scan_k",
    )(ctx_f, lat_f, ctx_b, lat_b, rev)


def _from_scan_kernel(y_ref, rev_ref, o0_ref, o1_ref, u_ref, *, bsz):
    nrow = u_ref.shape[0] // B_HD
    vrows = B_HD // 2
    for vi in range(vrows):
        mt = y_ref[pl.ds(vi, TIME_BLOCK, stride=vrows), :].T
        for half in range(2):
            u_ref[pl.ds(half * vrows + vi, nrow, stride=B_HD), :] = mt[half * nrow:(half + 1) * nrow]
    for direction, o_ref in enumerate((o0_ref, o1_ref)):
        for b in range(bsz):
            for j in range(HEAD_PAIRS):
                q = (direction * bsz + b) * HEAD_PAIRS + j
                tile = u_ref[q * LANES:(q + 1) * LANES, :].T
                if direction == 1:
                    tile = _permute_rows(rev_ref[...], tile)
                o_ref[b, :, j * LANES:(j + 1) * LANES] = tile


def _from_scan(y, rev, bsz, nc, nl):
    nctx, nlat = nc // TIME_BLOCK, nl // TIME_BLOCK
    lanes = y.shape[1]
    vrows = B_HD // 2
    bwd = lambda i: jnp.where(i < nctx, nctx - 1 - i, 2 * nctx + nlat - 1 - i)
    out = jax.ShapeDtypeStruct((bsz, nc + nl, B_WIDTH), F32)
    return pl.pallas_call(
        functools.partial(_from_scan_kernel, bsz=bsz),
        out_shape=(out, out),
        grid=(nctx + nlat,),
        in_specs=[pl.BlockSpec((TIME_BLOCK * vrows, lanes), lambda i: (i, 0)),
                  pl.BlockSpec((TIME_BLOCK, TIME_BLOCK), lambda i: (0, 0))],
        out_specs=[pl.BlockSpec((bsz, TIME_BLOCK, B_WIDTH), lambda i: (0, i, 0)),
                   pl.BlockSpec((bsz, TIME_BLOCK, B_WIDTH), lambda i: (0, bwd(i), 0))],
        scratch_shapes=[pltpu.VMEM((2 * bsz * HEAD_PAIRS * LANES, TIME_BLOCK), F32)],
        compiler_params=_cparams(("parallel",)),
        name="from_scan",
    )(y, rev)


def _rwkv_bidirectional(prep_c, prep_l):
    bsz, nc, _ = prep_c['v'].shape
    nl = prep_l['v'].shape[1]
    rev = jnp.asarray(np.eye(TIME_BLOCK, dtype=np.float32)[::-1].copy(), BF16)
    pair = lambda fwd, bwd, values=False: _to_scan(prep_c[fwd], prep_l[fwd], prep_c[bwd], prep_l[bwd], rev, values)
    y = _rwkv_scan(pair('gam0', 'gam1'), pair('b0', 'b1'), pair('kd0', 'kd1'), pair('r0', 'r1'), pair('kk', 'kk'),
                   pair('v', 'v', True))
    return _from_scan(y, rev, bsz, nc, nl)


def _merge_kernel(a_ref, o0_ref, o1_ref, bonus_ref, g_ref, cv_ref, cprev_ref, cnext_ref, ga_ref, gb_ref, gc_ref,
                  x_ref, lng_ref, lnb_ref, cw_ref, wb_ref, wo_ref, ones_ref, g1_ref, n2_ref, sc_ref, sh_ref,
                  x_o, hf_o, hb_o):
    gmean = lambda t: _group_sum(t, ones_ref[...]) * (1.0 / B_HD)

    def group_norm(o):
        dlt = o - gmean(o)
        return dlt * lax.rsqrt(gmean(dlt * dlt) + GN_EPS) * lng_ref[...] + lnb_ref[...]

    yb = (group_norm(o0_ref[0]) + group_norm(o1_ref[0]) + bonus_ref[0]) * g_ref[0]

    w = C_WIDTH
    u = lambda blk: blk[:, w:2 * w] * blk[:, 2 * w:3 * w]
    cv = cv_ref[0].astype(F32)
    i = pl.program_id(1)
    pr = u(cprev_ref[0].astype(F32)[HALO_ROWS - 1:HALO_ROWS])
    nr = u(cnext_ref[0].astype(F32)[0:1])
    pr = jnp.where(i == 0, jnp.zeros_like(pr), pr)
    nr = jnp.where(i == pl.num_programs(1) - 1, jnp.zeros_like(nr), nr)
    yc = cv[:, 0:w] * _dwconv3(u(cv), pr, nr, cw_ref)

    d = D_MODEL
    m = jnp.zeros((cv.shape[0], d), F32)
    for j, (yj, gate_ref) in enumerate(((a_ref[0], ga_ref), (yb, gb_ref), (yc, gc_ref))):
        gate = jax.nn.sigmoid(gate_ref[0].astype(F32))
        m = m + gate * jnp.dot(yj.astype(BF16), wb_ref[j], preferred_element_type=F32)
    ymix = jnp.dot(m.astype(BF16), wo_ref[...], preferred_element_type=F32)
    x1 = x_ref[0] + g1_ref[0] * ymix
    x_o[0] = x1
    ms = jnp.mean(x1 * x1, axis=-1, keepdims=True)
    h = x1 * lax.rsqrt(ms + NORM_EPS) * n2_ref[...] * (1.0 + sc_ref[0]) + sh_ref[0]
    hf_o[0] = h
    hb_o[0] = h.astype(BF16)


def _merge(a, o0, o1, row_off, bonus, g, p, x, lw, g1, sc2, sh2):
    b, n, d = x.shape
    tm = _pick(n, 256)
    w = C_WIDTH
    assert row_off % tm == 0
    tile = lambda width: pl.BlockSpec((1, tm, width), lambda bi, i: (bi, i, 0))
    scan_tile = pl.BlockSpec((1, tm, w), lambda bi, i: (bi, i + row_off // tm, 0))
    const = lambda shape: pl.BlockSpec(shape, lambda bi, i: (0,) * len(shape))
    per_b = pl.BlockSpec((1, 1, d), lambda bi, i: (bi, 0, 0))
    gate = lambda j: pl.BlockSpec((1, tm, d), lambda bi, i, j=j: (bi, i, COL_GATES // d + j))
    in_specs = ([tile(w), scan_tile, scan_tile, tile(w), tile(w)]
                + _halo_specs(tm, 3 * w, COL_CONV // (3 * w), n) + [gate(0), gate(1), gate(2)]
                + [tile(d), const((1, w)), const((1, w)), const((3, w)), const((N_BRANCH, w, d)), const((d, d)),
                   const((w, w)), per_b, const((1, d)), per_b, per_b])
    return pl.pallas_call(
        _merge_kernel,
        out_shape=(jax.ShapeDtypeStruct((b, n, d), F32), jax.ShapeDtypeStruct((b, n, d), F32),
                   jax.ShapeDtypeStruct((b, n, d), BF16)),
        grid=(b, n // tm),
        in_specs=in_specs,
        out_specs=[tile(d)] * 3,
        compiler_params=_cparams(("parallel", "parallel")),
        name="merge_mixers",
    )(a, o0, o1, bonus, g, p, p, p, p, p, p, x, lw['rwkv_ln_g'], lw['rwkv_ln_b'], lw['conv_w'], lw['w_branch_bf16'],
      lw['w_out_bf16'], lw['ones64'], g1, lw['norm2_g'], sc2, sh2)


def _router_kernel(h_ref, rw_ref, o_ref):
    logits = lax.dot_general(rw_ref[...], h_ref[0], (((1,), (1,)), ((), ())), precision=HIGHEST,
                             preferred_element_type=F32)
    z = jnp.exp(logits - jnp.max(logits, axis=0, keepdims=True))
    o_ref[0] = z / jnp.sum(z, axis=0, keepdims=True)


def _router(h, rw_t):
    b, n, d = h.shape
    tm = _pick(n, 512)
    return pl.pallas_call(
        _router_kernel,
        out_shape=jax.ShapeDtypeStruct((b, N_EXPERTS, n), F32),
        grid=(b, n // tm),
        in_specs=[pl.BlockSpec((1, tm, d), lambda bi, i: (bi, i, 0)), pl.BlockSpec((N_EXPERTS, d), lambda bi, i: (0, 0))],
        out_specs=pl.BlockSpec((1, N_EXPERTS, tm), lambda bi, i: (bi, 0, i)),
        compiler_params=_cparams(("parallel", "parallel")),
        name="router_affinity",
    )(h, rw_t)


def _select_kernel(aff_ref, tri_ref, pos_o, cnt_o, *, cap, lb):
    aff = aff_ref[0]
    n = aff.shape[1]
    bits = pltpu.bitcast(aff, jnp.int32)
    count = lambda mask: jnp.sum(mask.astype(F32), axis=1, keepdims=True)

    def refine(i, lo):
        cand = lo | jnp.left_shift(jnp.int32(1), 30 - i)
        return jnp.where(count(bits >= cand) >= cap, cand, lo)

    thr = lax.fori_loop(0, 31, refine, jnp.zeros((N_EXPERTS, 1), jnp.int32))
    gt = bits > thr
    eq = bits == thr
    need = cap - count(gt)
    tri = tri_ref[...]

    def prefix(mask_of_block, write):
        off = jnp.zeros((N_EXPERTS, 1), F32)
        for j in range(n // lb):
            m = mask_of_block(j)
            excl = jnp.dot(m.astype(BF16), tri, preferred_element_type=F32) + off
            write(j, m, excl)
            off = off + jnp.sum(m.astype(F32), axis=1, keepdims=True)

    blk = lambda x, j: x[:, j * lb:(j + 1) * lb]
    sel_blocks = [None] * (n // lb)

    def write_sel(j, m, excl):
        sel_blocks[j] = blk(gt, j) | (m & (excl < need))

    prefix(lambda j: blk(eq, j), write_sel)

    def write_pos(j, m, excl):
        c = excl.astype(jnp.int32)
        cnt_o[0, :, j * lb:(j + 1) * lb] = c
        pos_o[0, :, j * lb:(j + 1) * lb] = jnp.where(m, c, -1)

    prefix(lambda j: sel_blocks[j], write_pos)


def _select(aff, cap):
    b, e, n = aff.shape
    lb = min(n, LANES)
    tri = jnp.asarray(np.triu(np.ones((lb, lb), np.float32), 1), BF16)
    blk = pl.BlockSpec((1, e, n), lambda bi: (bi, 0, 0))
    out = jax.ShapeDtypeStruct((b, e, n), jnp.int32)
    return pl.pallas_call(
        functools.partial(_select_kernel, cap=cap, lb=lb),
        out_shape=(out, out),
        grid=(b,),
        in_specs=[blk, pl.BlockSpec((lb, lb), lambda bi: (0, 0))],
        out_specs=[blk, blk],
        compiler_params=_cparams(("parallel",)),
        name="expert_select",
    )(aff, tri)


GATHER_CHUNK = 64
FFN_ROWS = 512


def _expert_kernel(starts_ref, pos_ref, aff_ref, h_ref, w1_ref, w3_ref, w2_ref, y_o, xin_s, gate_s, w1_s, w3_s, w2_s,
                   *, cap, tile, ntile, tpg, nchunk):
    e, bi, g = pl.program_id(0), pl.program_id(1), pl.program_id(2)

    @pl.when((bi == 0) & (g == 0))
    def _():
        w1_s[...] = w1_ref[0, 0].astype(BF16)
        w3_s[...] = w3_ref[0, 0].astype(BF16)
        w2_s[...] = w2_ref[0, 0].astype(BF16)

    @pl.when(g == 0)
    def _():
        xin_s[...] = jnp.zeros(xin_s.shape, F32)
        gate_s[...] = jnp.zeros(gate_s.shape, F32)

    for tt in range(tpg):
        base = (e * pl.num_programs(1) + bi) * (ntile + 1) + g * tpg + tt
        start, end = starts_ref[base], starts_ref[base + 1]
        w0 = (start // SUBLANES) * SUBLANES
        cols = slice(tt * tile, (tt + 1) * tile)
        for c in range(nchunk):
            r0 = pl.multiple_of(w0 + c * GATHER_CHUNK, SUBLANES)

            @pl.when((r0 < end) & (r0 + GATHER_CHUNK > start))
            def _():
                slot = r0 + lax.broadcasted_iota(jnp.int32, (GATHER_CHUNK, tile), 0)
                onehot = pos_ref[0, 0][:, cols] == slot
                rows = pl.ds(r0, GATHER_CHUNK)
                xin_s[rows, :] += jnp.dot(onehot.astype(BF16), h_ref[0, cols, :], preferred_element_type=F32)
                gate = jnp.sum(jnp.where(onehot, aff_ref[0, 0][:, cols], 0.0), axis=1, keepdims=True)
                gate_s[rows, :] += jnp.broadcast_to(gate, (GATHER_CHUNK, gate_s.shape[1]))

    @pl.when(g == pl.num_programs(2) - 1)
    def _():
        rc = min(cap, FFN_ROWS)
        capp = y_o.shape[2]
        if capp > cap:
            y_o[0, 0, cap:capp, :] = jnp.zeros((capp - cap, y_o.shape[3]), BF16)
        for c in range(cap // rc):
            x = xin_s[c * rc:(c + 1) * rc, :].astype(BF16)
            hid = (_silu(jnp.dot(x, w1_s[...], preferred_element_type=F32))
                   * jnp.dot(x, w3_s[...], preferred_element_type=F32))
            y = jnp.dot(hid.astype(BF16), w2_s[...], preferred_element_type=F32)
            y_o[0, 0, c * rc:(c + 1) * rc, :] = (y * gate_s[c * rc:(c + 1) * rc, 0:1]).astype(BF16)


def _moe_tile(n):
    return _pick(n, 256)


def _experts(starts, pos4, aff4, h_bf16, w1, w3, w2, layer, cap):
    b, n, d = h_bf16.shape
    tile = _moe_tile(n)
    ntile = n // tile
    nchunk = tile // GATHER_CHUNK + 1
    rows = cap + nchunk * GATHER_CHUNK
    capp = -(-cap // SLOT_BLOCK) * SLOT_BLOCK
    f = w1.shape[3]
    tg = _pick(n, 2048)
    tpg = tg // tile
    grid_spec = pltpu.PrefetchScalarGridSpec(
        num_scalar_prefetch=1,
        grid=(N_EXPERTS, b, n // tg),
        in_specs=[pl.BlockSpec((1, 1, 1, tg), lambda e, bi, g, s: (bi, e, 0, g)),
                  pl.BlockSpec((1, 1, 1, tg), lambda e, bi, g, s: (bi, e, 0, g)),
                  pl.BlockSpec((1, tg, d), lambda e, bi, g, s: (bi, g, 0)),
                  pl.BlockSpec((1, 1, d, f), lambda e, bi, g, s: (layer, e, 0, 0), pipeline_mode=pl.Buffered(1)),
                  pl.BlockSpec((1, 1, d, f), lambda e, bi, g, s: (layer, e, 0, 0), pipeline_mode=pl.Buffered(1)),
                  pl.BlockSpec((1, 1, f, d), lambda e, bi, g, s: (layer, e, 0, 0), pipeline_mode=pl.Buffered(1))],
        out_specs=pl.BlockSpec((1, 1, capp, d), lambda e, bi, g, s: (e, bi, 0, 0)),
        scratch_shapes=[pltpu.VMEM((rows, d), F32), pltpu.VMEM((rows, LANES), F32),
                        pltpu.VMEM((d, f), BF16), pltpu.VMEM((d, f), BF16), pltpu.VMEM((f, d), BF16)])
    return pl.pallas_call(
        functools.partial(_expert_kernel, cap=cap, tile=tile, ntile=ntile, tpg=tpg, nchunk=nchunk),
        out_shape=jax.ShapeDtypeStruct((N_EXPERTS, b, capp, d), BF16),
        grid_spec=grid_spec,
        compiler_params=_cparams(("arbitrary", "arbitrary", "arbitrary")),
        name="expert_gather_ffn",
    )(starts, pos4, aff4, h_bf16, w1, w3, w2)


def _combine_kernel(starts_ref, pos_ref, *refs, tile, tpg, ntile, sb, nblk):
    y_refs, (x_ref, g2_ref, o_ref) = refs[:nblk], refs[nblk:]
    bi, gi, e = pl.program_id(0), pl.program_id(1), pl.program_id(2)

    @pl.when(e == 0)
    def _():
        o_ref[...] = jnp.zeros(o_ref.shape, F32)

    base = (e * pl.num_programs(0) + bi) * (ntile + 1) + gi * tpg
    start, end = starts_ref[base], starts_ref[base + tpg]
    k0 = start // sb
    lane = lax.broadcasted_iota(jnp.int32, (1, LANES), 1)
    pos_t = jnp.concatenate(
        [jnp.broadcast_to(pos_ref[0, 0][:, tt * tile:(tt + 1) * tile].astype(F32), (LANES, tile)).T
         for tt in range(tpg)], axis=0)
    for j, y_ref in enumerate(y_refs):
        s0 = (k0 + j) * sb

        @pl.when((s0 < end) & (end > start))
        def _():
            blocks = [pos_t == (s0 + jb * LANES + lane).astype(F32) for jb in range(sb // LANES)]
            onehot_t = jnp.concatenate(blocks, axis=1).astype(BF16)
            o_ref[0] += jnp.dot(onehot_t, y_ref[0, 0], preferred_element_type=F32)

    @pl.when(e == pl.num_programs(2) - 1)
    def _():
        o_ref[0] = x_ref[0] + g2_ref[0] * o_ref[0]


def _combine(starts, pos4, y, x, g2):
    b, n, d = x.shape
    tile = _moe_tile(n)
    ntile = n // tile
    gt = _pick(n, 1024)
    tpg = gt // tile
    sb = SLOT_BLOCK
    nsb = y.shape[2] // sb
    nblk = min(gt // sb + 1, nsb)

    def yblk(j):
        def imap(bi, gi, e, s):
            k0 = s[(e * b + bi) * (ntile + 1) + gi * tpg] // sb
            return (e, bi, jnp.minimum(k0 + j, nsb - 1), 0)
        return pl.BlockSpec((1, 1, sb, d), imap)

    grid_spec = pltpu.PrefetchScalarGridSpec(
        num_scalar_prefetch=1,
        grid=(b, n // gt, N_EXPERTS),
        in_specs=[pl.BlockSpec((1, 1, 1, gt), lambda bi, gi, e, s: (bi, e, 0, gi))]
                 + [yblk(j) for j in range(nblk)]
                 + [pl.BlockSpec((1, gt, d), lambda bi, gi, e, s: (bi, gi, 0)),
                    pl.BlockSpec((1, 1, d), lambda bi, gi, e, s: (bi, 0, 0))],
        out_specs=pl.BlockSpec((1, gt, d), lambda bi, gi, e, s: (bi, gi, 0)),
        scratch_shapes=[])
    return pl.pallas_call(
        functools.partial(_combine_kernel, tile=tile, tpg=tpg, ntile=ntile, sb=sb, nblk=nblk),
        out_shape=jax.ShapeDtypeStruct((b, n, d), F32),
        grid_spec=grid_spec,
        compiler_params=_cparams(("parallel", "parallel", "arbitrary")),
        name="expert_combine",
    )(starts, pos4, *([y] * nblk), x, g2)


def _moe(h_f32, h_bf16, x, g2, lw):
    b, n, _ = x.shape
    cap = CAPACITY_FACTOR * n // N_EXPERTS
    aff = _router(h_f32, lw['router_w_t'])
    pos, cnt = _select(aff, cap)
    tile = _moe_tile(n)
    starts = jnp.concatenate([cnt[:, :, ::tile], jnp.full((b, N_EXPERTS, 1), cap, jnp.int32)], axis=2)
    starts = jnp.transpose(starts, (1, 0, 2)).reshape(-1)
    pos4 = pos.reshape(b, N_EXPERTS, 1, n)
    aff4 = aff.reshape(b, N_EXPERTS, 1, n)
    y = _experts(starts, pos4, aff4, h_bf16, lw['exp_w1'], lw['exp_w3'], lw['exp_w2'], lw['layer'], cap)
    return _combine(starts, pos4, y, x, g2)


def _pad_w_in(w_in):
    d = w_in.shape[0]
    sizes = (A_WIDTH,) * 3 + (B_WIDTH,) * 3 + (B_DECAY_RANK, B_ICL_RANK, B_GATE_RANK) + (C_WIDTH,) * 3 + (N_BRANCH * D_MODEL,)
    cuts = np.cumsum((0,) + sizes)
    piece = lambda lo, hi: w_in[:, int(cuts[lo]):int(cuts[hi])]
    zeros = jnp.zeros((d, COL_GATES - COL_LOWRANK - 256), w_in.dtype)
    out = jnp.concatenate([piece(0, 3), piece(3, 6), piece(9, 12), piece(6, 9), zeros, piece(12, 13)], axis=1)
    assert out.shape[1] == N_IN_PAD
    return out.astype(BF16)


def _rows_at(w, lo, total=256):
    pad = [(0, 0)] * w.ndim
    pad[-2] = (lo, total - lo - w.shape[-2])
    return jnp.pad(w, pad)


def _rope_tables(rows):
    half = A_HD // 2
    inv = jnp.power(ROPE_THETA, -jnp.arange(0, half, 2, dtype=F32) / half)
    r = jnp.repeat(jnp.arange(rows, dtype=F32), GRID_W)
    col = jnp.tile(jnp.arange(GRID_W, dtype=F32), rows)
    ang = jnp.concatenate([r[:, None] * inv, col[:, None] * inv], axis=-1)
    cos, sin = jnp.cos(ang), jnp.sin(ang)
    reps = A_WIDTH // A_HD
    cos_t = jnp.tile(jnp.concatenate([cos, cos], axis=-1), (1, reps))
    sin_t = jnp.tile(jnp.concatenate([-sin, sin], axis=-1), (1, reps))
    return cos_t, sin_t


def _layer_weights(i, norm2_g, w_in, q_norm_g, k_norm_g, diff_lambda, diff_subln_g, rwkv_conv_w, rwkv_w0, rwkv_w_up,
                   rwkv_a0, rwkv_a_up, rwkv_g_up, rwkv_k_k, rwkv_k_a, rwkv_r_k, rwkv_ln_g, rwkv_ln_b, conv_w,
                   w_branch, w_out, router_w, exp_w1, exp_w3, exp_w2):
    row = lambda v: v.reshape(1, -1)
    return {
        'norm2_g': row(norm2_g[i]),
        'w_in_pad': _pad_w_in(w_in[i]),
        'q_g': row(jnp.tile(q_norm_g[i], A_WIDTH // A_HD)), 'k_g': row(jnp.tile(k_norm_g[i], A_WIDTH // A_HD)),
        'diff_lambda': diff_lambda[i], 'subln_g': row(diff_subln_g[i]),
        'rwkv_conv_w': rwkv_conv_w[i], 'rwkv_w0': rwkv_w0[i], 'rwkv_a0': rwkv_a0[i],
        'w_up_pad': _rows_at(rwkv_w_up[i], 0), 'a_up_pad': _rows_at(rwkv_a_up[i], B_DECAY_RANK),
        'g_up_pad': _rows_at(rwkv_g_up[i], B_DECAY_RANK + B_ICL_RANK),
        'rwkv_k_k': row(rwkv_k_k[i]), 'rwkv_k_a': row(rwkv_k_a[i]), 'rwkv_r_k': row(rwkv_r_k[i]),
        'rwkv_ln_g': row(rwkv_ln_g[i]), 'rwkv_ln_b': row(rwkv_ln_b[i]),
        'conv_w': conv_w[i], 'w_branch_bf16': w_branch[i].astype(BF16), 'w_out_bf16': w_out[i].astype(BF16),
        'router_w_t': router_w[i].T,
        'exp_w1': exp_w1, 'exp_w3': exp_w3, 'exp_w2': exp_w2, 'layer': i,
        'ones64': _group_ones(B_WIDTH, B_HD).astype(BF16),
    }


def _mixer_inputs(x, norm1_g, sc1, sh1, lw, cos_t, sin_t):
    p = _norm_mod_proj(x, norm1_g, sc1, sh1, lw['w_in_pad'])
    q, k = _qkv_prep(p, lw['q_g'], lw['k_g'], lw['ones64'], cos_t, sin_t)
    prep = dict(zip(PREP_NAMES, _rwkv_prep(p, lw)))
    return p, (q, k), prep


def kernel(x, c, ctx, c_ctx, norm1_g, norm2_g, ada_w, ada_b, w_in, q_norm_g, k_norm_g, diff_lambda, diff_subln_g,
           rwkv_conv_w, rwkv_w0, rwkv_w_up, rwkv_a0, rwkv_a_up, rwkv_g_up, rwkv_k_k, rwkv_k_a, rwkv_r_k, rwkv_ln_g,
           rwkv_ln_b, conv_w, w_branch, w_out, router_w, exp_w1, exp_w3, exp_w2):
    bsz, n_lat, d = x.shape
    depth = ada_w.shape[0]
    cos_t, sin_t = _rope_tables(n_lat // GRID_W)

    cond = jnp.concatenate([c, jnp.broadcast_to(c_ctx[None, :], (SUBLANES - bsz % SUBLANES, d))], axis=0)
    ada = _ada_params(cond, ada_w, ada_b)

    xc, xl = ctx, x
    for i in range(depth):
        lam_init = 0.8 - 0.6 * math.exp(-0.3 * i)
        need_ctx = i != depth - 1
        lw = _layer_weights(i, norm2_g, w_in, q_norm_g, k_norm_g, diff_lambda, diff_subln_g, rwkv_conv_w, rwkv_w0,
                            rwkv_w_up, rwkv_a0, rwkv_a_up, rwkv_g_up, rwkv_k_k, rwkv_k_a, rwkv_r_k, rwkv_ln_g,
                            rwkv_ln_b, conv_w, w_branch, w_out, router_w, exp_w1, exp_w3, exp_w2)
        mods = ada[i].reshape(-1, 6, d)
        lat = [mods[:bsz, j][:, None, :] for j in range(6)]
        cxt = [jnp.broadcast_to(mods[bsz, j][None, None, :], (bsz, 1, d)) for j in range(6)]

        pc, (qc, kc), prep_c = _mixer_inputs(xc, norm1_g[i], cxt[1], cxt[0], lw, None, None)
        pl_, (ql, kl), prep_l = _mixer_inputs(xl, norm1_g[i], lat[1], lat[0], lw, cos_t, sin_t)

        v_cols = slice(COL_ATTN + 2 * A_WIDTH, COL_ATTN + 3 * A_WIDTH)
        k_all = jnp.concatenate([kc, kl], axis=1)
        v_all = jnp.concatenate([pc[:, :, v_cols], pl_[:, :, v_cols]], axis=1)
        al = _flash_attention(lw['diff_lambda'], lw['subln_g'], ql, k_all, v_all, lam_init)
        o0, o1 = _rwkv_bidirectional(prep_c, prep_l)
        n_ctx = xc.shape[1]

        xl, hl, hlb = _merge(al, o0, o1, n_ctx, prep_l['bonus'], prep_l['g'], pl_, xl, lw, lat[2], lat[4], lat[3])
        xl = _moe(hl, hlb, xl, lat[5], lw)
        if need_ctx:
            ac = _ctx_attention(lw['diff_lambda'], lw['subln_g'], qc, kc, pc, lam_init)
            xc, hc, hcb = _merge(ac, o0, o1, 0, prep_c['bonus'], prep_c['g'], pc, xc, lw, cxt[2], cxt[4], cxt[3])
            xc = _moe(hc, hcb, xc, cxt[5], lw)
    return xl
```

```python
import functools
import math

import numpy as np
import jax
import jax.numpy as jnp
from jax import lax
from jax.experimental import pallas as pl
from jax.experimental.pallas import tpu as pltpu

F32 = jnp.float32
BF16 = jnp.bfloat16
HIGHEST = lax.Precision.HIGHEST

D_MODEL = 1024
GRID_W = 64
A_HEADS = 4
A_HD = 64
A_WIDTH = A_HEADS * 2 * A_HD
B_HEADS = 8
B_HD = 64
B_WIDTH = B_HEADS * B_HD
B_DECAY_RANK = 64
B_ICL_RANK = 64
B_GATE_RANK = 128
C_WIDTH = 512
N_BRANCH = 3
N_EXPERTS = 16
EXPERT_FF = 1024
CAPACITY_FACTOR = 2
ROPE_THETA = 10000.0
NORM_EPS = 1e-6
GN_EPS = 64e-5
LOG2_E = math.log2(math.e)

COL_ATTN = 0
COL_RWKV = 1536
COL_CONV = 3072
COL_LOWRANK = 4608
COL_GATES = 5120
N_IN_PAD = 8192

VMEM_LIMIT = 56 * 1024 * 1024
LANES = 128
SUBLANES = 8
SLOT_BLOCK = 256


def _cparams(sem):
    return pltpu.CompilerParams(dimension_semantics=sem, vmem_limit_bytes=VMEM_LIMIT)


def _pick(n, pref):
    t = min(n, pref)
    while n % t:
        t -= SUBLANES
    return t


def _group_ones(width, group):
    idx = np.arange(width) // group
    return jnp.asarray((idx[:, None] == idx[None, :]).astype(np.float32))


def _silu(x):
    return x * jax.nn.sigmoid(x)


def _split2(x):
    hi = x.astype(BF16)
    return hi, (x - hi.astype(F32)).astype(BF16)


def _group_sum(x, ones_bf16):
    hi, lo = _split2(x)
    return (jnp.dot(hi, ones_bf16, preferred_element_type=F32) + jnp.dot(lo, ones_bf16, preferred_element_type=F32))


def _permute_rows(perm_bf16, x):
    hi, lo = _split2(x)
    rest = x - hi.astype(F32) - lo.astype(F32)
    dot = lambda v: jnp.dot(perm_bf16, v, preferred_element_type=F32)
    return dot(hi) + dot(lo) + dot(rest.astype(BF16))


def _dot3(a, b):
    a_hi, a_lo = _split2(a)
    b_hi, b_lo = _split2(b)
    dot = lambda u, v: jnp.dot(u, v, preferred_element_type=F32)
    return dot(a_hi, b_hi) + (dot(a_hi, b_lo) + dot(a_lo, b_hi))


def _ada_kernel(cond_ref, w_ref, b_ref, o_ref):
    c = cond_ref[...]
    o_ref[0] = jnp.dot(_silu(c), w_ref[0], precision=HIGHEST, preferred_element_type=F32) + b_ref[0]


def _ada_params(cond, ada_w, ada_b):
    depth, d, n6 = ada_w.shape
    tn = _pick(n6, 1536)
    return pl.pallas_call(
        _ada_kernel,
        out_shape=jax.ShapeDtypeStruct((depth, cond.shape[0], n6), F32),
        grid=(depth, n6 // tn),
        in_specs=[pl.BlockSpec(cond.shape, lambda i, j: (0, 0)),
                  pl.BlockSpec((1, d, tn), lambda i, j: (i, 0, j)),
                  pl.BlockSpec((1, 1, tn), lambda i, j: (i, 0, j))],
        out_specs=pl.BlockSpec((1, cond.shape[0], tn), lambda i, j: (i, 0, j)),
        compiler_params=_cparams(("parallel", "parallel")),
        name="ada_params",
    )(cond, ada_w, ada_b.reshape(depth, 1, n6))


def _proj_kernel(x_ref, g_ref, sc_ref, sh_ref, w_ref, o_ref):
    x = x_ref[0]
    ms = jnp.mean(x * x, axis=-1, keepdims=True)
    y = x * lax.rsqrt(ms + NORM_EPS) * g_ref[...]
    h = y * (1.0 + sc_ref[0]) + sh_ref[0]
    o_ref[0] = jnp.dot(h.astype(BF16), w_ref[...], preferred_element_type=F32).astype(BF16)


def _norm_mod_proj(x, g, sc, sh, w_bf16):
    b, n, d = x.shape
    npad = w_bf16.shape[1]
    tm = _pick(n, 512)
    tn = 4096
    return pl.pallas_call(
        _proj_kernel,
        out_shape=jax.ShapeDtypeStruct((b, n, npad), BF16),
        grid=(npad // tn, b, n // tm),
        in_specs=[pl.BlockSpec((1, tm, d), lambda j, bi, i: (bi, i, 0)),
                  pl.BlockSpec((1, d), lambda j, bi, i: (0, 0)),
                  pl.BlockSpec((1, 1, d), lambda j, bi, i: (bi, 0, 0)),
                  pl.BlockSpec((1, 1, d), lambda j, bi, i: (bi, 0, 0)),
                  pl.BlockSpec((d, tn), lambda j, bi, i: (0, j))],
        out_specs=pl.BlockSpec((1, tm, tn), lambda j, bi, i: (bi, i, j)),
        compiler_params=_cparams(("parallel", "parallel", "parallel")),
        name="norm_mod_proj",
    )(x, g.reshape(1, d), sc, sh, w_bf16)


def _swap_halves(x, half):
    width = x.shape[-1]
    lane = lax.broadcasted_iota(jnp.int32, (1, width), 1) % (2 * half)
    fwd = pltpu.roll(x, width - half, axis=1)
    bwd = pltpu.roll(x, half, axis=1)
    return jnp.where(lane < half, fwd, bwd)


def _qkv_prep_kernel(*refs, use_rope):
    if use_rope:
        q_ref, k_ref, qg_ref, kg_ref, ones_ref, cos_ref, sin_ref, qo_ref, ko_ref = refs
    else:
        q_ref, k_ref, qg_ref, kg_ref, ones_ref, qo_ref, ko_ref = refs

    def head_norm(x, gain):
        ms = _group_sum(x * x, ones_ref[...]) * (1.0 / A_HD)
        return x * lax.rsqrt(ms + NORM_EPS) * gain

    def rope(x):
        if not use_rope:
            return x
        return x * cos_ref[...] + _swap_halves(x, A_HD // 2) * sin_ref[...]

    q = rope(head_norm(q_ref[0].astype(F32), qg_ref[...]))
    k = rope(head_norm(k_ref[0].astype(F32), kg_ref[...]))
    qo_ref[0] = (q * (A_HD ** -0.5 * LOG2_E)).astype(BF16)
    ko_ref[0] = k.astype(BF16)


def _qkv_prep(p, qg, kg, ones, cos_t, sin_t):
    b, n, _ = p.shape
    tm = _pick(n, 512)
    use_rope = cos_t is not None
    w = A_WIDTH
    col = lambda c: pl.BlockSpec((1, tm, w), lambda bi, i, c=c: (bi, i, COL_ATTN // w + c))
    const = lambda shape: pl.BlockSpec(shape, lambda bi, i: (0,) * len(shape))
    in_specs = [col(0), col(1), const((1, w)), const((1, w)), const((w, w))]
    args = [p, p, qg, kg, ones]
    if use_rope:
        in_specs += [pl.BlockSpec((tm, w), lambda bi, i: (i, 0))] * 2
        args += [cos_t, sin_t]
    out = jax.ShapeDtypeStruct((b, n, w), BF16)
    return pl.pallas_call(
        functools.partial(_qkv_prep_kernel, use_rope=use_rope),
        out_shape=(out, out),
        grid=(b, n // tm),
        in_specs=in_specs,
        out_specs=[pl.BlockSpec((1, tm, w), lambda bi, i: (bi, i, 0))] * 2,
        compiler_params=_cparams(("parallel", "parallel")),
        name="qkv_prep_rope" if use_rope else "qkv_prep",
    )(*args)


def _map_stacked(q):
    lane = lax.broadcasted_iota(jnp.int32, (1, 2 * A_HD), 1)
    zero = jnp.zeros_like(q)
    return jnp.concatenate([jnp.where(lane < A_HD, q, zero), jnp.where(lane >= A_HD, q, zero)], axis=0)


def _lane_tiled(x, width):
    return jnp.concatenate([x] * (width // LANES), axis=1)


def _lane_block_sum(p):
    out = p[:, 0:LANES]
    for j in range(1, p.shape[1] // LANES):
        out = out + p[:, j * LANES:(j + 1) * LANES]
    return out


def _lambda_full(dl_ref, lam_init):
    dl = dl_ref[...]
    s1 = jnp.sum(dl[0:1] * dl[1:2], axis=-1, keepdims=True)
    s2 = jnp.sum(dl[2:3] * dl[3:4], axis=-1, keepdims=True)
    return jnp.exp(s1) - jnp.exp(s2) + lam_init


def _subln(o, sg_ref, lam_init):
    ms = jnp.mean(o * o, axis=-1, keepdims=True)
    return o * lax.rsqrt(ms + NORM_EPS) * sg_ref[...] * (1.0 - lam_init)


def _scores(qm, k):
    return lax.dot_general(qm, k, (((1,), (1,)), ((), ())), preferred_element_type=F32)


def _flash_kernel(dl_ref, sg_ref, q_ref, k_ref, v_ref, o_ref, m_s, l_s, acc_s, *, lam_init, nkv):
    kj = pl.program_id(3)
    tq = q_ref.shape[1]

    @pl.when(kj == 0)
    def _():
        m_s[...] = jnp.full(m_s.shape, -jnp.inf, F32)
        l_s[...] = jnp.zeros(l_s.shape, F32)
        acc_s[...] = jnp.zeros(acc_s.shape, F32)

    s = _scores(_map_stacked(q_ref[0]), k_ref[0])
    m_prev = m_s[...]
    m_new = jnp.maximum(m_prev, jnp.max(s, axis=-1, keepdims=True))
    alpha = jnp.exp2(m_prev - m_new)
    p = jnp.exp2(s - _lane_tiled(m_new, s.shape[1]))
    l_s[...] = alpha * l_s[...] + _lane_block_sum(p)
    acc_s[...] = alpha * acc_s[...] + jnp.dot(p.astype(BF16), v_ref[0], preferred_element_type=F32)
    m_s[...] = m_new

    @pl.when(kj == nkv - 1)
    def _():
        lam = _lambda_full(dl_ref, lam_init)
        o = acc_s[...] / jnp.sum(l_s[...], axis=-1, keepdims=True)
        o_ref[0] = _subln(o[0:tq] - lam * o[tq:2 * tq], sg_ref, lam_init).astype(o_ref.dtype)


V_COL_BLOCK = (COL_ATTN + 2 * A_WIDTH) // (2 * A_HD)


def _pick_lanes(n, pref):
    t = min(n, pref) // LANES * LANES
    while n % t:
        t -= LANES
    return t


def _flash_attention(dl, sg, q, k_all, v_all, lam_init):
    b, s, _ = q.shape
    t = k_all.shape[1]
    hw = 2 * A_HD
    tq = _pick(s, 1024)
    tk = _pick_lanes(t, 3072)
    nkv = t // tk
    return pl.pallas_call(
        functools.partial(_flash_kernel, lam_init=lam_init, nkv=nkv),
        out_shape=jax.ShapeDtypeStruct((b, s, A_WIDTH), BF16),
        grid=(b, A_HEADS, s // tq, nkv),
        in_specs=[pl.BlockSpec((4, A_HD), lambda bi, h, i, j: (0, 0)),
                  pl.BlockSpec((1, hw), lambda bi, h, i, j: (0, 0)),
                  pl.BlockSpec((1, tq, hw), lambda bi, h, i, j: (bi, i, h)),
                  pl.BlockSpec((1, tk, hw), lambda bi, h, i, j: (bi, j, h)),
                  pl.BlockSpec((1, tk, hw), lambda bi, h, i, j: (bi, j, h))],
        out_specs=pl.BlockSpec((1, tq, hw), lambda bi, h, i, j: (bi, i, h)),
        scratch_shapes=[pltpu.VMEM((2 * tq, LANES), F32), pltpu.VMEM((2 * tq, LANES), F32), pltpu.VMEM((2 * tq, hw), F32)],
        compiler_params=_cparams(("parallel", "parallel", "parallel", "arbitrary")),
        name="diff_flash_attention",
    )(dl, sg, q, k_all, v_all)


def _ctx_attn_kernel(dl_ref, sg_ref, q_ref, k_ref, v_ref, o_ref, *, lam_init):
    n = q_ref.shape[1]
    s = _scores(_map_stacked(q_ref[0]), k_ref[0])
    p = jnp.exp2(s - jnp.max(s, axis=-1, keepdims=True))
    o = jnp.dot(p.astype(BF16), v_ref[0], preferred_element_type=F32) / jnp.sum(p, axis=-1, keepdims=True)
    lam = _lambda_full(dl_ref, lam_init)
    o_ref[0] = _subln(o[0:n] - lam * o[n:2 * n], sg_ref, lam_init).astype(o_ref.dtype)


def _ctx_attention(dl, sg, q, k, v, lam_init):
    b, n, _ = q.shape
    hw = 2 * A_HD
    blk = pl.BlockSpec((1, n, hw), lambda bi, h: (bi, 0, h))
    vblk = pl.BlockSpec((1, n, hw), lambda bi, h: (bi, 0, V_COL_BLOCK + h))
    return pl.pallas_call(
        functools.partial(_ctx_attn_kernel, lam_init=lam_init),
        out_shape=jax.ShapeDtypeStruct((b, n, A_WIDTH), BF16),
        grid=(b, A_HEADS),
        in_specs=[pl.BlockSpec((4, A_HD), lambda bi, h: (0, 0)), pl.BlockSpec((1, hw), lambda bi, h: (0, 0)),
                  blk, blk, vblk],
        out_specs=blk,
        compiler_params=_cparams(("parallel", "parallel")),
        name="ctx_attention",
    )(dl, sg, q, k, v)


HALO_ROWS = 16


def _halo_specs(tm, width, colblk, n):
    rh = tm // HALO_ROWS
    last = n // HALO_ROWS - 1
    main = pl.BlockSpec((1, tm, width), lambda bi, i: (bi, i, colblk))
    prev = pl.BlockSpec((1, HALO_ROWS, width), lambda bi, i: (bi, jnp.maximum(i * rh - 1, 0), colblk))
    nxt = pl.BlockSpec((1, HALO_ROWS, width), lambda bi, i: (bi, jnp.minimum((i + 1) * rh, last), colblk))
    return [main, prev, nxt]


def _dwconv3(x, prev_row, next_row, w_ref):
    tm = x.shape[0]
    row = lax.broadcasted_iota(jnp.int32, (tm, 1), 0)
    xp = jnp.where(row == 0, prev_row, pltpu.roll(x, 1, axis=0))
    xn = jnp.where(row == tm - 1, next_row, pltpu.roll(x, tm - 1, axis=0))
    w = w_ref[...]
    return w[0:1] * xp + w[1:2] * x + w[2:3] * xn


def _edge_rows(prev_ref, next_ref):
    i = pl.program_id(1)
    n = pl.num_programs(1)
    pr = prev_ref[0].astype(F32)[HALO_ROWS - 1:HALO_ROWS]
    nr = next_ref[0].astype(F32)[0:1]
    pr = jnp.where(i == 0, jnp.zeros_like(pr), pr)
    nr = jnp.where(i == n - 1, jnp.zeros_like(nr), nr)
    return pr, nr


def _softplus(u):
    return jnp.maximum(u, 0.0) + jnp.log(1.0 + jnp.exp(-jnp.abs(u)))


SCAN_BLOCK = 64


def _block_triangles(tm):
    i = np.arange(tm)
    same = (i[:, None] // SCAN_BLOCK) == (i[None, :] // SCAN_BLOCK)
    return jnp.asarray(np.stack([same & (i[None, :] <= i[:, None]), same & (i[None, :] >= i[:, None])]), BF16)


def _rwkv_prep_kernel(rkv_ref, prev_ref, next_ref, lr_ref, cw_ref, w0_ref, wup_ref, a0_ref, aup_ref, gup_ref,
                      kk_ref_w, ka_ref, rk_ref, ones_ref, tri_ref,
                      v_o, kk_o, g_o, bonus_o, gam0_o, b0_o, kd0_o, r0_o, gam1_o, b1_o, kd1_o, r1_o):
    pr, nr = _edge_rows(prev_ref, next_ref)
    rkv = _dwconv3(rkv_ref[0].astype(F32), pr, nr, cw_ref)
    w = B_WIDTH
    r, k, v = rkv[:, 0:w], rkv[:, w:2 * w], rkv[:, 2 * w:3 * w]
    gsum = lambda t: _group_sum(t, ones_ref[...])
    kkr = k * kk_ref_w[...]
    nrm = jnp.maximum(jnp.sqrt(gsum(kkr * kkr)), 1e-12)
    kk = kkr / nrm
    lr = lr_ref[0].astype(F32)
    th = jnp.tanh(lr)
    sg = jax.nn.sigmoid(lr)
    v_o[0] = v
    kk_o[0] = kk
    g_o[0] = _dot3(sg, gup_ref[...])
    kd_sum = jnp.zeros_like(k)
    outs = ((gam0_o, b0_o, kd0_o, r0_o), (gam1_o, b1_o, kd1_o, r1_o))
    for d in range(2):
        z = w0_ref[d:d + 1] + _dot3(th, wup_ref[d])
        wlog = -_softplus(-z) - 0.5
        log_decay = -jnp.exp(wlog)
        log_gamma = _permute_rows(tri_ref[d], log_decay)
        gamma, inv_gamma = jnp.exp(log_gamma), jnp.exp(-log_gamma)
        a = jax.nn.sigmoid(a0_ref[d:d + 1] + _dot3(lr, aup_ref[d]))
        kd = k * (1.0 + (a - 1.0) * ka_ref[...])
        outs[d][0][0] = gamma
        outs[d][1][0] = kk * a * inv_gamma
        outs[d][2][0] = kd * inv_gamma
        outs[d][3][0] = r * gamma
        kd_sum = kd_sum + kd
    bonus_o[0] = gsum(r * kd_sum * rk_ref[...]) * v


def _rwkv_prep(p, lw):
    b, n, _ = p.shape
    tm = _pick(n, 256)
    w = B_WIDTH
    w3 = 3 * w
    const = lambda shape: pl.BlockSpec(shape, lambda bi, i: (0,) * len(shape))
    in_specs = _halo_specs(tm, w3, COL_RWKV // w3, n) + [
        pl.BlockSpec((1, tm, 256), lambda bi, i: (bi, i, COL_LOWRANK // 256)),
        const((3, w3)), const((2, w)), const((2, 256, w)), const((2, w)), const((2, 256, w)), const((256, w)),
        const((1, w)), const((1, w)), const((1, w)), const((w, w)), const((2, tm, tm))]
    assert tm % SCAN_BLOCK == 0
    out = jax.ShapeDtypeStruct((b, n, w), F32)
    return pl.pallas_call(
        _rwkv_prep_kernel,
        out_shape=(out,) * len(PREP_NAMES),
        grid=(b, n // tm),
        in_specs=in_specs,
        out_specs=[pl.BlockSpec((1, tm, w), lambda bi, i: (bi, i, 0))] * len(PREP_NAMES),
        compiler_params=_cparams(("parallel", "parallel")),
        name="rwkv_prep",
    )(p, p, p, p, lw['rwkv_conv_w'], lw['rwkv_w0'], lw['w_up_pad'], lw['rwkv_a0'], lw['a_up_pad'], lw['g_up_pad'],
      lw['rwkv_k_k'], lw['rwkv_k_a'], lw['rwkv_r_k'], lw['ones64'], _block_triangles(tm))


PREP_NAMES = ('v', 'kk', 'g', 'bonus', 'gam0', 'b0', 'kd0', 'r0', 'gam1', 'b1', 'kd1', 'r1')


V_SLABS = B_HD // 2 // SUBLANES


def _scan_kernel(gam_ref, b_ref, kd_ref, r_ref, kk_ref, kkx_ref, v_ref, y_ref, s_ref, sa_ref, *, tc):
    @pl.when(pl.program_id(0) == 0)
    def _():
        s_ref[...] = jnp.zeros(s_ref.shape, F32)
        sa_ref[...] = jnp.zeros(sa_ref.shape, F32)

    lanes = s_ref.shape[-1]
    vrows = B_HD // 2
    bcast = lambda row: jnp.broadcast_to(row, (SUBLANES, lanes))

    def step(t, last_of_block):
        base = pl.multiple_of(t * vrows, vrows)
        sa = [sa_ref[s] for s in range(V_SLABS)]
        vv = [v_ref[pl.ds(base + SUBLANES * s, SUBLANES), :] for s in range(V_SLABS)]
        y = [jnp.zeros((SUBLANES, lanes), F32) for _ in range(V_SLABS)]
        san = [jnp.zeros((SUBLANES, lanes), F32) for _ in range(V_SLABS)]
        for k in range(B_HD):
            row = lambda ref: bcast(ref[k, pl.ds(t, 1), :])
            gk, bk, kdk, rk = row(gam_ref), row(b_ref), row(kd_ref), row(r_ref)
            kk_next = bcast(kkx_ref[k, 0:1, :]) if last_of_block else bcast(kk_ref[k, pl.ds(t + 1, 1), :])
            kkn = kk_next * gk
            for s in range(V_SLABS):
                new = s_ref[s, k] - sa[s] * bk + vv[s] * kdk
                s_ref[s, k] = new * gk if last_of_block else new
                y[s] = y[s] + new * rk
                san[s] = san[s] + new * kkn
        for s in range(V_SLABS):
            y_ref[pl.ds(base + SUBLANES * s, SUBLANES), :] = y[s]
            sa_ref[s] = san[s]

    def body(t, carry):
        step(t, False)
        return carry

    lax.fori_loop(0, tc - 1, body, 0)
    step(tc - 1, True)


def _rwkv_scan(gam, bb, kd, r, kk, v):
    _, t, lanes = gam.shape
    tc = SCAN_BLOCK
    assert t % tc == 0
    vrows = B_HD // 2
    last8 = t // SUBLANES - 1
    kspec = pl.BlockSpec((B_HD, tc, lanes), lambda i: (0, i, 0))
    kxspec = pl.BlockSpec((B_HD, SUBLANES, lanes), lambda i: (0, jnp.minimum((i + 1) * (tc // SUBLANES), last8), 0))
    vspec = pl.BlockSpec((tc * vrows, lanes), lambda i: (i, 0))
    return pl.pallas_call(
        functools.partial(_scan_kernel, tc=tc),
        out_shape=jax.ShapeDtypeStruct(v.shape, F32),
        grid=(t // tc,),
        in_specs=[kspec] * 5 + [kxspec, vspec],
        out_specs=vspec,
        scratch_shapes=[pltpu.VMEM((V_SLABS, B_HD, SUBLANES, lanes), F32), pltpu.VMEM((V_SLABS, SUBLANES, lanes), F32)],
        compiler_params=_cparams(("arbitrary",)),
        name="rwkv_scan",
    )(gam, bb, kd, r, kk, kk, v)


TIME_BLOCK = 128
HEAD_PAIRS = B_WIDTH // LANES


def _scan_block_maps(nctx, nlat):
    fwd_c = lambda i: jnp.minimum(i, nctx - 1)
    fwd_l = lambda i: jnp.clip(i - nctx, 0, nlat - 1)
    bwd_c = lambda i: jnp.clip(nctx - 1 - i, 0, nctx - 1)
    bwd_l = lambda i: jnp.clip(nlat - 1 - (i - nctx), 0, nlat - 1)
    return fwd_c, fwd_l, bwd_c, bwd_l


def _fill_transposed(u_ref, src_ref, direction, rev_ref, bsz):
    for b in range(bsz):
        for j in range(HEAD_PAIRS):
            tile = src_ref[b, :, j * LANES:(j + 1) * LANES]
            if direction == 1:
                tile = _permute_rows(rev_ref[...], tile)
            q = (direction * bsz + b) * HEAD_PAIRS + j
            u_ref[q * LANES:(q + 1) * LANES, :] = tile.T


def _fill_both(u_ref, cf_ref, lf_ref, cb_ref, lb_ref, rev_ref, nctx, bsz):
    i = pl.program_id(0)

    @pl.when(i < nctx)
    def _():
        _fill_transposed(u_ref, cf_ref, 0, rev_ref, bsz)
        _fill_transposed(u_ref, cb_ref, 1, rev_ref, bsz)

    @pl.when(i >= nctx)
    def _():
        _fill_transposed(u_ref, lf_ref, 0, rev_ref, bsz)
        _fill_transposed(u_ref, lb_ref, 1, rev_ref, bsz)


def _to_scan_k_kernel(cf_ref, lf_ref, cb_ref, lb_ref, rev_ref, o_ref, u_ref, *, nctx, bsz):
    _fill_both(u_ref, cf_ref, lf_ref, cb_ref, lb_ref, rev_ref, nctx, bsz)
    nrow = u_ref.shape[0] // B_HD
    for k in range(B_HD):
        g = u_ref[pl.ds(k, nrow, stride=B_HD), :]
        o_ref[k] = jnp.concatenate([g, g], axis=0).T


def _to_scan_v_kernel(cf_ref, lf_ref, cb_ref, lb_ref, rev_ref, o_ref, u_ref, *, nctx, bsz):
    _fill_both(u_ref, cf_ref, lf_ref, cb_ref, lb_ref, rev_ref, nctx, bsz)
    nrow = u_ref.shape[0] // B_HD
    vrows = B_HD // 2
    for vi in range(vrows):
        halves = [u_ref[pl.ds(half * vrows + vi, nrow, stride=B_HD), :] for half in range(2)]
        o_ref[pl.ds(vi, TIME_BLOCK, stride=vrows), :] = jnp.concatenate(halves, axis=0).T


def _to_scan(ctx_f, lat_f, ctx_b, lat_b, rev, values):
    bsz, nc, width = ctx_f.shape
    nl = lat_f.shape[1]
    nctx, nlat = nc // TIME_BLOCK, nl // TIME_BLOCK
    lanes = 4 * bsz * B_HEADS
    t = nc + nl
    maps = _scan_block_maps(nctx, nlat)
    src = lambda f: pl.BlockSpec((bsz, TIME_BLOCK, width), lambda i, f=f: (0, f(i), 0))
    if values:
        kern, out_shape = _to_scan_v_kernel, (t * (B_HD // 2), lanes)
        out_spec = pl.BlockSpec((TIME_BLOCK * (B_HD // 2), lanes), lambda i: (i, 0))
    else:
        kern, out_shape = _to_scan_k_kernel, (B_HD, t, lanes)
        out_spec = pl.BlockSpec((B_HD, TIME_BLOCK, lanes), lambda i: (0, i, 0))
    return pl.pallas_call(
        functools.partial(kern, nctx=nctx, bsz=bsz),
        out_shape=jax.ShapeDtypeStruct(out_shape, F32),
        grid=(nctx + nlat,),
        in_specs=[src(maps[0]), src(maps[1]), src(maps[2]), src(maps[3]),
                  pl.BlockSpec((TIME_BLOCK, TIME_BLOCK), lambda i: (0, 0))],
        out_specs=out_spec,
        scratch_shapes=[pltpu.VMEM((2 * bsz * HEAD_PAIRS * LANES, TIME_BLOCK), F32)],
        compiler_params=_cparams(("parallel",)),
        name="to_scan_v" if values else "to_scan_k",
    )(ctx_f, lat_f, ctx_b, lat_b, rev)


def _from_scan_kernel(y_ref, rev_ref, o0_ref, o1_ref, u_ref, *, bsz):
    nrow = u_ref.shape[0] // B_HD
    vrows = B_HD // 2
    for vi in range(vrows):
        mt = y_ref[pl.ds(vi, TIME_BLOCK, stride=vrows), :].T
        for half in range(2):
            u_ref[pl.ds(half * vrows + vi, nrow, stride=B_HD), :] = mt[half * nrow:(half + 1) * nrow]
    for direction, o_ref in enumerate((o0_ref, o1_ref)):
        for b in range(bsz):
            for j in range(HEAD_PAIRS):
                q = (direction * bsz + b) * HEAD_PAIRS + j
                tile = u_ref[q * LANES:(q + 1) * LANES, :].T
                if direction == 1:
                    tile = _permute_rows(rev_ref[...], tile)
                o_ref[b, :, j * LANES:(j + 1) * LANES] = tile


def _from_scan(y, rev, bsz, nc, nl):
    nctx, nlat = nc // TIME_BLOCK, nl // TIME_BLOCK
    lanes = y.shape[1]
    vrows = B_HD // 2
    bwd = lambda i: jnp.where(i < nctx, nctx - 1 - i, 2 * nctx + nlat - 1 - i)
    out = jax.ShapeDtypeStruct((bsz, nc + nl, B_WIDTH), F32)
    return pl.pallas_call(
        functools.partial(_from_scan_kernel, bsz=bsz),
        out_shape=(out, out),
        grid=(nctx + nlat,),
        in_specs=[pl.BlockSpec((TIME_BLOCK * vrows, lanes), lambda i: (i, 0)),
                  pl.BlockSpec((TIME_BLOCK, TIME_BLOCK), lambda i: (0, 0))],
        out_specs=[pl.BlockSpec((bsz, TIME_BLOCK, B_WIDTH), lambda i: (0, i, 0)),
                   pl.BlockSpec((bsz, TIME_BLOCK, B_WIDTH), lambda i: (0, bwd(i), 0))],
        scratch_shapes=[pltpu.VMEM((2 * bsz * HEAD_PAIRS * LANES, TIME_BLOCK), F32)],
        compiler_params=_cparams(("parallel",)),
        name="from_scan",
    )(y, rev)


def _rwkv_bidirectional(prep_c, prep_l):
    bsz, nc, _ = prep_c['v'].shape
    nl = prep_l['v'].shape[1]
    rev = jnp.asarray(np.eye(TIME_BLOCK, dtype=np.float32)[::-1].copy(), BF16)
    pair = lambda fwd, bwd, values=False: _to_scan(prep_c[fwd], prep_l[fwd], prep_c[bwd], prep_l[bwd], rev, values)
    y = _rwkv_scan(pair('gam0', 'gam1'), pair('b0', 'b1'), pair('kd0', 'kd1'), pair('r0', 'r1'), pair('kk', 'kk'),
                   pair('v', 'v', True))
    return _from_scan(y, rev, bsz, nc, nl)


def _merge_kernel(a_ref, o0_ref, o1_ref, bonus_ref, g_ref, cv_ref, cprev_ref, cnext_ref, ga_ref, gb_ref, gc_ref,
                  x_ref, lng_ref, lnb_ref, cw_ref, wb_ref, wo_ref, ones_ref, g1_ref, n2_ref, sc_ref, sh_ref,
                  x_o, hf_o, hb_o):
    gmean = lambda t: _group_sum(t, ones_ref[...]) * (1.0 / B_HD)

    def group_norm(o):
        dlt = o - gmean(o)
        return dlt * lax.rsqrt(gmean(dlt * dlt) + GN_EPS) * lng_ref[...] + lnb_ref[...]

    yb = (group_norm(o0_ref[0]) + group_norm(o1_ref[0]) + bonus_ref[0]) * g_ref[0]

    w = C_WIDTH
    u = lambda blk: blk[:, w:2 * w] * blk[:, 2 * w:3 * w]
    cv = cv_ref[0].astype(F32)
    i = pl.program_id(1)
    pr = u(cprev_ref[0].astype(F32)[HALO_ROWS - 1:HALO_ROWS])
    nr = u(cnext_ref[0].astype(F32)[0:1])
    pr = jnp.where(i == 0, jnp.zeros_like(pr), pr)
    nr = jnp.where(i == pl.num_programs(1) - 1, jnp.zeros_like(nr), nr)
    yc = cv[:, 0:w] * _dwconv3(u(cv), pr, nr, cw_ref)

    d = D_MODEL
    m = jnp.zeros((cv.shape[0], d), F32)
    for j, (yj, gate_ref) in enumerate(((a_ref[0], ga_ref), (yb, gb_ref), (yc, gc_ref))):
        gate = jax.nn.sigmoid(gate_ref[0].astype(F32))
        m = m + gate * jnp.dot(yj.astype(BF16), wb_ref[j], preferred_element_type=F32)
    ymix = jnp.dot(m.astype(BF16), wo_ref[...], preferred_element_type=F32)
    x1 = x_ref[0] + g1_ref[0] * ymix
    x_o[0] = x1
    ms = jnp.mean(x1 * x1, axis=-1, keepdims=True)
    h = x1 * lax.rsqrt(ms + NORM_EPS) * n2_ref[...] * (1.0 + sc_ref[0]) + sh_ref[0]
    hf_o[0] = h
    hb_o[0] = h.astype(BF16)


def _merge(a, o0, o1, row_off, bonus, g, p, x, lw, g1, sc2, sh2):
    b, n, d = x.shape
    tm = _pick(n, 256)
    w = C_WIDTH
    assert row_off % tm == 0
    tile = lambda width: pl.BlockSpec((1, tm, width), lambda bi, i: (bi, i, 0))
    scan_tile = pl.BlockSpec((1, tm, w), lambda bi, i: (bi, i + row_off // tm, 0))
    const = lambda shape: pl.BlockSpec(shape, lambda bi, i: (0,) * len(shape))
    per_b = pl.BlockSpec((1, 1, d), lambda bi, i: (bi, 0, 0))
    gate = lambda j: pl.BlockSpec((1, tm, d), lambda bi, i, j=j: (bi, i, COL_GATES // d + j))
    in_specs = ([tile(w), scan_tile, scan_tile, tile(w), tile(w)]
                + _halo_specs(tm, 3 * w, COL_CONV // (3 * w), n) + [gate(0), gate(1), gate(2)]
                + [tile(d), const((1, w)), const((1, w)), const((3, w)), const((N_BRANCH, w, d)), const((d, d)),
                   const((w, w)), per_b, const((1, d)), per_b, per_b])
    return pl.pallas_call(
        _merge_kernel,
        out_shape=(jax.ShapeDtypeStruct((b, n, d), F32), jax.ShapeDtypeStruct((b, n, d), F32),
                   jax.ShapeDtypeStruct((b, n, d), BF16)),
        grid=(b, n // tm),
        in_specs=in_specs,
        out_specs=[tile(d)] * 3,
        compiler_params=_cparams(("parallel", "parallel")),
        name="merge_mixers",
    )(a, o0, o1, bonus, g, p, p, p, p, p, p, x, lw['rwkv_ln_g'], lw['rwkv_ln_b'], lw['conv_w'], lw['w_branch_bf16'],
      lw['w_out_bf16'], lw['ones64'], g1, lw['norm2_g'], sc2, sh2)


def _router_kernel(h_ref, rw_ref, o_ref):
    logits = lax.dot_general(rw_ref[...], h_ref[0], (((1,), (1,)), ((), ())), precision=HIGHEST,
                             preferred_element_type=F32)
    z = jnp.exp(logits - jnp.max(logits, axis=0, keepdims=True))
    o_ref[0] = z / jnp.sum(z, axis=0, keepdims=True)


def _router(h, rw_t):
    b, n, d = h.shape
    tm = _pick(n, 512)
    return pl.pallas_call(
        _router_kernel,
        out_shape=jax.ShapeDtypeStruct((b, N_EXPERTS, n), F32),
        grid=(b, n // tm),
        in_specs=[pl.BlockSpec((1, tm, d), lambda bi, i: (bi, i, 0)), pl.BlockSpec((N_EXPERTS, d), lambda bi, i: (0, 0))],
        out_specs=pl.BlockSpec((1, N_EXPERTS, tm), lambda bi, i: (bi, 0, i)),
        compiler_params=_cparams(("parallel", "parallel")),
        name="router_affinity",
    )(h, rw_t)


def _select_kernel(aff_ref, tri_ref, pos_o, cnt_o, *, cap, lb):
    aff = aff_ref[0]
    n = aff.shape[1]
    bits = pltpu.bitcast(aff, jnp.int32)
    count = lambda mask: jnp.sum(mask.astype(F32), axis=1, keepdims=True)

    def refine(i, lo):
        cand = lo | jnp.left_shift(jnp.int32(1), 30 - i)
        return jnp.where(count(bits >= cand) >= cap, cand, lo)

    thr = lax.fori_loop(0, 31, refine, jnp.zeros((N_EXPERTS, 1), jnp.int32))
    gt = bits > thr
    eq = bits == thr
    need = cap - count(gt)
    tri = tri_ref[...]

    def prefix(mask_of_block, write):
        off = jnp.zeros((N_EXPERTS, 1), F32)
        for j in range(n // lb):
            m = mask_of_block(j)
            excl = jnp.dot(m.astype(BF16), tri, preferred_element_type=F32) + off
            write(j, m, excl)
            off = off + jnp.sum(m.astype(F32), axis=1, keepdims=True)

    blk = lambda x, j: x[:, j * lb:(j + 1) * lb]
    sel_blocks = [None] * (n // lb)

    def write_sel(j, m, excl):
        sel_blocks[j] = blk(gt, j) | (m & (excl < need))

    prefix(lambda j: blk(eq, j), write_sel)

    def write_pos(j, m, excl):
        c = excl.astype(jnp.int32)
        cnt_o[0, :, j * lb:(j + 1) * lb] = c
        pos_o[0, :, j * lb:(j + 1) * lb] = jnp.where(m, c, -1)

    prefix(lambda j: sel_blocks[j], write_pos)


def _select(aff, cap):
    b, e, n = aff.shape
    lb = min(n, LANES)
    tri = jnp.asarray(np.triu(np.ones((lb, lb), np.float32), 1), BF16)
    blk = pl.BlockSpec((1, e, n), lambda bi: (bi, 0, 0))
    out = jax.ShapeDtypeStruct((b, e, n), jnp.int32)
    return pl.pallas_call(
        functools.partial(_select_kernel, cap=cap, lb=lb),
        out_shape=(out, out),
        grid=(b,),
        in_specs=[blk, pl.BlockSpec((lb, lb), lambda bi: (0, 0))],
        out_specs=[blk, blk],
        compiler_params=_cparams(("parallel",)),
        name="expert_select",
    )(aff, tri)


GATHER_CHUNK = 128
FFN_ROWS = 512


def _expert_kernel(starts_ref, pos_ref, aff_ref, h_ref, w1_ref, w3_ref, w2_ref, y_o, xin_s, gate_s, w1_s, w3_s, w2_s,
                   *, cap, tile, ntile, tpg, nchunk):
    e, bi, g = pl.program_id(0), pl.program_id(1), pl.program_id(2)

    @pl.when((bi == 0) & (g == 0))
    def _():
        w1_s[...] = w1_ref[0, 0].astype(BF16)
        w3_s[...] = w3_ref[0, 0].astype(BF16)
        w2_s[...] = w2_ref[0, 0].astype(BF16)

    @pl.when(g == 0)
    def _():
        xin_s[...] = jnp.zeros(xin_s.shape, F32)
        gate_s[...] = jnp.zeros(gate_s.shape, F32)

    for tt in range(tpg):
        base = (e * pl.num_programs(1) + bi) * (ntile + 1) + g * tpg + tt
        start, end = starts_ref[base], starts_ref[base + 1]
        w0 = (start // SUBLANES) * SUBLANES
        cols = slice(tt * tile, (tt + 1) * tile)
        for c in range(nchunk):
            r0 = pl.multiple_of(w0 + c * GATHER_CHUNK, SUBLANES)

            @pl.when((r0 < end) & (r0 + GATHER_CHUNK > start))
            def _():
                slot = r0 + lax.broadcasted_iota(jnp.int32, (GATHER_CHUNK, tile), 0)
                onehot = pos_ref[0, 0][:, cols] == slot
                rows = pl.ds(r0, GATHER_CHUNK)
                xin_s[rows, :] += jnp.dot(onehot.astype(BF16), h_ref[0, cols, :], preferred_element_type=F32)
                gate = jnp.sum(jnp.where(onehot, aff_ref[0, 0][:, cols], 0.0), axis=1, keepdims=True)
                gate_s[rows, :] += jnp.broadcast_to(gate, (GATHER_CHUNK, gate_s.shape[1]))

    @pl.when(g == pl.num_programs(2) - 1)
    def _():
        rc = min(cap, FFN_ROWS)
        capp = y_o.shape[2]
        if capp > cap:
            y_o[0, 0, cap:capp, :] = jnp.zeros((capp - cap, y_o.shape[3]), BF16)
        for c in range(cap // rc):
            x = xin_s[c * rc:(c + 1) * rc, :].astype(BF16)
            hid = (_silu(jnp.dot(x, w1_s[...], preferred_element_type=F32))
                   * jnp.dot(x, w3_s[...], preferred_element_type=F32))
            y = jnp.dot(hid.astype(BF16), w2_s[...], preferred_element_type=F32)
            y_o[0, 0, c * rc:(c + 1) * rc, :] = (y * gate_s[c * rc:(c + 1) * rc, 0:1]).astype(BF16)


def _moe_tile(n):
    return _pick(n, 256)


def _experts(starts, pos4, aff4, h_bf16, w1, w3, w2, layer, cap):
    b, n, d = h_bf16.shape
    tile = _moe_tile(n)
    ntile = n // tile
    nchunk = tile // GATHER_CHUNK + 1
    rows = cap + nchunk * GATHER_CHUNK
    capp = -(-cap // SLOT_BLOCK) * SLOT_BLOCK
    f = w1.shape[3]
    tg = _pick(n, 2048)
    tpg = tg // tile
    grid_spec = pltpu.PrefetchScalarGridSpec(
        num_scalar_prefetch=1,
        grid=(N_EXPERTS, b, n // tg),
        in_specs=[pl.BlockSpec((1, 1, 1, tg), lambda e, bi, g, s: (bi, e, 0, g)),
                  pl.BlockSpec((1, 1, 1, tg), lambda e, bi, g, s: (bi, e, 0, g)),
                  pl.BlockSpec((1, tg, d), lambda e, bi, g, s: (bi, g, 0)),
                  pl.BlockSpec((1, 1, d, f), lambda e, bi, g, s: (layer, e, 0, 0), pipeline_mode=pl.Buffered(1)),
                  pl.BlockSpec((1, 1, d, f), lambda e, bi, g, s: (layer, e, 0, 0), pipeline_mode=pl.Buffered(1)),
                  pl.BlockSpec((1, 1, f, d), lambda e, bi, g, s: (layer, e, 0, 0), pipeline_mode=pl.Buffered(1))],
        out_specs=pl.BlockSpec((1, 1, capp, d), lambda e, bi, g, s: (e, bi, 0, 0)),
        scratch_shapes=[pltpu.VMEM((rows, d), F32), pltpu.VMEM((rows, LANES), F32),
                        pltpu.VMEM((d, f), BF16), pltpu.VMEM((d, f), BF16), pltpu.VMEM((f, d), BF16)])
    return pl.pallas_call(
        functools.partial(_expert_kernel, cap=cap, tile=tile, ntile=ntile, tpg=tpg, nchunk=nchunk),
        out_shape=jax.ShapeDtypeStruct((N_EXPERTS, b, capp, d), BF16),
        grid_spec=grid_spec,
        compiler_params=_cparams(("arbitrary", "arbitrary", "arbitrary")),
        name="expert_gather_ffn",
    )(starts, pos4, aff4, h_bf16, w1, w3, w2)


def _combine_kernel(starts_ref, pos_ref, *refs, tile, tpg, ntile, sb, nblk):
    y_refs, (x_ref, g2_ref, o_ref) = refs[:nblk], refs[nblk:]
    bi, gi, e = pl.program_id(0), pl.program_id(1), pl.program_id(2)

    @pl.when(e == 0)
    def _():
        o_ref[...] = jnp.zeros(o_ref.shape, F32)

    base = (e * pl.num_programs(0) + bi) * (ntile + 1) + gi * tpg
    start, end = starts_ref[base], starts_ref[base + tpg]
    k0 = start // sb
    lane = lax.broadcasted_iota(jnp.int32, (1, LANES), 1)
    pos_t = jnp.concatenate(
        [jnp.broadcast_to(pos_ref[0, 0][:, tt * tile:(tt + 1) * tile].astype(F32), (LANES, tile)).T
         for tt in range(tpg)], axis=0)
    for j, y_ref in enumerate(y_refs):
        s0 = (k0 + j) * sb

        @pl.when((s0 < end) & (end > start))
        def _():
            blocks = [pos_t == (s0 + jb * LANES + lane).astype(F32) for jb in range(sb // LANES)]
            onehot_t = jnp.concatenate(blocks, axis=1).astype(BF16)
            o_ref[0] += jnp.dot(onehot_t, y_ref[0, 0], preferred_element_type=F32)

    @pl.when(e == pl.num_programs(2) - 1)
    def _():
        o_ref[0] = x_ref[0] + g2_ref[0] * o_ref[0]


def _combine(starts, pos4, y, x, g2):
    b, n, d = x.shape
    tile = _moe_tile(n)
    ntile = n // tile
    gt = _pick(n, 1024)
    tpg = gt // tile
    sb = SLOT_BLOCK
    nsb = y.shape[2] // sb
    nblk = min(gt // sb + 1, nsb)

    def yblk(j):
        def imap(bi, gi, e, s):
            k0 = s[(e * b + bi) * (ntile + 1) + gi * tpg] // sb
            return (e, bi, jnp.minimum(k0 + j, nsb - 1), 0)
        return pl.BlockSpec((1, 1, sb, d), imap)

    grid_spec = pltpu.PrefetchScalarGridSpec(
        num_scalar_prefetch=1,
        grid=(b, n // gt, N_EXPERTS),
        in_specs=[pl.BlockSpec((1, 1, 1, gt), lambda bi, gi, e, s: (bi, e, 0, gi))]
                 + [yblk(j) for j in range(nblk)]
                 + [pl.BlockSpec((1, gt, d), lambda bi, gi, e, s: (bi, gi, 0)),
                    pl.BlockSpec((1, 1, d), lambda bi, gi, e, s: (bi, 0, 0))],
        out_specs=pl.BlockSpec((1, gt, d), lambda bi, gi, e, s: (bi, gi, 0)),
        scratch_shapes=[])
    return pl.pallas_call(
        functools.partial(_combine_kernel, tile=tile, tpg=tpg, ntile=ntile, sb=sb, nblk=nblk),
        out_shape=jax.ShapeDtypeStruct((b, n, d), F32),
        grid_spec=grid_spec,
        compiler_params=_cparams(("parallel", "parallel", "arbitrary")),
        name="expert_combine",
    )(starts, pos4, *([y] * nblk), x, g2)


def _moe(h_f32, h_bf16, x, g2, lw):
    b, n, _ = x.shape
    cap = CAPACITY_FACTOR * n // N_EXPERTS
    aff = _router(h_f32, lw['router_w_t'])
    pos, cnt = _select(aff, cap)
    tile = _moe_tile(n)
    starts = jnp.concatenate([cnt[:, :, ::tile], jnp.full((b, N_EXPERTS, 1), cap, jnp.int32)], axis=2)
    starts = jnp.transpose(starts, (1, 0, 2)).reshape(-1)
    pos4 = pos.reshape(b, N_EXPERTS, 1, n)
    aff4 = aff.reshape(b, N_EXPERTS, 1, n)
    y = _experts(starts, pos4, aff4, h_bf16, lw['exp_w1'], lw['exp_w3'], lw['exp_w2'], lw['layer'], cap)
    return _combine(starts, pos4, y, x, g2)


def _pad_w_in(w_in):
    d = w_in.shape[0]
    sizes = (A_WIDTH,) * 3 + (B_WIDTH,) * 3 + (B_DECAY_RANK, B_ICL_RANK, B_GATE_RANK) + (C_WIDTH,) * 3 + (N_BRANCH * D_MODEL,)
    cuts = np.cumsum((0,) + sizes)
    piece = lambda lo, hi: w_in[:, int(cuts[lo]):int(cuts[hi])]
    zeros = jnp.zeros((d, COL_GATES - COL_LOWRANK - 256), w_in.dtype)
    out = jnp.concatenate([piece(0, 3), piece(3, 6), piece(9, 12), piece(6, 9), zeros, piece(12, 13)], axis=1)
    assert out.shape[1] == N_IN_PAD
    return out.astype(BF16)


def _rows_at(w, lo, total=256):
    pad = [(0, 0)] * w.ndim
    pad[-2] = (lo, total - lo - w.shape[-2])
    return jnp.pad(w, pad)


def _rope_tables(rows):
    half = A_HD // 2
    inv = jnp.power(ROPE_THETA, -jnp.arange(0, half, 2, dtype=F32) / half)
    r = jnp.repeat(jnp.arange(rows, dtype=F32), GRID_W)
    col = jnp.tile(jnp.arange(GRID_W, dtype=F32), rows)
    ang = jnp.concatenate([r[:, None] * inv, col[:, None] * inv], axis=-1)
    cos, sin = jnp.cos(ang), jnp.sin(ang)
    reps = A_WIDTH // A_HD
    cos_t = jnp.tile(jnp.concatenate([cos, cos], axis=-1), (1, reps))
    sin_t = jnp.tile(jnp.concatenate([-sin, sin], axis=-1), (1, reps))
    return cos_t, sin_t


def _layer_weights(i, norm2_g, w_in, q_norm_g, k_norm_g, diff_lambda, diff_subln_g, rwkv_conv_w, rwkv_w0, rwkv_w_up,
                   rwkv_a0, rwkv_a_up, rwkv_g_up, rwkv_k_k, rwkv_k_a, rwkv_r_k, rwkv_ln_g, rwkv_ln_b, conv_w,
                   w_branch, w_out, router_w, exp_w1, exp_w3, exp_w2):
    row = lambda v: v.reshape(1, -1)
    return {
        'norm2_g': row(norm2_g[i]),
        'w_in_pad': _pad_w_in(w_in[i]),
        'q_g': row(jnp.tile(q_norm_g[i], A_WIDTH // A_HD)), 'k_g': row(jnp.tile(k_norm_g[i], A_WIDTH // A_HD)),
        'diff_lambda': diff_lambda[i], 'subln_g': row(diff_subln_g[i]),
        'rwkv_conv_w': rwkv_conv_w[i], 'rwkv_w0': rwkv_w0[i], 'rwkv_a0': rwkv_a0[i],
        'w_up_pad': _rows_at(rwkv_w_up[i], 0), 'a_up_pad': _rows_at(rwkv_a_up[i], B_DECAY_RANK),
        'g_up_pad': _rows_at(rwkv_g_up[i], B_DECAY_RANK + B_ICL_RANK),
        'rwkv_k_k': row(rwkv_k_k[i]), 'rwkv_k_a': row(rwkv_k_a[i]), 'rwkv_r_k': row(rwkv_r_k[i]),
        'rwkv_ln_g': row(rwkv_ln_g[i]), 'rwkv_ln_b': row(rwkv_ln_b[i]),
        'conv_w': conv_w[i], 'w_branch_bf16': w_branch[i].astype(BF16), 'w_out_bf16': w_out[i].astype(BF16),
        'router_w_t': router_w[i].T,
        'exp_w1': exp_w1, 'exp_w3': exp_w3, 'exp_w2': exp_w2, 'layer': i,
        'ones64': _group_ones(B_WIDTH, B_HD).astype(BF16),
    }


def _mixer_inputs(x, norm1_g, sc1, sh1, lw, cos_t, sin_t):
    p = _norm_mod_proj(x, norm1_g, sc1, sh1, lw['w_in_pad'])
    q, k = _qkv_prep(p, lw['q_g'], lw['k_g'], lw['ones64'], cos_t, sin_t)
    prep = dict(zip(PREP_NAMES, _rwkv_prep(p, lw)))
    return p, (q, k), prep


def kernel(x, c, ctx, c_ctx, norm1_g, norm2_g, ada_w, ada_b, w_in, q_norm_g, k_norm_g, diff_lambda, diff_subln_g,
           rwkv_conv_w, rwkv_w0, rwkv_w_up, rwkv_a0, rwkv_a_up, rwkv_g_up, rwkv_k_k, rwkv_k_a, rwkv_r_k, rwkv_ln_g,
           rwkv_ln_b, conv_w, w_branch, w_out, router_w, exp_w1, exp_w3, exp_w2):
    bsz, n_lat, d = x.shape
    depth = ada_w.shape[0]
    cos_t, sin_t = _rope_tables(n_lat // GRID_W)

    cond = jnp.concatenate([c, jnp.broadcast_to(c_ctx[None, :], (SUBLANES - bsz % SUBLANES, d))], axis=0)
    ada = _ada_params(cond, ada_w, ada_b)

    xc, xl = ctx, x
    for i in range(depth):
        lam_init = 0.8 - 0.6 * math.exp(-0.3 * i)
        need_ctx = i != depth - 1
        lw = _layer_weights(i, norm2_g, w_in, q_norm_g, k_norm_g, diff_lambda, diff_subln_g, rwkv_conv_w, rwkv_w0,
                            rwkv_w_up, rwkv_a0, rwkv_a_up, rwkv_g_up, rwkv_k_k, rwkv_k_a, rwkv_r_k, rwkv_ln_g,
                            rwkv_ln_b, conv_w, w_branch, w_out, router_w, exp_w1, exp_w3, exp_w2)
        mods = ada[i].reshape(-1, 6, d)
        lat = [mods[:bsz, j][:, None, :] for j in range(6)]
        cxt = [jnp.broadcast_to(mods[bsz, j][None, None, :], (bsz, 1, d)) for j in range(6)]

        pc, (qc, kc), prep_c = _mixer_inputs(xc, norm1_g[i], cxt[1], cxt[0], lw, None, None)
        pl_, (ql, kl), prep_l = _mixer_inputs(xl, norm1_g[i], lat[1], lat[0], lw, cos_t, sin_t)

        v_cols = slice(COL_ATTN + 2 * A_WIDTH, COL_ATTN + 3 * A_WIDTH)
        k_all = jnp.concatenate([kc, kl], axis=1)
        v_all = jnp.concatenate([pc[:, :, v_cols], pl_[:, :, v_cols]], axis=1)
        al = _flash_attention(lw['diff_lambda'], lw['subln_g'], ql, k_all, v_all, lam_init)
        o0, o1 = _rwkv_bidirectional(prep_c, prep_l)
        n_ctx = xc.shape[1]

        xl, hl, hlb = _merge(al, o0, o1, n_ctx, prep_l['bonus'], prep_l['g'], pl_, xl, lw, lat[2], lat[4], lat[3])
        xl = _moe(hl, hlb, xl, lat[5], lw)
        if need_ctx:
            ac = _ctx_attention(lw['diff_lambda'], lw['subln_g'], qc, kc, pc, lam_init)
            xc, hc, hcb = _merge(ac, o0, o1, 0, prep_c['bonus'], prep_c['g'], pc, xc, lw, cxt[2], cxt[4], cxt[3])
            xc = _moe(hc, hcb, xc, cxt[5], lw)
    return xl
```

```python
import functools
import math

import numpy as np
import jax
import jax.numpy as jnp
from jax import lax
from jax.experimental import pallas as pl
from jax.experimental.pallas import tpu as pltpu

F32 = jnp.float32
BF16 = jnp.bfloat16
HIGHEST = lax.Precision.HIGHEST

D_MODEL = 1024
GRID_W = 64
A_HEADS = 4
A_HD = 64
A_WIDTH = A_HEADS * 2 * A_HD
B_HEADS = 8
B_HD = 64
B_WIDTH = B_HEADS * B_HD
B_DECAY_RANK = 64
B_ICL_RANK = 64
B_GATE_RANK = 128
C_WIDTH = 512
N_BRANCH = 3
N_EXPERTS = 16
EXPERT_FF = 1024
CAPACITY_FACTOR = 2
ROPE_THETA = 10000.0
NORM_EPS = 1e-6
GN_EPS = 64e-5
LOG2_E = math.log2(math.e)

COL_ATTN = 0
COL_RWKV = 1536
COL_CONV = 3072
COL_LOWRANK = 4608
COL_GATES = 5120
N_IN_PAD = 8192

VMEM_LIMIT = 56 * 1024 * 1024
LANES = 128
SUBLANES = 8
SLOT_BLOCK = 256


def _cparams(sem):
    return pltpu.CompilerParams(dimension_semantics=sem, vmem_limit_bytes=VMEM_LIMIT)


def _pick(n, pref):
    t = min(n, pref)
    while n % t:
        t -= SUBLANES
    return t


def _group_ones(width, group):
    idx = np.arange(width) // group
    return jnp.asarray((idx[:, None] == idx[None, :]).astype(np.float32))


def _silu(x):
    return x * jax.nn.sigmoid(x)


def _split2(x):
    hi = x.astype(BF16)
    return hi, (x - hi.astype(F32)).astype(BF16)


def _group_sum(x, ones_bf16):
    hi, lo = _split2(x)
    return (jnp.dot(hi, ones_bf16, preferred_element_type=F32) + jnp.dot(lo, ones_bf16, preferred_element_type=F32))


def _permute_rows(perm_bf16, x):
    hi, lo = _split2(x)
    rest = x - hi.astype(F32) - lo.astype(F32)
    dot = lambda v: jnp.dot(perm_bf16, v, preferred_element_type=F32)
    return dot(hi) + dot(lo) + dot(rest.astype(BF16))


def _dot3(a, b):
    a_hi, a_lo = _split2(a)
    b_hi, b_lo = _split2(b)
    dot = lambda u, v: jnp.dot(u, v, preferred_element_type=F32)
    return dot(a_hi, b_hi) + (dot(a_hi, b_lo) + dot(a_lo, b_hi))


def _ada_kernel(cond_ref, w_ref, b_ref, o_ref):
    c = cond_ref[...]
    o_ref[0] = jnp.dot(_silu(c), w_ref[0], precision=HIGHEST, preferred_element_type=F32) + b_ref[0]


def _ada_params(cond, ada_w, ada_b):
    depth, d, n6 = ada_w.shape
    tn = _pick(n6, 1536)
    return pl.pallas_call(
        _ada_kernel,
        out_shape=jax.ShapeDtypeStruct((depth, cond.shape[0], n6), F32),
        grid=(depth, n6 // tn),
        in_specs=[pl.BlockSpec(cond.shape, lambda i, j: (0, 0)),
                  pl.BlockSpec((1, d, tn), lambda i, j: (i, 0, j)),
                  pl.BlockSpec((1, 1, tn), lambda i, j: (i, 0, j))],
        out_specs=pl.BlockSpec((1, cond.shape[0], tn), lambda i, j: (i, 0, j)),
        compiler_params=_cparams(("parallel", "parallel")),
        name="ada_params",
    )(cond, ada_w, ada_b.reshape(depth, 1, n6))


def _proj_kernel(x_ref, g_ref, sc_ref, sh_ref, w_ref, o_ref):
    x = x_ref[0]
    ms = jnp.mean(x * x, axis=-1, keepdims=True)
    y = x * lax.rsqrt(ms + NORM_EPS) * g_ref[...]
    h = y * (1.0 + sc_ref[0]) + sh_ref[0]
    o_ref[0] = jnp.dot(h.astype(BF16), w_ref[...], preferred_element_type=F32).astype(BF16)


def _norm_mod_proj(x, g, sc, sh, w_bf16):
    b, n, d = x.shape
    npad = w_bf16.shape[1]
    tm = _pick(n, 512)
    tn = 4096
    return pl.pallas_call(
        _proj_kernel,
        out_shape=jax.ShapeDtypeStruct((b, n, npad), BF16),
        grid=(npad // tn, b, n // tm),
        in_specs=[pl.BlockSpec((1, tm, d), lambda j, bi, i: (bi, i, 0)),
                  pl.BlockSpec((1, d), lambda j, bi, i: (0, 0)),
                  pl.BlockSpec((1, 1, d), lambda j, bi, i: (bi, 0, 0)),
                  pl.BlockSpec((1, 1, d), lambda j, bi, i: (bi, 0, 0)),
                  pl.BlockSpec((d, tn), lambda j, bi, i: (0, j))],
        out_specs=pl.BlockSpec((1, tm, tn), lambda j, bi, i: (bi, i, j)),
        compiler_params=_cparams(("parallel", "parallel", "parallel")),
        name="norm_mod_proj",
    )(x, g.reshape(1, d), sc, sh, w_bf16)


def _swap_halves(x, half):
    width = x.shape[-1]
    lane = lax.broadcasted_iota(jnp.int32, (1, width), 1) % (2 * half)
    fwd = pltpu.roll(x, width - half, axis=1)
    bwd = pltpu.roll(x, half, axis=1)
    return jnp.where(lane < half, fwd, bwd)


def _qkv_prep_kernel(*refs, use_rope):
    if use_rope:
        q_ref, k_ref, qg_ref, kg_ref, ones_ref, cos_ref, sin_ref, qo_ref, ko_ref = refs
    else:
        q_ref, k_ref, qg_ref, kg_ref, ones_ref, qo_ref, ko_ref = refs

    def head_norm(x, gain):
        ms = _group_sum(x * x, ones_ref[...]) * (1.0 / A_HD)
        return x * lax.rsqrt(ms + NORM_EPS) * gain

    def rope(x):
        if not use_rope:
            return x
        return x * cos_ref[...] + _swap_halves(x, A_HD // 2) * sin_ref[...]

    q = rope(head_norm(q_ref[0].astype(F32), qg_ref[...]))
    k = rope(head_norm(k_ref[0].astype(F32), kg_ref[...]))
    qo_ref[0] = (q * (A_HD ** -0.5 * LOG2_E)).astype(BF16)
    ko_ref[0] = k.astype(BF16)


def _qkv_prep(p, qg, kg, ones, cos_t, sin_t):
    b, n, _ = p.shape
    tm = _pick(n, 512)
    use_rope = cos_t is not None
    w = A_WIDTH
    col = lambda c: pl.BlockSpec((1, tm, w), lambda bi, i, c=c: (bi, i, COL_ATTN // w + c))
    const = lambda shape: pl.BlockSpec(shape, lambda bi, i: (0,) * len(shape))
    in_specs = [col(0), col(1), const((1, w)), const((1, w)), const((w, w))]
    args = [p, p, qg, kg, ones]
    if use_rope:
        in_specs += [pl.BlockSpec((tm, w), lambda bi, i: (i, 0))] * 2
        args += [cos_t, sin_t]
    out = jax.ShapeDtypeStruct((b, n, w), BF16)
    return pl.pallas_call(
        functools.partial(_qkv_prep_kernel, use_rope=use_rope),
        out_shape=(out, out),
        grid=(b, n // tm),
        in_specs=in_specs,
        out_specs=[pl.BlockSpec((1, tm, w), lambda bi, i: (bi, i, 0))] * 2,
        compiler_params=_cparams(("parallel", "parallel")),
        name="qkv_prep_rope" if use_rope else "qkv_prep",
    )(*args)


def _map_stacked(q):
    lane = lax.broadcasted_iota(jnp.int32, (1, 2 * A_HD), 1)
    zero = jnp.zeros_like(q)
    return jnp.concatenate([jnp.where(lane < A_HD, q, zero), jnp.where(lane >= A_HD, q, zero)], axis=0)


def _lane_tiled(x, width):
    return jnp.concatenate([x] * (width // LANES), axis=1)


def _lane_block_sum(p):
    out = p[:, 0:LANES]
    for j in range(1, p.shape[1] // LANES):
        out = out + p[:, j * LANES:(j + 1) * LANES]
    return out


def _lambda_full(dl_ref, lam_init):
    dl = dl_ref[...]
    s1 = jnp.sum(dl[0:1] * dl[1:2], axis=-1, keepdims=True)
    s2 = jnp.sum(dl[2:3] * dl[3:4], axis=-1, keepdims=True)
    return jnp.exp(s1) - jnp.exp(s2) + lam_init


def _subln(o, sg_ref, lam_init):
    ms = jnp.mean(o * o, axis=-1, keepdims=True)
    return o * lax.rsqrt(ms + NORM_EPS) * sg_ref[...] * (1.0 - lam_init)


def _scores(qm, k):
    return lax.dot_general(qm, k, (((1,), (1,)), ((), ())), preferred_element_type=F32)


def _flash_kernel(dl_ref, sg_ref, q_ref, k_ref, v_ref, o_ref, m_s, l_s, acc_s, *, lam_init, nkv):
    kj = pl.program_id(3)
    tq = q_ref.shape[1]

    @pl.when(kj == 0)
    def _():
        m_s[...] = jnp.full(m_s.shape, -jnp.inf, F32)
        l_s[...] = jnp.zeros(l_s.shape, F32)
        acc_s[...] = jnp.zeros(acc_s.shape, F32)

    s = _scores(_map_stacked(q_ref[0]), k_ref[0])
    m_prev = m_s[...]
    m_new = jnp.maximum(m_prev, jnp.max(s, axis=-1, keepdims=True))
    alpha = jnp.exp2(m_prev - m_new)
    p = jnp.exp2(s - _lane_tiled(m_new, s.shape[1]))
    l_s[...] = alpha * l_s[...] + _lane_block_sum(p)
    acc_s[...] = alpha * acc_s[...] + jnp.dot(p.astype(BF16), v_ref[0], preferred_element_type=F32)
    m_s[...] = m_new

    @pl.when(kj == nkv - 1)
    def _():
        lam = _lambda_full(dl_ref, lam_init)
        o = acc_s[...] / jnp.sum(l_s[...], axis=-1, keepdims=True)
        o_ref[0] = _subln(o[0:tq] - lam * o[tq:2 * tq], sg_ref, lam_init).astype(o_ref.dtype)


V_COL_BLOCK = (COL_ATTN + 2 * A_WIDTH) // (2 * A_HD)


def _pick_lanes(n, pref):
    t = min(n, pref) // LANES * LANES
    while n % t:
        t -= LANES
    return t


def _flash_attention(dl, sg, q, k_all, v_all, lam_init):
    b, s, _ = q.shape
    t = k_all.shape[1]
    hw = 2 * A_HD
    tq = _pick(s, 1024)
    tk = _pick_lanes(t, 3072)
    nkv = t // tk
    return pl.pallas_call(
        functools.partial(_flash_kernel, lam_init=lam_init, nkv=nkv),
        out_shape=jax.ShapeDtypeStruct((b, s, A_WIDTH), BF16),
        grid=(b, A_HEADS, s // tq, nkv),
        in_specs=[pl.BlockSpec((4, A_HD), lambda bi, h, i, j: (0, 0)),
                  pl.BlockSpec((1, hw), lambda bi, h, i, j: (0, 0)),
                  pl.BlockSpec((1, tq, hw), lambda bi, h, i, j: (bi, i, h)),
                  pl.BlockSpec((1, tk, hw), lambda bi, h, i, j: (bi, j, h)),
                  pl.BlockSpec((1, tk, hw), lambda bi, h, i, j: (bi, j, h))],
        out_specs=pl.BlockSpec((1, tq, hw), lambda bi, h, i, j: (bi, i, h)),
        scratch_shapes=[pltpu.VMEM((2 * tq, LANES), F32), pltpu.VMEM((2 * tq, LANES), F32), pltpu.VMEM((2 * tq, hw), F32)],
        compiler_params=_cparams(("parallel", "parallel", "parallel", "arbitrary")),
        name="diff_flash_attention",
    )(dl, sg, q, k_all, v_all)


def _ctx_attn_kernel(dl_ref, sg_ref, q_ref, k_ref, v_ref, o_ref, *, lam_init):
    n = q_ref.shape[1]
    s = _scores(_map_stacked(q_ref[0]), k_ref[0])
    p = jnp.exp2(s - jnp.max(s, axis=-1, keepdims=True))
    o = jnp.dot(p.astype(BF16), v_ref[0], preferred_element_type=F32) / jnp.sum(p, axis=-1, keepdims=True)
    lam = _lambda_full(dl_ref, lam_init)
    o_ref[0] = _subln(o[0:n] - lam * o[n:2 * n], sg_ref, lam_init).astype(o_ref.dtype)


def _ctx_attention(dl, sg, q, k, v, lam_init):
    b, n, _ = q.shape
    hw = 2 * A_HD
    blk = pl.BlockSpec((1, n, hw), lambda bi, h: (bi, 0, h))
    vblk = pl.BlockSpec((1, n, hw), lambda bi, h: (bi, 0, V_COL_BLOCK + h))
    return pl.pallas_call(
        functools.partial(_ctx_attn_kernel, lam_init=lam_init),
        out_shape=jax.ShapeDtypeStruct((b, n, A_WIDTH), BF16),
        grid=(b, A_HEADS),
        in_specs=[pl.BlockSpec((4, A_HD), lambda bi, h: (0, 0)), pl.BlockSpec((1, hw), lambda bi, h: (0, 0)),
                  blk, blk, vblk],
        out_specs=blk,
        compiler_params=_cparams(("parallel", "parallel")),
        name="ctx_attention",
    )(dl, sg, q, k, v)


HALO_ROWS = 16


def _halo_specs(tm, width, colblk, n):
    rh = tm // HALO_ROWS
    last = n // HALO_ROWS - 1
    main = pl.BlockSpec((1, tm, width), lambda bi, i: (bi, i, colblk))
    prev = pl.BlockSpec((1, HALO_ROWS, width), lambda bi, i: (bi, jnp.maximum(i * rh - 1, 0), colblk))
    nxt = pl.BlockSpec((1, HALO_ROWS, width), lambda bi, i: (bi, jnp.minimum((i + 1) * rh, last), colblk))
    return [main, prev, nxt]


def _dwconv3(x, prev_row, next_row, w_ref):
    tm = x.shape[0]
    row = lax.broadcasted_iota(jnp.int32, (tm, 1), 0)
    xp = jnp.where(row == 0, prev_row, pltpu.roll(x, 1, axis=0))
    xn = jnp.where(row == tm - 1, next_row, pltpu.roll(x, tm - 1, axis=0))
    w = w_ref[...]
    return w[0:1] * xp + w[1:2] * x + w[2:3] * xn


def _edge_rows(prev_ref, next_ref):
    i = pl.program_id(1)
    n = pl.num_programs(1)
    pr = prev_ref[0].astype(F32)[HALO_ROWS - 1:HALO_ROWS]
    nr = next_ref[0].astype(F32)[0:1]
    pr = jnp.where(i == 0, jnp.zeros_like(pr), pr)
    nr = jnp.where(i == n - 1, jnp.zeros_like(nr), nr)
    return pr, nr


def _softplus(u):
    return jnp.maximum(u, 0.0) + jnp.log(1.0 + jnp.exp(-jnp.abs(u)))


SCAN_BLOCK = 64


def _block_triangles(tm):
    i = np.arange(tm)
    same = (i[:, None] // SCAN_BLOCK) == (i[None, :] // SCAN_BLOCK)
    return jnp.asarray(np.stack([same & (i[None, :] <= i[:, None]), same & (i[None, :] >= i[:, None])]), BF16)


def _rwkv_prep_kernel(rkv_ref, prev_ref, next_ref, lr_ref, cw_ref, w0_ref, wup_ref, a0_ref, aup_ref, gup_ref,
                      kk_ref_w, ka_ref, rk_ref, ones_ref, tri_ref,
                      v_o, kk_o, g_o, bonus_o, gam0_o, b0_o, kd0_o, r0_o, gam1_o, b1_o, kd1_o, r1_o):
    pr, nr = _edge_rows(prev_ref, next_ref)
    rkv = _dwconv3(rkv_ref[0].astype(F32), pr, nr, cw_ref)
    w = B_WIDTH
    r, k, v = rkv[:, 0:w], rkv[:, w:2 * w], rkv[:, 2 * w:3 * w]
    gsum = lambda t: _group_sum(t, ones_ref[...])
    kkr = k * kk_ref_w[...]
    nrm = jnp.maximum(jnp.sqrt(gsum(kkr * kkr)), 1e-12)
    kk = kkr / nrm
    lr = lr_ref[0].astype(F32)
    th = jnp.tanh(lr)
    sg = jax.nn.sigmoid(lr)
    v_o[0] = v
    kk_o[0] = kk
    g_o[0] = _dot3(sg, gup_ref[...])
    kd_sum = jnp.zeros_like(k)
    outs = ((gam0_o, b0_o, kd0_o, r0_o), (gam1_o, b1_o, kd1_o, r1_o))
    for d in range(2):
        z = w0_ref[d:d + 1] + _dot3(th, wup_ref[d])
        wlog = -_softplus(-z) - 0.5
        log_decay = -jnp.exp(wlog)
        log_gamma = _permute_rows(tri_ref[d], log_decay)
        gamma, inv_gamma = jnp.exp(log_gamma), jnp.exp(-log_gamma)
        a = jax.nn.sigmoid(a0_ref[d:d + 1] + _dot3(lr, aup_ref[d]))
        kd = k * (1.0 + (a - 1.0) * ka_ref[...])
        outs[d][0][0] = gamma
        outs[d][1][0] = kk * a * inv_gamma
        outs[d][2][0] = kd * inv_gamma
        outs[d][3][0] = r * gamma
        kd_sum = kd_sum + kd
    bonus_o[0] = gsum(r * kd_sum * rk_ref[...]) * v


def _rwkv_prep(p, lw):
    b, n, _ = p.shape
    tm = _pick(n, 256)
    w = B_WIDTH
    w3 = 3 * w
    const = lambda shape: pl.BlockSpec(shape, lambda bi, i: (0,) * len(shape))
    in_specs = _halo_specs(tm, w3, COL_RWKV // w3, n) + [
        pl.BlockSpec((1, tm, 256), lambda bi, i: (bi, i, COL_LOWRANK // 256)),
        const((3, w3)), const((2, w)), const((2, 256, w)), const((2, w)), const((2, 256, w)), const((256, w)),
        const((1, w)), const((1, w)), const((1, w)), const((w, w)), const((2, tm, tm))]
    assert tm % SCAN_BLOCK == 0
    out = jax.ShapeDtypeStruct((b, n, w), F32)
    return pl.pallas_call(
        _rwkv_prep_kernel,
        out_shape=(out,) * len(PREP_NAMES),
        grid=(b, n // tm),
        in_specs=in_specs,
        out_specs=[pl.BlockSpec((1, tm, w), lambda bi, i: (bi, i, 0))] * len(PREP_NAMES),
        compiler_params=_cparams(("parallel", "parallel")),
        name="rwkv_prep",
    )(p, p, p, p, lw['rwkv_conv_w'], lw['rwkv_w0'], lw['w_up_pad'], lw['rwkv_a0'], lw['a_up_pad'], lw['g_up_pad'],
      lw['rwkv_k_k'], lw['rwkv_k_a'], lw['rwkv_r_k'], lw['ones64'], _block_triangles(tm))


PREP_NAMES = ('v', 'kk', 'g', 'bonus', 'gam0', 'b0', 'kd0', 'r0', 'gam1', 'b1', 'kd1', 'r1')


V_SLABS = B_HD // 2 // SUBLANES


def _scan_kernel(gam_ref, b_ref, kd_ref, r_ref, kk_ref, kkx_ref, v_ref, y_ref, s_ref, sa_ref, *, tc):
    @pl.when(pl.program_id(0) == 0)
    def _():
        s_ref[...] = jnp.zeros(s_ref.shape, F32)
        sa_ref[...] = jnp.zeros(sa_ref.shape, F32)

    lanes = s_ref.shape[-1]
    vrows = B_HD // 2
    bcast = lambda row: jnp.broadcast_to(row, (SUBLANES, lanes))

    def step(t, last_of_block):
        base = pl.multiple_of(t * vrows, vrows)
        sa = [sa_ref[s] for s in range(V_SLABS)]
        vv = [v_ref[pl.ds(base + SUBLANES * s, SUBLANES), :] for s in range(V_SLABS)]
        y = [jnp.zeros((SUBLANES, lanes), F32) for _ in range(V_SLABS)]
        san = [jnp.zeros((SUBLANES, lanes), F32) for _ in range(V_SLABS)]
        for k in range(B_HD):
            row = lambda ref: bcast(ref[k, pl.ds(t, 1), :])
            gk, bk, kdk, rk = row(gam_ref), row(b_ref), row(kd_ref), row(r_ref)
            kk_next = bcast(kkx_ref[k, 0:1, :]) if last_of_block else bcast(kk_ref[k, pl.ds(t + 1, 1), :])
            kkn = kk_next * gk
            for s in range(V_SLABS):
                new = s_ref[s, k] - sa[s] * bk + vv[s] * kdk
                s_ref[s, k] = new * gk if last_of_block else new
                y[s] = y[s] + new * rk
                san[s] = san[s] + new * kkn
        for s in range(V_SLABS):
            y_ref[pl.ds(base + SUBLANES * s, SUBLANES), :] = y[s]
            sa_ref[s] = san[s]

    def body(t, carry):
        step(t, False)
        return carry

    lax.fori_loop(0, tc - 1, body, 0)
    step(tc - 1, True)


def _rwkv_scan(gam, bb, kd, r, kk, v):
    _, t, lanes = gam.shape
    tc = SCAN_BLOCK
    assert t % tc == 0
    vrows = B_HD // 2
    last8 = t // SUBLANES - 1
    kspec = pl.BlockSpec((B_HD, tc, lanes), lambda i: (0, i, 0))
    kxspec = pl.BlockSpec((B_HD, SUBLANES, lanes), lambda i: (0, jnp.minimum((i + 1) * (tc // SUBLANES), last8), 0))
    vspec = pl.BlockSpec((tc * vrows, lanes), lambda i: (i, 0))
    return pl.pallas_call(
        functools.partial(_scan_kernel, tc=tc),
        out_shape=jax.ShapeDtypeStruct(v.shape, F32),
        grid=(t // tc,),
        in_specs=[kspec] * 5 + [kxspec, vspec],
        out_specs=vspec,
        scratch_shapes=[pltpu.VMEM((V_SLABS, B_HD, SUBLANES, lanes), F32), pltpu.VMEM((V_SLABS, SUBLANES, lanes), F32)],
        compiler_params=_cparams(("arbitrary",)),
        name="rwkv_scan",
    )(gam, bb, kd, r, kk, kk, v)


TIME_BLOCK = 128
HEAD_PAIRS = B_WIDTH // LANES


def _scan_block_maps(nctx, nlat):
    fwd_c = lambda i: jnp.minimum(i, nctx - 1)
    fwd_l = lambda i: jnp.clip(i - nctx, 0, nlat - 1)
    bwd_c = lambda i: jnp.clip(nctx - 1 - i, 0, nctx - 1)
    bwd_l = lambda i: jnp.clip(nlat - 1 - (i - nctx), 0, nlat - 1)
    return fwd_c, fwd_l, bwd_c, bwd_l


def _fill_transposed(u_ref, src_ref, direction, rev_ref, bsz):
    for b in range(bsz):
        for j in range(HEAD_PAIRS):
            tile = src_ref[b, :, j * LANES:(j + 1) * LANES]
            if direction == 1:
                tile = _permute_rows(rev_ref[...], tile)
            q = (direction * bsz + b) * HEAD_PAIRS + j
            u_ref[q * LANES:(q + 1) * LANES, :] = tile.T


def _fill_both(u_ref, cf_ref, lf_ref, cb_ref, lb_ref, rev_ref, nctx, bsz):
    i = pl.program_id(0)

    @pl.when(i < nctx)
    def _():
        _fill_transposed(u_ref, cf_ref, 0, rev_ref, bsz)
        _fill_transposed(u_ref, cb_ref, 1, rev_ref, bsz)

    @pl.when(i >= nctx)
    def _():
        _fill_transposed(u_ref, lf_ref, 0, rev_ref, bsz)
        _fill_transposed(u_ref, lb_ref, 1, rev_ref, bsz)


def _to_scan_k_kernel(cf_ref, lf_ref, cb_ref, lb_ref, rev_ref, o_ref, u_ref, *, nctx, bsz):
    _fill_both(u_ref, cf_ref, lf_ref, cb_ref, lb_ref, rev_ref, nctx, bsz)
    nrow = u_ref.shape[0] // B_HD
    for k in range(B_HD):
        g = u_ref[pl.ds(k, nrow, stride=B_HD), :]
        o_ref[k] = jnp.concatenate([g, g], axis=0).T


def _to_scan_v_kernel(cf_ref, lf_ref, cb_ref, lb_ref, rev_ref, o_ref, u_ref, *, nctx, bsz):
    _fill_both(u_ref, cf_ref, lf_ref, cb_ref, lb_ref, rev_ref, nctx, bsz)
    nrow = u_ref.shape[0] // B_HD
    vrows = B_HD // 2
    for vi in range(vrows):
        halves = [u_ref[pl.ds(half * vrows + vi, nrow, stride=B_HD), :] for half in range(2)]
        o_ref[pl.ds(vi, TIME_BLOCK, stride=vrows), :] = jnp.concatenate(halves, axis=0).T


def _to_scan(ctx_f, lat_f, ctx_b, lat_b, rev, values):
    bsz, nc, width = ctx_f.shape
    nl = lat_f.shape[1]
    nctx, nlat = nc // TIME_BLOCK, nl // TIME_BLOCK
    lanes = 4 * bsz * B_HEADS
    t = nc + nl
    maps = _scan_block_maps(nctx, nlat)
    src = lambda f: pl.BlockSpec((bsz, TIME_BLOCK, width), lambda i, f=f: (0, f(i), 0))
    if values:
        kern, out_shape = _to_scan_v_kernel, (t * (B_HD // 2), lanes)
        out_spec = pl.BlockSpec((TIME_BLOCK * (B_HD // 2), lanes), lambda i: (i, 0))
    else:
        kern, out_shape = _to_scan_k_kernel, (B_HD, t, lanes)
        out_spec = pl.BlockSpec((B_HD, TIME_BLOCK, lanes), lambda i: (0, i, 0))
    return pl.pallas_call(
        functools.partial(kern, nctx=nctx, bsz=bsz),
        out_shape=jax.ShapeDtypeStruct(out_shape, F32),
        grid=(nctx + nlat,),
        in_specs=[src(maps[0]), src(maps[1]), src(maps[2]), src(maps[3]),
                  pl.BlockSpec((TIME_BLOCK, TIME_BLOCK), lambda i: (0, 0))],
        out_specs=out_spec,
        scratch_shapes=[pltpu.VMEM((2 * bsz * HEAD_PAIRS * LANES, TIME_BLOCK), F32)],
        compiler_params=_cparams(("parallel",)),
        name="to_scan_v" if values else "to_scan_k",
    )(ctx_f, lat_f, ctx_b, lat_b, rev)


def _from_scan_kernel(y_ref, rev_ref, o0_ref, o1_ref, u_ref, *, bsz):
    nrow = u_ref.shape[0] // B_HD
    vrows = B_HD // 2
    for vi in range(vrows):
        mt = y_ref[pl.ds(vi, TIME_BLOCK, stride=vrows), :].T
        for half in range(2):
            u_ref[pl.ds(half * vrows + vi, nrow, stride=B_HD), :] = mt[half * nrow:(half + 1) * nrow]
    for direction, o_ref in enumerate((o0_ref, o1_ref)):
        for b in range(bsz):
            for j in range(HEAD_PAIRS):
                q = (direction * bsz + b) * HEAD_PAIRS + j
                tile = u_ref[q * LANES:(q + 1) * LANES, :].T
                if direction == 1:
                    tile = _permute_rows(rev_ref[...], tile)
                o_ref[b, :, j * LANES:(j + 1) * LANES] = tile


def _from_scan(y, rev, bsz, nc, nl):
    nctx, nlat = nc // TIME_BLOCK, nl // TIME_BLOCK
    lanes = y.shape[1]
    vrows = B_HD // 2
    bwd = lambda i: jnp.where(i < nctx, nctx - 1 - i, 2 * nctx + nlat - 1 - i)
    out = jax.ShapeDtypeStruct((bsz, nc + nl, B_WIDTH), F32)
    return pl.pallas_call(
        functools.partial(_from_scan_kernel, bsz=bsz),
        out_shape=(out, out),
        grid=(nctx + nlat,),
        in_specs=[pl.BlockSpec((TIME_BLOCK * vrows, lanes), lambda i: (i, 0)),
                  pl.BlockSpec((TIME_BLOCK, TIME_BLOCK), lambda i: (0, 0))],
        out_specs=[pl.BlockSpec((bsz, TIME_BLOCK, B_WIDTH), lambda i: (0, i, 0)),
                   pl.BlockSpec((bsz, TIME_BLOCK, B_WIDTH), lambda i: (0, bwd(i), 0))],
        scratch_shapes=[pltpu.VMEM((2 * bsz * HEAD_PAIRS * LANES, TIME_BLOCK), F32)],
        compiler_params=_cparams(("parallel",)),
        name="from_scan",
    )(y, rev)


def _rwkv_bidirectional(prep_c, prep_l):
    bsz, nc, _ = prep_c['v'].shape
    nl = prep_l['v'].shape[1]
    rev = jnp.asarray(np.eye(TIME_BLOCK, dtype=np.float32)[::-1].copy(), BF16)
    pair = lambda fwd, bwd, values=False: _to_scan(prep_c[fwd], prep_l[fwd], prep_c[bwd], prep_l[bwd], rev, values)
    y = _rwkv_scan(pair('gam0', 'gam1'), pair('b0', 'b1'), pair('kd0', 'kd1'), pair('r0', 'r1'), pair('kk', 'kk'),
                   pair('v', 'v', True))
    return _from_scan(y, rev, bsz, nc, nl)


def _merge_kernel(a_ref, o0_ref, o1_ref, bonus_ref, g_ref, cv_ref, cprev_ref, cnext_ref, ga_ref, gb_ref, gc_ref,
                  x_ref, lng_ref, lnb_ref, cw_ref, wb_ref, wo_ref, ones_ref, g1_ref, n2_ref, sc_ref, sh_ref,
                  x_o, hf_o, hb_o):
    gmean = lambda t: _group_sum(t, ones_ref[...]) * (1.0 / B_HD)

    def group_norm(o):
        dlt = o - gmean(o)
        return dlt * lax.rsqrt(gmean(dlt * dlt) + GN_EPS) * lng_ref[...] + lnb_ref[...]

    yb = (group_norm(o0_ref[0]) + group_norm(o1_ref[0]) + bonus_ref[0]) * g_ref[0]

    w = C_WIDTH
    u = lambda blk: blk[:, w:2 * w] * blk[:, 2 * w:3 * w]
    cv = cv_ref[0].astype(F32)
    i = pl.program_id(1)
    pr = u(cprev_ref[0].astype(F32)[HALO_ROWS - 1:HALO_ROWS])
    nr = u(cnext_ref[0].astype(F32)[0:1])
    pr = jnp.where(i == 0, jnp.zeros_like(pr), pr)
    nr = jnp.where(i == pl.num_programs(1) - 1, jnp.zeros_like(nr), nr)
    yc = cv[:, 0:w] * _dwconv3(u(cv), pr, nr, cw_ref)

    d = D_MODEL
    m = jnp.zeros((cv.shape[0], d), F32)
    for j, (yj, gate_ref) in enumerate(((a_ref[0], ga_ref), (yb, gb_ref), (yc, gc_ref))):
        gate = jax.nn.sigmoid(gate_ref[0].astype(F32))
        m = m + gate * jnp.dot(yj.astype(BF16), wb_ref[j], preferred_element_type=F32)
    ymix = jnp.dot(m.astype(BF16), wo_ref[...], preferred_element_type=F32)
    x1 = x_ref[0] + g1_ref[0] * ymix
    x_o[0] = x1
    ms = jnp.mean(x1 * x1, axis=-1, keepdims=True)
    h = x1 * lax.rsqrt(ms + NORM_EPS) * n2_ref[...] * (1.0 + sc_ref[0]) + sh_ref[0]
    hf_o[0] = h
    hb_o[0] = h.astype(BF16)


def _merge(a, o0, o1, row_off, bonus, g, p, x, lw, g1, sc2, sh2):
    b, n, d = x.shape
    tm = _pick(n, 256)
    w = C_WIDTH
    assert row_off % tm == 0
    tile = lambda width: pl.BlockSpec((1, tm, width), lambda bi, i: (bi, i, 0))
    scan_tile = pl.BlockSpec((1, tm, w), lambda bi, i: (bi, i + row_off // tm, 0))
    const = lambda shape: pl.BlockSpec(shape, lambda bi, i: (0,) * len(shape))
    per_b = pl.BlockSpec((1, 1, d), lambda bi, i: (bi, 0, 0))
    gate = lambda j: pl.BlockSpec((1, tm, d), lambda bi, i, j=j: (bi, i, COL_GATES // d + j))
    in_specs = ([tile(w), scan_tile, scan_tile, tile(w), tile(w)]
                + _halo_specs(tm, 3 * w, COL_CONV // (3 * w), n) + [gate(0), gate(1), gate(2)]
                + [tile(d), const((1, w)), const((1, w)), const((3, w)), const((N_BRANCH, w, d)), const((d, d)),
                   const((w, w)), per_b, const((1, d)), per_b, per_b])
    return pl.pallas_call(
        _merge_kernel,
        out_shape=(jax.ShapeDtypeStruct((b, n, d), F32), jax.ShapeDtypeStruct((b, n, d), F32),
                   jax.ShapeDtypeStruct((b, n, d), BF16)),
        grid=(b, n // tm),
        in_specs=in_specs,
        out_specs=[tile(d)] * 3,
        compiler_params=_cparams(("parallel", "parallel")),
        name="merge_mixers",
    )(a, o0, o1, bonus, g, p, p, p, p, p, p, x, lw['rwkv_ln_g'], lw['rwkv_ln_b'], lw['conv_w'], lw['w_branch_bf16'],
      lw['w_out_bf16'], lw['ones64'], g1, lw['norm2_g'], sc2, sh2)


def _router_kernel(h_ref, rw_ref, o_ref):
    logits = lax.dot_general(rw_ref[...], h_ref[0], (((1,), (1,)), ((), ())), precision=HIGHEST,
                             preferred_element_type=F32)
    z = jnp.exp(logits - jnp.max(logits, axis=0, keepdims=True))
    o_ref[0] = z / jnp.sum(z, axis=0, keepdims=True)


def _router(h, rw_t):
    b, n, d = h.shape
    tm = _pick(n, 512)
    return pl.pallas_call(
        _router_kernel,
        out_shape=jax.ShapeDtypeStruct((b, N_EXPERTS, n), F32),
        grid=(b, n // tm),
        in_specs=[pl.BlockSpec((1, tm, d), lambda bi, i: (bi, i, 0)), pl.BlockSpec((N_EXPERTS, d), lambda bi, i: (0, 0))],
        out_specs=pl.BlockSpec((1, N_EXPERTS, tm), lambda bi, i: (bi, 0, i)),
        compiler_params=_cparams(("parallel", "parallel")),
        name="router_affinity",
    )(h, rw_t)


def _select_kernel(aff_ref, tri_ref, pos_o, cnt_o, *, cap, lb):
    aff = aff_ref[0]
    n = aff.shape[1]
    bits = pltpu.bitcast(aff, jnp.int32)
    count = lambda mask: jnp.sum(mask.astype(F32), axis=1, keepdims=True)

    def refine(i, lo):
        cand = lo | jnp.left_shift(jnp.int32(1), 30 - i)
        return jnp.where(count(bits >= cand) >= cap, cand, lo)

    thr = lax.fori_loop(0, 31, refine, jnp.zeros((N_EXPERTS, 1), jnp.int32))
    gt = bits > thr
    eq = bits == thr
    need = cap - count(gt)
    tri = tri_ref[...]

    def prefix(mask_of_block, write):
        off = jnp.zeros((N_EXPERTS, 1), F32)
        for j in range(n // lb):
            m = mask_of_block(j)
            excl = jnp.dot(m.astype(BF16), tri, preferred_element_type=F32) + off
            write(j, m, excl)
            off = off + jnp.sum(m.astype(F32), axis=1, keepdims=True)

    blk = lambda x, j: x[:, j * lb:(j + 1) * lb]
    sel_blocks = [None] * (n // lb)

    def write_sel(j, m, excl):
        sel_blocks[j] = blk(gt, j) | (m & (excl < need))

    prefix(lambda j: blk(eq, j), write_sel)

    def write_pos(j, m, excl):
        c = excl.astype(jnp.int32)
        cnt_o[0, :, j * lb:(j + 1) * lb] = c
        pos_o[0, :, j * lb:(j + 1) * lb] = jnp.where(m, c, -1)

    prefix(lambda j: sel_blocks[j], write_pos)


def _select(aff, cap):
    b, e, n = aff.shape
    lb = min(n, LANES)
    tri = jnp.asarray(np.triu(np.ones((lb, lb), np.float32), 1), BF16)
    blk = pl.BlockSpec((1, e, n), lambda bi: (bi, 0, 0))
    out = jax.ShapeDtypeStruct((b, e, n), jnp.int32)
    return pl.pallas_call(
        functools.partial(_select_kernel, cap=cap, lb=lb),
        out_shape=(out, out),
        grid=(b,),
        in_specs=[blk, pl.BlockSpec((lb, lb), lambda bi: (0, 0))],
        out_specs=[blk, blk],
        compiler_params=_cparams(("parallel",)),
        name="expert_select",
    )(aff, tri)


GATHER_CHUNK = 128
FFN_ROWS = 512


def _expert_kernel(starts_ref, pos_ref, aff_ref, h_ref, w1_ref, w3_ref, w2_ref, y_o, xin_s, gate_s, w1_s, w3_s, w2_s,
                   *, cap, tile, ntile, tpg, nchunk):
    e, bi, g = pl.program_id(0), pl.program_id(1), pl.program_id(2)

    @pl.when((bi == 0) & (g == 0))
    def _():
        w1_s[...] = w1_ref[0, 0].astype(BF16)
        w3_s[...] = w3_ref[0, 0].astype(BF16)
        w2_s[...] = w2_ref[0, 0].astype(BF16)

    @pl.when(g == 0)
    def _():
        xin_s[...] = jnp.zeros(xin_s.shape, F32)
        gate_s[...] = jnp.zeros(gate_s.shape, F32)

    for tt in range(tpg):
        base = (e * pl.num_programs(1) + bi) * (ntile + 1) + g * tpg + tt
        start, end = starts_ref[base], starts_ref[base + 1]
        w0 = (start // SUBLANES) * SUBLANES
        cols = slice(tt * tile, (tt + 1) * tile)
        for c in range(nchunk):
            r0 = pl.multiple_of(w0 + c * GATHER_CHUNK, SUBLANES)

            @pl.when((r0 < end) & (r0 + GATHER_CHUNK > start))
            def _():
                slot = r0 + lax.broadcasted_iota(jnp.int32, (GATHER_CHUNK, tile), 0)
                onehot = pos_ref[0, 0][:, cols] == slot
                rows = pl.ds(r0, GATHER_CHUNK)
                xin_s[rows, :] += jnp.dot(onehot.astype(BF16), h_ref[0, cols, :], preferred_element_type=F32)
                gate = jnp.sum(jnp.where(onehot, aff_ref[0, 0][:, cols], 0.0), axis=1, keepdims=True)
                gate_s[rows, :] += jnp.broadcast_to(gate, (GATHER_CHUNK, gate_s.shape[1]))

    @pl.when(g == pl.num_programs(2) - 1)
    def _():
        rc = min(cap, FFN_ROWS)
        capp = y_o.shape[2]
        if capp > cap:
            y_o[0, 0, cap:capp, :] = jnp.zeros((capp - cap, y_o.shape[3]), BF16)
        for c in range(cap // rc):
            x = xin_s[c * rc:(c + 1) * rc, :].astype(BF16)
            hid = (_silu(jnp.dot(x, w1_s[...], preferred_element_type=F32))
                   * jnp.dot(x, w3_s[...], preferred_element_type=F32))
            y = jnp.dot(hid.astype(BF16), w2_s[...], preferred_element_type=F32)
            y_o[0, 0, c * rc:(c + 1) * rc, :] = (y * gate_s[c * rc:(c + 1) * rc, 0:1]).astype(BF16)


def _moe_tile(n):
    return _pick(n, 512)


def _experts(starts, pos4, aff4, h_bf16, w1, w3, w2, layer, cap):
    b, n, d = h_bf16.shape
    tile = _moe_tile(n)
    ntile = n // tile
    nchunk = tile // GATHER_CHUNK + 1
    rows = cap + nchunk * GATHER_CHUNK
    capp = -(-cap // SLOT_BLOCK) * SLOT_BLOCK
    f = w1.shape[3]
    tg = _pick(n, 2048)
    tpg = tg // tile
    grid_spec = pltpu.PrefetchScalarGridSpec(
        num_scalar_prefetch=1,
        grid=(N_EXPERTS, b, n // tg),
        in_specs=[pl.BlockSpec((1, 1, 1, tg), lambda e, bi, g, s: (bi, e, 0, g)),
                  pl.BlockSpec((1, 1, 1, tg), lambda e, bi, g, s: (bi, e, 0, g)),
                  pl.BlockSpec((1, tg, d), lambda e, bi, g, s: (bi, g, 0)),
                  pl.BlockSpec((1, 1, d, f), lambda e, bi, g, s: (layer, e, 0, 0), pipeline_mode=pl.Buffered(1)),
                  pl.BlockSpec((1, 1, d, f), lambda e, bi, g, s: (layer, e, 0, 0), pipeline_mode=pl.Buffered(1)),
                  pl.BlockSpec((1, 1, f, d), lambda e, bi, g, s: (layer, e, 0, 0), pipeline_mode=pl.Buffered(1))],
        out_specs=pl.BlockSpec((1, 1, capp, d), lambda e, bi, g, s: (e, bi, 0, 0)),
        scratch_shapes=[pltpu.VMEM((rows, d), F32), pltpu.VMEM((rows, LANES), F32),
                        pltpu.VMEM((d, f), BF16), pltpu.VMEM((d, f), BF16), pltpu.VMEM((f, d), BF16)])
    return pl.pallas_call(
        functools.partial(_expert_kernel, cap=cap, tile=tile, ntile=ntile, tpg=tpg, nchunk=nchunk),
        out_shape=jax.ShapeDtypeStruct((N_EXPERTS, b, capp, d), BF16),
        grid_spec=grid_spec,
        compiler_params=_cparams(("arbitrary", "arbitrary", "arbitrary")),
        name="expert_gather_ffn",
    )(starts, pos4, aff4, h_bf16, w1, w3, w2)


def _combine_kernel(starts_ref, pos_ref, *refs, tile, tpg, ntile, sb, nblk):
    y_refs, (x_ref, g2_ref, o_ref) = refs[:nblk], refs[nblk:]
    bi, gi, e = pl.program_id(0), pl.program_id(1), pl.program_id(2)

    @pl.when(e == 0)
    def _():
        o_ref[...] = jnp.zeros(o_ref.shape, F32)

    base = (e * pl.num_programs(0) + bi) * (ntile + 1) + gi * tpg
    start, end = starts_ref[base], starts_ref[base + tpg]
    k0 = start // sb
    lane = lax.broadcasted_iota(jnp.int32, (1, LANES), 1)
    pos_t = jnp.concatenate(
        [jnp.broadcast_to(pos_ref[0, 0][:, tt * tile:(tt + 1) * tile].astype(F32), (LANES, tile)).T
         for tt in range(tpg)], axis=0)
    for j, y_ref in enumerate(y_refs):
        s0 = (k0 + j) * sb

        @pl.when((s0 < end) & (end > start))
        def _():
            blocks = [pos_t == (s0 + jb * LANES + lane).astype(F32) for jb in range(sb // LANES)]
            onehot_t = jnp.concatenate(blocks, axis=1).astype(BF16)
            o_ref[0] += jnp.dot(onehot_t, y_ref[0, 0], preferred_element_type=F32)

    @pl.when(e == pl.num_programs(2) - 1)
    def _():
        o_ref[0] = x_ref[0] + g2_ref[0] * o_ref[0]


def _combine(starts, pos4, y, x, g2):
    b, n, d = x.shape
    tile = _moe_tile(n)
    ntile = n // tile
    gt = _pick(n, 1024)
    tpg = gt // tile
    sb = SLOT_BLOCK
    nsb = y.shape[2] // sb
    nblk = min(gt // sb + 1, nsb)

    def yblk(j):
        def imap(bi, gi, e, s):
            k0 = s[(e * b + bi) * (ntile + 1) + gi * tpg] // sb
            return (e, bi, jnp.minimum(k0 + j, nsb - 1), 0)
        return pl.BlockSpec((1, 1, sb, d), imap)

    grid_spec = pltpu.PrefetchScalarGridSpec(
        num_scalar_prefetch=1,
        grid=(b, n // gt, N_EXPERTS),
        in_specs=[pl.BlockSpec((1, 1, 1, gt), lambda bi, gi, e, s: (bi, e, 0, gi))]
                 + [yblk(j) for j in range(nblk)]
                 + [pl.BlockSpec((1, gt, d), lambda bi, gi, e, s: (bi, gi, 0)),
                    pl.BlockSpec((1, 1, d), lambda bi, gi, e, s: (bi, 0, 0))],
        out_specs=pl.BlockSpec((1, gt, d), lambda bi, gi, e, s: (bi, gi, 0)),
        scratch_shapes=[])
    return pl.pallas_call(
        functools.partial(_combine_kernel, tile=tile, tpg=tpg, ntile=ntile, sb=sb, nblk=nblk),
        out_shape=jax.ShapeDtypeStruct((b, n, d), F32),
        grid_spec=grid_spec,
        compiler_params=_cparams(("parallel", "parallel", "arbitrary")),
        name="expert_combine",
    )(starts, pos4, *([y] * nblk), x, g2)


def _moe(h_f32, h_bf16, x, g2, lw):
    b, n, _ = x.shape
    cap = CAPACITY_FACTOR * n // N_EXPERTS
    aff = _router(h_f32, lw['router_w_t'])
    pos, cnt = _select(aff, cap)
    tile = _moe_tile(n)
    starts = jnp.concatenate([cnt[:, :, ::tile], jnp.full((b, N_EXPERTS, 1), cap, jnp.int32)], axis=2)
    starts = jnp.transpose(starts, (1, 0, 2)).reshape(-1)
    pos4 = pos.reshape(b, N_EXPERTS, 1, n)
    aff4 = aff.reshape(b, N_EXPERTS, 1, n)
    y = _experts(starts, pos4, aff4, h_bf16, lw['exp_w1'], lw['exp_w3'], lw['exp_w2'], lw['layer'], cap)
    return _combine(starts, pos4, y, x, g2)


def _pad_w_in(w_in):
    d = w_in.shape[0]
    sizes = (A_WIDTH,) * 3 + (B_WIDTH,) * 3 + (B_DECAY_RANK, B_ICL_RANK, B_GATE_RANK) + (C_WIDTH,) * 3 + (N_BRANCH * D_MODEL,)
    cuts = np.cumsum((0,) + sizes)
    piece = lambda lo, hi: w_in[:, int(cuts[lo]):int(cuts[hi])]
    zeros = jnp.zeros((d, COL_GATES - COL_LOWRANK - 256), w_in.dtype)
    out = jnp.concatenate([piece(0, 3), piece(3, 6), piece(9, 12), piece(6, 9), zeros, piece(12, 13)], axis=1)
    assert out.shape[1] == N_IN_PAD
    return out.astype(BF16)


def _rows_at(w, lo, total=256):
    pad = [(0, 0)] * w.ndim
    pad[-2] = (lo, total - lo - w.shape[-2])
    return jnp.pad(w, pad)


def _rope_tables(rows):
    half = A_HD // 2
    inv = jnp.power(ROPE_THETA, -jnp.arange(0, half, 2, dtype=F32) / half)
    r = jnp.repeat(jnp.arange(rows, dtype=F32), GRID_W)
    col = jnp.tile(jnp.arange(GRID_W, dtype=F32), rows)
    ang = jnp.concatenate([r[:, None] * inv, col[:, None] * inv], axis=-1)
    cos, sin = jnp.cos(ang), jnp.sin(ang)
    reps = A_WIDTH // A_HD
    cos_t = jnp.tile(jnp.concatenate([cos, cos], axis=-1), (1, reps))
    sin_t = jnp.tile(jnp.concatenate([-sin, sin], axis=-1), (1, reps))
    return cos_t, sin_t


def _layer_weights(i, norm2_g, w_in, q_norm_g, k_norm_g, diff_lambda, diff_subln_g, rwkv_conv_w, rwkv_w0, rwkv_w_up,
                   rwkv_a0, rwkv_a_up, rwkv_g_up, rwkv_k_k, rwkv_k_a, rwkv_r_k, rwkv_ln_g, rwkv_ln_b, conv_w,
                   w_branch, w_out, router_w, exp_w1, exp_w3, exp_w2):
    row = lambda v: v.reshape(1, -1)
    return {
        'norm2_g': row(norm2_g[i]),
        'w_in_pad': _pad_w_in(w_in[i]),
        'q_g': row(jnp.tile(q_norm_g[i], A_WIDTH // A_HD)), 'k_g': row(jnp.tile(k_norm_g[i], A_WIDTH // A_HD)),
        'diff_lambda': diff_lambda[i], 'subln_g': row(diff_subln_g[i]),
        'rwkv_conv_w': rwkv_conv_w[i], 'rwkv_w0': rwkv_w0[i], 'rwkv_a0': rwkv_a0[i],
        'w_up_pad': _rows_at(rwkv_w_up[i], 0), 'a_up_pad': _rows_at(rwkv_a_up[i], B_DECAY_RANK),
        'g_up_pad': _rows_at(rwkv_g_up[i], B_DECAY_RANK + B_ICL_RANK),
        'rwkv_k_k': row(rwkv_k_k[i]), 'rwkv_k_a': row(rwkv_k_a[i]), 'rwkv_r_k': row(rwkv_r_k[i]),
        'rwkv_ln_g': row(rwkv_ln_g[i]), 'rwkv_ln_b': row(rwkv_ln_b[i]),
        'conv_w': conv_w[i], 'w_branch_bf16': w_branch[i].astype(BF16), 'w_out_bf16': w_out[i].astype(BF16),
        'router_w_t': router_w[i].T,
        'exp_w1': exp_w1, 'exp_w3': exp_w3, 'exp_w2': exp_w2, 'layer': i,
        'ones64': _group_ones(B_WIDTH, B_HD).astype(BF16),
    }


def _mixer_inputs(x, norm1_g, sc1, sh1, lw, cos_t, sin_t):
    p = _norm_mod_proj(x, norm1_g, sc1, sh1, lw['w_in_pad'])
    q, k = _qkv_prep(p, lw['q_g'], lw['k_g'], lw['ones64'], cos_t, sin_t)
    prep = dict(zip(PREP_NAMES, _rwkv_prep(p, lw)))
    return p, (q, k), prep


def kernel(x, c, ctx, c_ctx, norm1_g, norm2_g, ada_w, ada_b, w_in, q_norm_g, k_norm_g, diff_lambda, diff_subln_g,
           rwkv_conv_w, rwkv_w0, rwkv_w_up, rwkv_a0, rwkv_a_up, rwkv_g_up, rwkv_k_k, rwkv_k_a, rwkv_r_k, rwkv_ln_g,
           rwkv_ln_b, conv_w, w_branch, w_out, router_w, exp_w1, exp_w3, exp_w2):
    bsz, n_lat, d = x.shape
    depth = ada_w.shape[0]
    cos_t, sin_t = _rope_tables(n_lat // GRID_W)

    cond = jnp.concatenate([c, jnp.broadcast_to(c_ctx[None, :], (SUBLANES - bsz % SUBLANES, d))], axis=0)
    ada = _ada_params(cond, ada_w, ada_b)

    xc, xl = ctx, x
    for i in range(depth):
        lam_init = 0.8 - 0.6 * math.exp(-0.3 * i)
        need_ctx = i != depth - 1
        lw = _layer_weights(i, norm2_g, w_in, q_norm_g, k_norm_g, diff_lambda, diff_subln_g, rwkv_conv_w, rwkv_w0,
                            rwkv_w_up, rwkv_a0, rwkv_a_up, rwkv_g_up, rwkv_k_k, rwkv_k_a, rwkv_r_k, rwkv_ln_g,
                            rwkv_ln_b, conv_w, w_branch, w_out, router_w, exp_w1, exp_w3, exp_w2)
        mods = ada[i].reshape(-1, 6, d)
        lat = [mods[:bsz, j][:, None, :] for j in range(6)]
        cxt = [jnp.broadcast_to(mods[bsz, j][None, None, :], (bsz, 1, d)) for j in range(6)]

        pc, (qc, kc), prep_c = _mixer_inputs(xc, norm1_g[i], cxt[1], cxt[0], lw, None, None)
        pl_, (ql, kl), prep_l = _mixer_inputs(xl, norm1_g[i], lat[1], lat[0], lw, cos_t, sin_t)

        v_cols = slice(COL_ATTN + 2 * A_WIDTH, COL_ATTN + 3 * A_WIDTH)
        k_all = jnp.concatenate([kc, kl], axis=1)
        v_all = jnp.concatenate([pc[:, :, v_cols], pl_[:, :, v_cols]], axis=1)
        al = _flash_attention(lw['diff_lambda'], lw['subln_g'], ql, k_all, v_all, lam_init)
        o0, o1 = _rwkv_bidirectional(prep_c, prep_l)
        n_ctx = xc.shape[1]

        xl, hl, hlb = _merge(al, o0, o1, n_ctx, prep_l['bonus'], prep_l['g'], pl_, xl, lw, lat[2], lat[4], lat[3])
        xl = _moe(hl, hlb, xl, lat[5], lw)
        if need_ctx:
            ac = _ctx_attention(lw['diff_lambda'], lw['subln_g'], qc, kc, pc, lam_init)
            xc, hc, hcb = _merge(ac, o0, o1, 0, prep_c['bonus'], prep_c['g'], pc, xc, lw, cxt[2], cxt[4], cxt[3])
            xc = _moe(hc, hcb, xc, cxt[5], lw)
    return xl
```

```python
import functools
import math

import numpy as np
import jax
import jax.numpy as jnp
from jax import lax
from jax.experimental import pallas as pl
from jax.experimental.pallas import tpu as pltpu

F32 = jnp.float32
BF16 = jnp.bfloat16
HIGHEST = lax.Precision.HIGHEST

D_MODEL = 1024
GRID_W = 64
A_HEADS = 4
A_HD = 64
A_WIDTH = A_HEADS * 2 * A_HD
B_HEADS = 8
B_HD = 64
B_WIDTH = B_HEADS * B_HD
B_DECAY_RANK = 64
B_ICL_RANK = 64
B_GATE_RANK = 128
C_WIDTH = 512
N_BRANCH = 3
N_EXPERTS = 16
EXPERT_FF = 1024
CAPACITY_FACTOR = 2
ROPE_THETA = 10000.0
NORM_EPS = 1e-6
GN_EPS = 64e-5
LOG2_E = math.log2(math.e)

COL_ATTN = 0
COL_RWKV = 1536
COL_CONV = 3072
COL_LOWRANK = 4608
COL_GATES = 5120
N_IN_PAD = 8192

VMEM_LIMIT = 56 * 1024 * 1024
LANES = 128
SUBLANES = 8
SLOT_BLOCK = 256


def _cparams(sem):
    return pltpu.CompilerParams(dimension_semantics=sem, vmem_limit_bytes=VMEM_LIMIT)


def _pick(n, pref):
    t = min(n, pref)
    while n % t:
        t -= SUBLANES
    return t


def _group_ones(width, group):
    idx = np.arange(width) // group
    return jnp.asarray((idx[:, None] == idx[None, :]).astype(np.float32))


def _silu(x):
    return x * jax.nn.sigmoid(x)


def _split2(x):
    hi = x.astype(BF16)
    return hi, (x - hi.astype(F32)).astype(BF16)


def _group_sum(x, ones_bf16):
    hi, lo = _split2(x)
    return (jnp.dot(hi, ones_bf16, preferred_element_type=F32) + jnp.dot(lo, ones_bf16, preferred_element_type=F32))


def _permute_rows(perm_bf16, x):
    hi, lo = _split2(x)
    rest = x - hi.astype(F32) - lo.astype(F32)
    dot = lambda v: jnp.dot(perm_bf16, v, preferred_element_type=F32)
    return dot(hi) + dot(lo) + dot(rest.astype(BF16))


def _dot3(a, b):
    a_hi, a_lo = _split2(a)
    b_hi, b_lo = _split2(b)
    dot = lambda u, v: jnp.dot(u, v, preferred_element_type=F32)
    return dot(a_hi, b_hi) + (dot(a_hi, b_lo) + dot(a_lo, b_hi))


def _ada_kernel(cond_ref, w_ref, b_ref, o_ref):
    c = cond_ref[...]
    o_ref[0] = jnp.dot(_silu(c), w_ref[0], precision=HIGHEST, preferred_element_type=F32) + b_ref[0]


def _ada_params(cond, ada_w, ada_b):
    depth, d, n6 = ada_w.shape
    tn = _pick(n6, 1536)
    return pl.pallas_call(
        _ada_kernel,
        out_shape=jax.ShapeDtypeStruct((depth, cond.shape[0], n6), F32),
        grid=(depth, n6 // tn),
        in_specs=[pl.BlockSpec(cond.shape, lambda i, j: (0, 0)),
                  pl.BlockSpec((1, d, tn), lambda i, j: (i, 0, j)),
                  pl.BlockSpec((1, 1, tn), lambda i, j: (i, 0, j))],
        out_specs=pl.BlockSpec((1, cond.shape[0], tn), lambda i, j: (i, 0, j)),
        compiler_params=_cparams(("parallel", "parallel")),
        name="ada_params",
    )(cond, ada_w, ada_b.reshape(depth, 1, n6))


def _proj_kernel(x_ref, g_ref, sc_ref, sh_ref, w_ref, o_ref):
    x = x_ref[0]
    ms = jnp.mean(x * x, axis=-1, keepdims=True)
    y = x * lax.rsqrt(ms + NORM_EPS) * g_ref[...]
    h = y * (1.0 + sc_ref[0]) + sh_ref[0]
    o_ref[0] = jnp.dot(h.astype(BF16), w_ref[...], preferred_element_type=F32).astype(BF16)


def _norm_mod_proj(x, g, sc, sh, w_bf16):
    b, n, d = x.shape
    npad = w_bf16.shape[1]
    tm = _pick(n, 512)
    tn = 4096
    return pl.pallas_call(
        _proj_kernel,
        out_shape=jax.ShapeDtypeStruct((b, n, npad), BF16),
        grid=(npad // tn, b, n // tm),
        in_specs=[pl.BlockSpec((1, tm, d), lambda j, bi, i: (bi, i, 0)),
                  pl.BlockSpec((1, d), lambda j, bi, i: (0, 0)),
                  pl.BlockSpec((1, 1, d), lambda j, bi, i: (bi, 0, 0)),
                  pl.BlockSpec((1, 1, d), lambda j, bi, i: (bi, 0, 0)),
                  pl.BlockSpec((d, tn), lambda j, bi, i: (0, j))],
        out_specs=pl.BlockSpec((1, tm, tn), lambda j, bi, i: (bi, i, j)),
        compiler_params=_cparams(("parallel", "parallel", "parallel")),
        name="norm_mod_proj",
    )(x, g.reshape(1, d), sc, sh, w_bf16)


def _swap_halves(x, half):
    width = x.shape[-1]
    lane = lax.broadcasted_iota(jnp.int32, (1, width), 1) % (2 * half)
    fwd = pltpu.roll(x, width - half, axis=1)
    bwd = pltpu.roll(x, half, axis=1)
    return jnp.where(lane < half, fwd, bwd)


def _qkv_prep_kernel(*refs, use_rope):
    if use_rope:
        q_ref, k_ref, qg_ref, kg_ref, ones_ref, cos_ref, sin_ref, qo_ref, ko_ref = refs
    else:
        q_ref, k_ref, qg_ref, kg_ref, ones_ref, qo_ref, ko_ref = refs

    def head_norm(x, gain):
        ms = _group_sum(x * x, ones_ref[...]) * (1.0 / A_HD)
        return x * lax.rsqrt(ms + NORM_EPS) * gain

    def rope(x):
        if not use_rope:
            return x
        return x * cos_ref[...] + _swap_halves(x, A_HD // 2) * sin_ref[...]

    q = rope(head_norm(q_ref[0].astype(F32), qg_ref[...]))
    k = rope(head_norm(k_ref[0].astype(F32), kg_ref[...]))
    qo_ref[0] = (q * (A_HD ** -0.5 * LOG2_E)).astype(BF16)
    ko_ref[0] = k.astype(BF16)


def _qkv_prep(p, qg, kg, ones, cos_t, sin_t):
    b, n, _ = p.shape
    tm = _pick(n, 512)
    use_rope = cos_t is not None
    w = A_WIDTH
    col = lambda c: pl.BlockSpec((1, tm, w), lambda bi, i, c=c: (bi, i, COL_ATTN // w + c))
    const = lambda shape: pl.BlockSpec(shape, lambda bi, i: (0,) * len(shape))
    in_specs = [col(0), col(1), const((1, w)), const((1, w)), const((w, w))]
    args = [p, p, qg, kg, ones]
    if use_rope:
        in_specs += [pl.BlockSpec((tm, w), lambda bi, i: (i, 0))] * 2
        args += [cos_t, sin_t]
    out = jax.ShapeDtypeStruct((b, n, w), BF16)
    return pl.pallas_call(
        functools.partial(_qkv_prep_kernel, use_rope=use_rope),
        out_shape=(out, out),
        grid=(b, n // tm),
        in_specs=in_specs,
        out_specs=[pl.BlockSpec((1, tm, w), lambda bi, i: (bi, i, 0))] * 2,
        compiler_params=_cparams(("parallel", "parallel")),
        name="qkv_prep_rope" if use_rope else "qkv_prep",
    )(*args)


def _map_stacked(q):
    lane = lax.broadcasted_iota(jnp.int32, (1, 2 * A_HD), 1)
    zero = jnp.zeros_like(q)
    return jnp.concatenate([jnp.where(lane < A_HD, q, zero), jnp.where(lane >= A_HD, q, zero)], axis=0)


def _lane_tiled(x, width):
    return jnp.concatenate([x] * (width // LANES), axis=1)


def _lane_block_sum(p):
    out = p[:, 0:LANES]
    for j in range(1, p.shape[1] // LANES):
        out = out + p[:, j * LANES:(j + 1) * LANES]
    return out


def _lambda_full(dl_ref, lam_init):
    dl = dl_ref[...]
    s1 = jnp.sum(dl[0:1] * dl[1:2], axis=-1, keepdims=True)
    s2 = jnp.sum(dl[2:3] * dl[3:4], axis=-1, keepdims=True)
    return jnp.exp(s1) - jnp.exp(s2) + lam_init


def _subln(o, sg_ref, lam_init):
    ms = jnp.mean(o * o, axis=-1, keepdims=True)
    return o * lax.rsqrt(ms + NORM_EPS) * sg_ref[...] * (1.0 - lam_init)


def _scores(qm, k):
    return lax.dot_general(qm, k, (((1,), (1,)), ((), ())), preferred_element_type=F32)


def _flash_kernel(dl_ref, sg_ref, q_ref, k_ref, v_ref, o_ref, m_s, l_s, acc_s, *, lam_init, nkv):
    kj = pl.program_id(3)
    tq = q_ref.shape[1]

    @pl.when(kj == 0)
    def _():
        m_s[...] = jnp.full(m_s.shape, -jnp.inf, F32)
        l_s[...] = jnp.zeros(l_s.shape, F32)
        acc_s[...] = jnp.zeros(acc_s.shape, F32)

    s = _scores(_map_stacked(q_ref[0]), k_ref[0])
    m_prev = m_s[...]
    m_new = jnp.maximum(m_prev, jnp.max(s, axis=-1, keepdims=True))
    alpha = jnp.exp2(m_prev - m_new)
    p = jnp.exp2(s - _lane_tiled(m_new, s.shape[1]))
    l_s[...] = alpha * l_s[...] + _lane_block_sum(p)
    acc_s[...] = alpha * acc_s[...] + jnp.dot(p.astype(BF16), v_ref[0], preferred_element_type=F32)
    m_s[...] = m_new

    @pl.when(kj == nkv - 1)
    def _():
        lam = _lambda_full(dl_ref, lam_init)
        o = acc_s[...] / jnp.sum(l_s[...], axis=-1, keepdims=True)
        o_ref[0] = _subln(o[0:tq] - lam * o[tq:2 * tq], sg_ref, lam_init).astype(o_ref.dtype)


V_COL_BLOCK = (COL_ATTN + 2 * A_WIDTH) // (2 * A_HD)


def _pick_lanes(n, pref):
    t = min(n, pref) // LANES * LANES
    while n % t:
        t -= LANES
    return t


def _flash_attention(dl, sg, q, k_all, v_all, lam_init):
    b, s, _ = q.shape
    t = k_all.shape[1]
    hw = 2 * A_HD
    tq = _pick(s, 1024)
    tk = _pick_lanes(t, 3072)
    nkv = t // tk
    return pl.pallas_call(
        functools.partial(_flash_kernel, lam_init=lam_init, nkv=nkv),
        out_shape=jax.ShapeDtypeStruct((b, s, A_WIDTH), BF16),
        grid=(b, A_HEADS, s // tq, nkv),
        in_specs=[pl.BlockSpec((4, A_HD), lambda bi, h, i, j: (0, 0)),
                  pl.BlockSpec((1, hw), lambda bi, h, i, j: (0, 0)),
                  pl.BlockSpec((1, tq, hw), lambda bi, h, i, j: (bi, i, h)),
                  pl.BlockSpec((1, tk, hw), lambda bi, h, i, j: (bi, j, h)),
                  pl.BlockSpec((1, tk, hw), lambda bi, h, i, j: (bi, j, h))],
        out_specs=pl.BlockSpec((1, tq, hw), lambda bi, h, i, j: (bi, i, h)),
        scratch_shapes=[pltpu.VMEM((2 * tq, LANES), F32), pltpu.VMEM((2 * tq, LANES), F32), pltpu.VMEM((2 * tq, hw), F32)],
        compiler_params=_cparams(("parallel", "parallel", "parallel", "arbitrary")),
        name="diff_flash_attention",
    )(dl, sg, q, k_all, v_all)


def _ctx_attn_kernel(dl_ref, sg_ref, q_ref, k_ref, v_ref, o_ref, *, lam_init):
    n = q_ref.shape[1]
    s = _scores(_map_stacked(q_ref[0]), k_ref[0])
    p = jnp.exp2(s - jnp.max(s, axis=-1, keepdims=True))
    o = jnp.dot(p.astype(BF16), v_ref[0], preferred_element_type=F32) / jnp.sum(p, axis=-1, keepdims=True)
    lam = _lambda_full(dl_ref, lam_init)
    o_ref[0] = _subln(o[0:n] - lam * o[n:2 * n], sg_ref, lam_init).astype(o_ref.dtype)


def _ctx_attention(dl, sg, q, k, v, lam_init):
    b, n, _ = q.shape
    hw = 2 * A_HD
    blk = pl.BlockSpec((1, n, hw), lambda bi, h: (bi, 0, h))
    vblk = pl.BlockSpec((1, n, hw), lambda bi, h: (bi, 0, V_COL_BLOCK + h))
    return pl.pallas_call(
        functools.partial(_ctx_attn_kernel, lam_init=lam_init),
        out_shape=jax.ShapeDtypeStruct((b, n, A_WIDTH), BF16),
        grid=(b, A_HEADS),
        in_specs=[pl.BlockSpec((4, A_HD), lambda bi, h: (0, 0)), pl.BlockSpec((1, hw), lambda bi, h: (0, 0)),
                  blk, blk, vblk],
        out_specs=blk,
        compiler_params=_cparams(("parallel", "parallel")),
        name="ctx_attention",
    )(dl, sg, q, k, v)


HALO_ROWS = 16


def _halo_specs(tm, width, colblk, n):
    rh = tm // HALO_ROWS
    last = n // HALO_ROWS - 1
    main = pl.BlockSpec((1, tm, width), lambda bi, i: (bi, i, colblk))
    prev = pl.BlockSpec((1, HALO_ROWS, width), lambda bi, i: (bi, jnp.maximum(i * rh - 1, 0), colblk))
    nxt = pl.BlockSpec((1, HALO_ROWS, width), lambda bi, i: (bi, jnp.minimum((i + 1) * rh, last), colblk))
    return [main, prev, nxt]


def _dwconv3(x, prev_row, next_row, w_ref):
    tm = x.shape[0]
    row = lax.broadcasted_iota(jnp.int32, (tm, 1), 0)
    xp = jnp.where(row == 0, prev_row, pltpu.roll(x, 1, axis=0))
    xn = jnp.where(row == tm - 1, next_row, pltpu.roll(x, tm - 1, axis=0))
    w = w_ref[...]
    return w[0:1] * xp + w[1:2] * x + w[2:3] * xn


def _edge_rows(prev_ref, next_ref):
    i = pl.program_id(1)
    n = pl.num_programs(1)
    pr = prev_ref[0].astype(F32)[HALO_ROWS - 1:HALO_ROWS]
    nr = next_ref[0].astype(F32)[0:1]
    pr = jnp.where(i == 0, jnp.zeros_like(pr), pr)
    nr = jnp.where(i == n - 1, jnp.zeros_like(nr), nr)
    return pr, nr


def _softplus(u):
    return jnp.maximum(u, 0.0) + jnp.log(1.0 + jnp.exp(-jnp.abs(u)))


SCAN_BLOCK = 64


def _block_triangles(tm):
    i = np.arange(tm)
    same = (i[:, None] // SCAN_BLOCK) == (i[None, :] // SCAN_BLOCK)
    return jnp.asarray(np.stack([same & (i[None, :] <= i[:, None]), same & (i[None, :] >= i[:, None])]), BF16)


def _rwkv_prep_kernel(rkv_ref, prev_ref, next_ref, lr_ref, cw_ref, w0_ref, wup_ref, a0_ref, aup_ref, gup_ref,
                      kk_ref_w, ka_ref, rk_ref, ones_ref, tri_ref,
                      v_o, kk_o, g_o, bonus_o, gam0_o, b0_o, kd0_o, r0_o, gam1_o, b1_o, kd1_o, r1_o):
    pr, nr = _edge_rows(prev_ref, next_ref)
    rkv = _dwconv3(rkv_ref[0].astype(F32), pr, nr, cw_ref)
    w = B_WIDTH
    r, k, v = rkv[:, 0:w], rkv[:, w:2 * w], rkv[:, 2 * w:3 * w]
    gsum = lambda t: _group_sum(t, ones_ref[...])
    kkr = k * kk_ref_w[...]
    nrm = jnp.maximum(jnp.sqrt(gsum(kkr * kkr)), 1e-12)
    kk = kkr / nrm
    lr = lr_ref[0].astype(F32)
    th = jnp.tanh(lr)
    sg = jax.nn.sigmoid(lr)
    v_o[0] = v
    kk_o[0] = kk
    g_o[0] = _dot3(sg, gup_ref[...])
    kd_sum = jnp.zeros_like(k)
    outs = ((gam0_o, b0_o, kd0_o, r0_o), (gam1_o, b1_o, kd1_o, r1_o))
    for d in range(2):
        z = w0_ref[d:d + 1] + _dot3(th, wup_ref[d])
        wlog = -_softplus(-z) - 0.5
        log_decay = -jnp.exp(wlog)
        log_gamma = _permute_rows(tri_ref[d], log_decay)
        gamma, inv_gamma = jnp.exp(log_gamma), jnp.exp(-log_gamma)
        a = jax.nn.sigmoid(a0_ref[d:d + 1] + _dot3(lr, aup_ref[d]))
        kd = k * (1.0 + (a - 1.0) * ka_ref[...])
        outs[d][0][0] = gamma
        outs[d][1][0] = kk * a * inv_gamma
        outs[d][2][0] = kd * inv_gamma
        outs[d][3][0] = r * gamma
        kd_sum = kd_sum + kd
    bonus_o[0] = gsum(r * kd_sum * rk_ref[...]) * v


def _rwkv_prep(p, lw):
    b, n, _ = p.shape
    tm = _pick(n, 256)
    w = B_WIDTH
    w3 = 3 * w
    const = lambda shape: pl.BlockSpec(shape, lambda bi, i: (0,) * len(shape))
    in_specs = _halo_specs(tm, w3, COL_RWKV // w3, n) + [
        pl.BlockSpec((1, tm, 256), lambda bi, i: (bi, i, COL_LOWRANK // 256)),
        const((3, w3)), const((2, w)), const((2, 256, w)), const((2, w)), const((2, 256, w)), const((256, w)),
        const((1, w)), const((1, w)), const((1, w)), const((w, w)), const((2, tm, tm))]
    assert tm % SCAN_BLOCK == 0
    out = jax.ShapeDtypeStruct((b, n, w), F32)
    return pl.pallas_call(
        _rwkv_prep_kernel,
        out_shape=(out,) * len(PREP_NAMES),
        grid=(b, n // tm),
        in_specs=in_specs,
        out_specs=[pl.BlockSpec((1, tm, w), lambda bi, i: (bi, i, 0))] * len(PREP_NAMES),
        compiler_params=_cparams(("parallel", "parallel")),
        name="rwkv_prep",
    )(p, p, p, p, lw['rwkv_conv_w'], lw['rwkv_w0'], lw['w_up_pad'], lw['rwkv_a0'], lw['a_up_pad'], lw['g_up_pad'],
      lw['rwkv_k_k'], lw['rwkv_k_a'], lw['rwkv_r_k'], lw['ones64'], _block_triangles(tm))


PREP_NAMES = ('v', 'kk', 'g', 'bonus', 'gam0', 'b0', 'kd0', 'r0', 'gam1', 'b1', 'kd1', 'r1')


V_SLABS = B_HD // 2 // SUBLANES


def _scan_kernel(gam_ref, b_ref, kd_ref, r_ref, kk_ref, kkx_ref, v_ref, y_ref, s_ref, sa_ref, *, tc):
    @pl.when(pl.program_id(0) == 0)
    def _():
        s_ref[...] = jnp.zeros(s_ref.shape, F32)
        sa_ref[...] = jnp.zeros(sa_ref.shape, F32)

    lanes = s_ref.shape[-1]
    vrows = B_HD // 2
    bcast = lambda row: jnp.broadcast_to(row, (SUBLANES, lanes))

    def step(t, last_of_block):
        base = pl.multiple_of(t * vrows, vrows)
        sa = [sa_ref[s] for s in range(V_SLABS)]
        vv = [v_ref[pl.ds(base + SUBLANES * s, SUBLANES), :] for s in range(V_SLABS)]
        y = [jnp.zeros((SUBLANES, lanes), F32) for _ in range(V_SLABS)]
        san = [jnp.zeros((SUBLANES, lanes), F32) for _ in range(V_SLABS)]
        for k in range(B_HD):
            row = lambda ref: bcast(ref[k, pl.ds(t, 1), :])
            gk, bk, kdk, rk = row(gam_ref), row(b_ref), row(kd_ref), row(r_ref)
            kk_next = bcast(kkx_ref[k, 0:1, :]) if last_of_block else bcast(kk_ref[k, pl.ds(t + 1, 1), :])
            kkn = kk_next * gk
            for s in range(V_SLABS):
                new = s_ref[s, k] - sa[s] * bk + vv[s] * kdk
                s_ref[s, k] = new * gk if last_of_block else new
                y[s] = y[s] + new * rk
                san[s] = san[s] + new * kkn
        for s in range(V_SLABS):
            y_ref[pl.ds(base + SUBLANES * s, SUBLANES), :] = y[s]
            sa_ref[s] = san[s]

    def body(t, carry):
        step(t, False)
        return carry

    lax.fori_loop(0, tc - 1, body, 0)
    step(tc - 1, True)


def _rwkv_scan(gam, bb, kd, r, kk, v):
    _, t, lanes = gam.shape
    tc = SCAN_BLOCK
    assert t % tc == 0
    vrows = B_HD // 2
    last8 = t // SUBLANES - 1
    kspec = pl.BlockSpec((B_HD, tc, lanes), lambda i: (0, i, 0))
    kxspec = pl.BlockSpec((B_HD, SUBLANES, lanes), lambda i: (0, jnp.minimum((i + 1) * (tc // SUBLANES), last8), 0))
    vspec = pl.BlockSpec((tc * vrows, lanes), lambda i: (i, 0))
    return pl.pallas_call(
        functools.partial(_scan_kernel, tc=tc),
        out_shape=jax.ShapeDtypeStruct(v.shape, F32),
        grid=(t // tc,),
        in_specs=[kspec] * 5 + [kxspec, vspec],
        out_specs=vspec,
        scratch_shapes=[pltpu.VMEM((V_SLABS, B_HD, SUBLANES, lanes), F32), pltpu.VMEM((V_SLABS, SUBLANES, lanes), F32)],
        compiler_params=_cparams(("arbitrary",)),
        name="rwkv_scan",
    )(gam, bb, kd, r, kk, kk, v)


TIME_BLOCK = 128
HEAD_PAIRS = B_WIDTH // LANES


def _scan_block_maps(nctx, nlat):
    fwd_c = lambda i: jnp.minimum(i, nctx - 1)
    fwd_l = lambda i: jnp.clip(i - nctx, 0, nlat - 1)
    bwd_c = lambda i: jnp.clip(nctx - 1 - i, 0, nctx - 1)
    bwd_l = lambda i: jnp.clip(nlat - 1 - (i - nctx), 0, nlat - 1)
    return fwd_c, fwd_l, bwd_c, bwd_l


def _fill_transposed(u_ref, src_ref, direction, rev_ref, bsz):
    for b in range(bsz):
        for j in range(HEAD_PAIRS):
            tile = src_ref[b, :, j * LANES:(j + 1) * LANES]
            if direction == 1:
                tile = _permute_rows(rev_ref[...], tile)
            q = (direction * bsz + b) * HEAD_PAIRS + j
            u_ref[q * LANES:(q + 1) * LANES, :] = tile.T


def _fill_both(u_ref, cf_ref, lf_ref, cb_ref, lb_ref, rev_ref, nctx, bsz):
    i = pl.program_id(0)

    @pl.when(i < nctx)
    def _():
        _fill_transposed(u_ref, cf_ref, 0, rev_ref, bsz)
        _fill_transposed(u_ref, cb_ref, 1, rev_ref, bsz)

    @pl.when(i >= nctx)
    def _():
        _fill_transposed(u_ref, lf_ref, 0, rev_ref, bsz)
        _fill_transposed(u_ref, lb_ref, 1, rev_ref, bsz)


def _to_scan_k_kernel(cf_ref, lf_ref, cb_ref, lb_ref, rev_ref, o_ref, u_ref, *, nctx, bsz):
    _fill_both(u_ref, cf_ref, lf_ref, cb_ref, lb_ref, rev_ref, nctx, bsz)
    nrow = u_ref.shape[0] // B_HD
    for k in range(B_HD):
        g = u_ref[pl.ds(k, nrow, stride=B_HD), :]
        o_ref[k] = jnp.concatenate([g, g], axis=0).T


def _to_scan_v_kernel(cf_ref, lf_ref, cb_ref, lb_ref, rev_ref, o_ref, u_ref, *, nctx, bsz):
    _fill_both(u_ref, cf_ref, lf_ref, cb_ref, lb_ref, rev_ref, nctx, bsz)
    nrow = u_ref.shape[0] // B_HD
    vrows = B_HD // 2
    for vi in range(vrows):
        halves = [u_ref[pl.ds(half * vrows + vi, nrow, stride=B_HD), :] for half in range(2)]
        o_ref[pl.ds(vi, TIME_BLOCK, stride=vrows), :] = jnp.concatenate(halves, axis=0).T


def _to_scan(ctx_f, lat_f, ctx_b, lat_b, rev, values):
    bsz, nc, width = ctx_f.shape
    nl = lat_f.shape[1]
    nctx, nlat = nc // TIME_BLOCK, nl // TIME_BLOCK
    lanes = 4 * bsz * B_HEADS
    t = nc + nl
    maps = _scan_block_maps(nctx, nlat)
    src = lambda f: pl.BlockSpec((bsz, TIME_BLOCK, width), lambda i, f=f: (0, f(i), 0))
    if values:
        kern, out_shape = _to_scan_v_kernel, (t * (B_HD // 2), lanes)
        out_spec = pl.BlockSpec((TIME_BLOCK * (B_HD // 2), lanes), lambda i: (i, 0))
    else:
        kern, out_shape = _to_scan_k_kernel, (B_HD, t, lanes)
        out_spec = pl.BlockSpec((B_HD, TIME_BLOCK, lanes), lambda i: (0, i, 0))
    return pl.pallas_call(
        functools.partial(kern, nctx=nctx, bsz=bsz),
        out_shape=jax.ShapeDtypeStruct(out_shape, F32),
        grid=(nctx + nlat,),
        in_specs=[src(maps[0]), src(maps[1]), src(maps[2]), src(maps[3]),
                  pl.BlockSpec((TIME_BLOCK, TIME_BLOCK), lambda i: (0, 0))],
        out_specs=out_spec,
        scratch_shapes=[pltpu.VMEM((2 * bsz * HEAD_PAIRS * LANES, TIME_BLOCK), F32)],
        compiler_params=_cparams(("parallel",)),
        name="to_scan_v" if values else "to_scan_k",
    )(ctx_f, lat_f, ctx_b, lat_b, rev)


def _from_scan_kernel(y_ref, rev_ref, o0_ref, o1_ref, u_ref, *, bsz):
    nrow = u_ref.shape[0] // B_HD
    vrows = B_HD // 2
    for vi in range(vrows):
        mt = y_ref[pl.ds(vi, TIME_BLOCK, stride=vrows), :].T
        for half in range(2):
            u_ref[pl.ds(half * vrows + vi, nrow, stride=B_HD), :] = mt[half * nrow:(half + 1) * nrow]
    for direction, o_ref in enumerate((o0_ref, o1_ref)):
        for b in range(bsz):
            for j in range(HEAD_PAIRS):
                q = (direction * bsz + b) * HEAD_PAIRS + j
                tile = u_ref[q * LANES:(q + 1) * LANES, :].T
                if direction == 1:
                    tile = _permute_rows(rev_ref[...], tile)
                o_ref[b, :, j * LANES:(j + 1) * LANES] = tile


def _from_scan(y, rev, bsz, nc, nl):
    nctx, nlat = nc // TIME_BLOCK, nl // TIME_BLOCK
    lanes = y.shape[1]
    vrows = B_HD // 2
    bwd = lambda i: jnp.where(i < nctx, nctx - 1 - i, 2 * nctx + nlat - 1 - i)
    out = jax.ShapeDtypeStruct((bsz, nc + nl, B_WIDTH), F32)
    return pl.pallas_call(
        functools.partial(_from_scan_kernel, bsz=bsz),
        out_shape=(out, out),
        grid=(nctx + nlat,),
        in_specs=[pl.BlockSpec((TIME_BLOCK * vrows, lanes), lambda i: (i, 0)),
                  pl.BlockSpec((TIME_BLOCK, TIME_BLOCK), lambda i: (0, 0))],
        out_specs=[pl.BlockSpec((bsz, TIME_BLOCK, B_WIDTH), lambda i: (0, i, 0)),
                   pl.BlockSpec((bsz, TIME_BLOCK, B_WIDTH), lambda i: (0, bwd(i), 0))],
        scratch_shapes=[pltpu.VMEM((2 * bsz * HEAD_PAIRS * LANES, TIME_BLOCK), F32)],
        compiler_params=_cparams(("parallel",)),
        name="from_scan",
    )(y, rev)


def _rwkv_bidirectional(prep_c, prep_l):
    bsz, nc, _ = prep_c['v'].shape
    nl = prep_l['v'].shape[1]
    rev = jnp.asarray(np.eye(TIME_BLOCK, dtype=np.float32)[::-1].copy(), BF16)
    pair = lambda fwd, bwd, values=False: _to_scan(prep_c[fwd], prep_l[fwd], prep_c[bwd], prep_l[bwd], rev, values)
    y = _rwkv_scan(pair('gam0', 'gam1'), pair('b0', 'b1'), pair('kd0', 'kd1'), pair('r0', 'r1'), pair('kk', 'kk'),
                   pair('v', 'v', True))
    return _from_scan(y, rev, bsz, nc, nl)


def _merge_kernel(a_ref, o0_ref, o1_ref, bonus_ref, g_ref, cv_ref, cprev_ref, cnext_ref, ga_ref, gb_ref, gc_ref,
                  x_ref, lng_ref, lnb_ref, cw_ref, wb_ref, wo_ref, ones_ref, g1_ref, n2_ref, sc_ref, sh_ref,
                  x_o, hf_o, hb_o):
    gmean = lambda t: _group_sum(t, ones_ref[...]) * (1.0 / B_HD)

    def group_norm(o):
        dlt = o - gmean(o)
        return dlt * lax.rsqrt(gmean(dlt * dlt) + GN_EPS) * lng_ref[...] + lnb_ref[...]

    yb = (group_norm(o0_ref[0]) + group_norm(o1_ref[0]) + bonus_ref[0]) * g_ref[0]

    w = C_WIDTH
    u = lambda blk: blk[:, w:2 * w] * blk[:, 2 * w:3 * w]
    cv = cv_ref[0].astype(F32)
    i = pl.program_id(1)
    pr = u(cprev_ref[0].astype(F32)[HALO_ROWS - 1:HALO_ROWS])
    nr = u(cnext_ref[0].astype(F32)[0:1])
    pr = jnp.where(i == 0, jnp.zeros_like(pr), pr)
    nr = jnp.where(i == pl.num_programs(1) - 1, jnp.zeros_like(nr), nr)
    yc = cv[:, 0:w] * _dwconv3(u(cv), pr, nr, cw_ref)

    d = D_MODEL
    m = jnp.zeros((cv.shape[0], d), F32)
    for j, (yj, gate_ref) in enumerate(((a_ref[0], ga_ref), (yb, gb_ref), (yc, gc_ref))):
        gate = jax.nn.sigmoid(gate_ref[0].astype(F32))
        m = m + gate * jnp.dot(yj.astype(BF16), wb_ref[j], preferred_element_type=F32)
    ymix = jnp.dot(m.astype(BF16), wo_ref[...], preferred_element_type=F32)
    x1 = x_ref[0] + g1_ref[0] * ymix
    x_o[0] = x1
    ms = jnp.mean(x1 * x1, axis=-1, keepdims=True)
    h = x1 * lax.rsqrt(ms + NORM_EPS) * n2_ref[...] * (1.0 + sc_ref[0]) + sh_ref[0]
    hf_o[0] = h
    hb_o[0] = h.astype(BF16)


def _merge(a, o0, o1, row_off, bonus, g, p, x, lw, g1, sc2, sh2):
    b, n, d = x.shape
    tm = _pick(n, 256)
    w = C_WIDTH
    assert row_off % tm == 0
    tile = lambda width: pl.BlockSpec((1, tm, width), lambda bi, i: (bi, i, 0))
    scan_tile = pl.BlockSpec((1, tm, w), lambda bi, i: (bi, i + row_off // tm, 0))
    const = lambda shape: pl.BlockSpec(shape, lambda bi, i: (0,) * len(shape))
    per_b = pl.BlockSpec((1, 1, d), lambda bi, i: (bi, 0, 0))
    gate = lambda j: pl.BlockSpec((1, tm, d), lambda bi, i, j=j: (bi, i, COL_GATES // d + j))
    in_specs = ([tile(w), scan_tile, scan_tile, tile(w), tile(w)]
                + _halo_specs(tm, 3 * w, COL_CONV // (3 * w), n) + [gate(0), gate(1), gate(2)]
                + [tile(d), const((1, w)), const((1, w)), const((3, w)), const((N_BRANCH, w, d)), const((d, d)),
                   const((w, w)), per_b, const((1, d)), per_b, per_b])
    return pl.pallas_call(
        _merge_kernel,
        out_shape=(jax.ShapeDtypeStruct((b, n, d), F32), jax.ShapeDtypeStruct((b, n, d), F32),
                   jax.ShapeDtypeStruct((b, n, d), BF16)),
        grid=(b, n // tm),
        in_specs=in_specs,
        out_specs=[tile(d)] * 3,
        compiler_params=_cparams(("parallel", "parallel")),
        name="merge_mixers",
    )(a, o0, o1, bonus, g, p, p, p, p, p, p, x, lw['rwkv_ln_g'], lw['rwkv_ln_b'], lw['conv_w'], lw['w_branch_bf16'],
      lw['w_out_bf16'], lw['ones64'], g1, lw['norm2_g'], sc2, sh2)


def _router_kernel(h_ref, rw_ref, o_ref):
    logits = lax.dot_general(rw_ref[...], h_ref[0], (((1,), (1,)), ((), ())), precision=HIGHEST,
                             preferred_element_type=F32)
    z = jnp.exp(logits - jnp.max(logits, axis=0, keepdims=True))
    o_ref[0] = z / jnp.sum(z, axis=0, keepdims=True)


def _router(h, rw_t):
    b, n, d = h.shape
    tm = _pick(n, 512)
    return pl.pallas_call(
        _router_kernel,
        out_shape=jax.ShapeDtypeStruct((b, N_EXPERTS, n), F32),
        grid=(b, n // tm),
        in_specs=[pl.BlockSpec((1, tm, d), lambda bi, i: (bi, i, 0)), pl.BlockSpec((N_EXPERTS, d), lambda bi, i: (0, 0))],
        out_specs=pl.BlockSpec((1, N_EXPERTS, tm), lambda bi, i: (bi, 0, i)),
        compiler_params=_cparams(("parallel", "parallel")),
        name="router_affinity",
    )(h, rw_t)


def _select_kernel(aff_ref, tri_ref, pos_o, cnt_o, *, cap, lb):
    aff = aff_ref[0]
    n = aff.shape[1]
    bits = pltpu.bitcast(aff, jnp.int32)
    count = lambda mask: jnp.sum(mask.astype(F32), axis=1, keepdims=True)

    def refine(i, lo):
        cand = lo | jnp.left_shift(jnp.int32(1), 30 - i)
        return jnp.where(count(bits >= cand) >= cap, cand, lo)

    thr = lax.fori_loop(0, 31, refine, jnp.zeros((N_EXPERTS, 1), jnp.int32))
    gt = bits > thr
    eq = bits == thr
    need = cap - count(gt)
    tri = tri_ref[...]

    def prefix(mask_of_block, write):
        off = jnp.zeros((N_EXPERTS, 1), F32)
        for j in range(n // lb):
            m = mask_of_block(j)
            excl = jnp.dot(m.astype(BF16), tri, preferred_element_type=F32) + off
            write(j, m, excl)
            off = off + jnp.sum(m.astype(F32), axis=1, keepdims=True)

    blk = lambda x, j: x[:, j * lb:(j + 1) * lb]
    sel_blocks = [None] * (n // lb)

    def write_sel(j, m, excl):
        sel_blocks[j] = blk(gt, j) | (m & (excl < need))

    prefix(lambda j: blk(eq, j), write_sel)

    def write_pos(j, m, excl):
        c = excl.astype(jnp.int32)
        cnt_o[0, :, j * lb:(j + 1) * lb] = c
        pos_o[0, :, j * lb:(j + 1) * lb] = jnp.where(m, c, -1)

    prefix(lambda j: sel_blocks[j], write_pos)


def _select(aff, cap):
    b, e, n = aff.shape
    lb = min(n, LANES)
    tri = jnp.asarray(np.triu(np.ones((lb, lb), np.float32), 1), BF16)
    blk = pl.BlockSpec((1, e, n), lambda bi: (bi, 0, 0))
    out = jax.ShapeDtypeStruct((b, e, n), jnp.int32)
    return pl.pallas_call(
        functools.partial(_select_kernel, cap=cap, lb=lb),
        out_shape=(out, out),
        grid=(b,),
        in_specs=[blk, pl.BlockSpec((lb, lb), lambda bi: (0, 0))],
        out_specs=[blk, blk],
        compiler_params=_cparams(("parallel",)),
        name="expert_select",
    )(aff, tri)


GATHER_CHUNK = 128
FFN_ROWS = 512


def _expert_kernel(starts_ref, pos_ref, aff_ref, h_ref, w1_ref, w3_ref, w2_ref, y_o, xin_s, gate_s, w1_s, w3_s, w2_s,
                   *, cap, tile, ntile, tpg, nchunk):
    e, bi, g = pl.program_id(0), pl.program_id(1), pl.program_id(2)

    @pl.when((bi == 0) & (g == 0))
    def _():
        w1_s[...] = w1_ref[0, 0].astype(BF16)
        w3_s[...] = w3_ref[0, 0].astype(BF16)
        w2_s[...] = w2_ref[0, 0].astype(BF16)

    @pl.when(g == 0)
    def _():
        xin_s[...] = jnp.zeros(xin_s.shape, F32)
        gate_s[...] = jnp.zeros(gate_s.shape, F32)

    for tt in range(tpg):
        base = (e * pl.num_programs(1) + bi) * (ntile + 1) + g * tpg + tt
        start, end = starts_ref[base], starts_ref[base + 1]
        w0 = (start // SUBLANES) * SUBLANES
        cols = slice(tt * tile, (tt + 1) * tile)
        for c in range(nchunk):
            r0 = pl.multiple_of(w0 + c * GATHER_CHUNK, SUBLANES)

            @pl.when((r0 < end) & (r0 + GATHER_CHUNK > start))
            def _():
                slot = r0 + lax.broadcasted_iota(jnp.int32, (GATHER_CHUNK, tile), 0)
                onehot = pos_ref[0, 0][:, cols] == slot
                rows = pl.ds(r0, GATHER_CHUNK)
                xin_s[rows, :] += jnp.dot(onehot.astype(BF16), h_ref[0, cols, :], preferred_element_type=F32)
                gate = jnp.sum(jnp.where(onehot, aff_ref[0, 0][:, cols], 0.0), axis=1, keepdims=True)
                gate_s[rows, :] += jnp.broadcast_to(gate, (GATHER_CHUNK, gate_s.shape[1]))

    @pl.when(g == pl.num_programs(2) - 1)
    def _():
        rc = min(cap, FFN_ROWS)
        capp = y_o.shape[2]
        if capp > cap:
            y_o[0, 0, cap:capp, :] = jnp.zeros((capp - cap, y_o.shape[3]), BF16)
        for c in range(cap // rc):
            x = xin_s[c * rc:(c + 1) * rc, :].astype(BF16)
            hid = (_silu(jnp.dot(x, w1_s[...], preferred_element_type=F32))
                   * jnp.dot(x, w3_s[...], preferred_element_type=F32))
            y = jnp.dot(hid.astype(BF16), w2_s[...], preferred_element_type=F32)
            y_o[0, 0, c * rc:(c + 1) * rc, :] = (y * gate_s[c * rc:(c + 1) * rc, 0:1]).astype(BF16)


def _moe_tile(n):
    return _pick(n, 512)


def _experts(starts, pos4, aff4, h_bf16, w1, w3, w2, layer, cap):
    b, n, d = h_bf16.shape
    tile = _moe_tile(n)
    ntile = n // tile
    nchunk = tile // GATHER_CHUNK + 1
    rows = cap + nchunk * GATHER_CHUNK
    capp = -(-cap // SLOT_BLOCK) * SLOT_BLOCK
    f = w1.shape[3]
    tg = _pick(n, 2048)
    tpg = tg // tile
    grid_spec = pltpu.PrefetchScalarGridSpec(
        num_scalar_prefetch=1,
        grid=(N_EXPERTS, b, n // tg),
        in_specs=[pl.BlockSpec((1, 1, 1, tg), lambda e, bi, g, s: (bi, e, 0, g)),
                  pl.BlockSpec((1, 1, 1, tg), lambda e, bi, g, s: (bi, e, 0, g)),
                  pl.BlockSpec((1, tg, d), lambda e, bi, g, s: (bi, g, 0)),
                  pl.BlockSpec((1, 1, d, f), lambda e, bi, g, s: (layer, e, 0, 0), pipeline_mode=pl.Buffered(1)),
                  pl.BlockSpec((1, 1, d, f), lambda e, bi, g, s: (layer, e, 0, 0), pipeline_mode=pl.Buffered(1)),
                  pl.BlockSpec((1, 1, f, d), lambda e, bi, g, s: (layer, e, 0, 0), pipeline_mode=pl.Buffered(1))],
        out_specs=pl.BlockSpec((1, 1, capp, d), lambda e, bi, g, s: (e, bi, 0, 0)),
        scratch_shapes=[pltpu.VMEM((rows, d), F32), pltpu.VMEM((rows, LANES), F32),
                        pltpu.VMEM((d, f), BF16), pltpu.VMEM((d, f), BF16), pltpu.VMEM((f, d), BF16)])
    return pl.pallas_call(
        functools.partial(_expert_kernel, cap=cap, tile=tile, ntile=ntile, tpg=tpg, nchunk=nchunk),
        out_shape=jax.ShapeDtypeStruct((N_EXPERTS, b, capp, d), BF16),
        grid_spec=grid_spec,
        compiler_params=_cparams(("arbitrary", "arbitrary", "arbitrary")),
        name="expert_gather_ffn",
    )(starts, pos4, aff4, h_bf16, w1, w3, w2)


def _combine_kernel(starts_ref, pos_ref, *refs, tile, tpg, ntile, sb, nblk):
    y_refs, (x_ref, g2_ref, o_ref) = refs[:nblk], refs[nblk:]
    bi, gi, e = pl.program_id(0), pl.program_id(1), pl.program_id(2)

    @pl.when(e == 0)
    def _():
        o_ref[...] = jnp.zeros(o_ref.shape, F32)

    base = (e * pl.num_programs(0) + bi) * (ntile + 1) + gi * tpg
    start, end = starts_ref[base], starts_ref[base + tpg]
    k0 = start // sb
    lane = lax.broadcasted_iota(jnp.int32, (1, LANES), 1)
    pos_t = jnp.concatenate(
        [jnp.broadcast_to(pos_ref[0, 0][:, tt * tile:(tt + 1) * tile].astype(F32), (LANES, tile)).T
         for tt in range(tpg)], axis=0)
    for j, y_ref in enumerate(y_refs):
        s0 = (k0 + j) * sb

        @pl.when((s0 < end) & (end > start))
        def _():
            blocks = [pos_t == (s0 + jb * LANES + lane).astype(F32) for jb in range(sb // LANES)]
            onehot_t = jnp.concatenate(blocks, axis=1).astype(BF16)
            o_ref[0] += jnp.dot(onehot_t, y_ref[0, 0], preferred_element_type=F32)

    @pl.when(e == pl.num_programs(2) - 1)
    def _():
        o_ref[0] = x_ref[0] + g2_ref[0] * o_ref[0]


def _combine(starts, pos4, y, x, g2):
    b, n, d = x.shape
    tile = _moe_tile(n)
    ntile = n // tile
    gt = _pick(n, 2048)
    tpg = gt // tile
    sb = SLOT_BLOCK
    nsb = y.shape[2] // sb
    nblk = min(gt // sb + 1, nsb)

    def yblk(j):
        def imap(bi, gi, e, s):
            k0 = s[(e * b + bi) * (ntile + 1) + gi * tpg] // sb
            return (e, bi, jnp.minimum(k0 + j, nsb - 1), 0)
        return pl.BlockSpec((1, 1, sb, d), imap)

    grid_spec = pltpu.PrefetchScalarGridSpec(
        num_scalar_prefetch=1,
        grid=(b, n // gt, N_EXPERTS),
        in_specs=[pl.BlockSpec((1, 1, 1, gt), lambda bi, gi, e, s: (bi, e, 0, gi))]
                 + [yblk(j) for j in range(nblk)]
                 + [pl.BlockSpec((1, gt, d), lambda bi, gi, e, s: (bi, gi, 0)),
                    pl.BlockSpec((1, 1, d), lambda bi, gi, e, s: (bi, 0, 0))],
        out_specs=pl.BlockSpec((1, gt, d), lambda bi, gi, e, s: (bi, gi, 0)),
        scratch_shapes=[])
    return pl.pallas_call(
        functools.partial(_combine_kernel, tile=tile, tpg=tpg, ntile=ntile, sb=sb, nblk=nblk),
        out_shape=jax.ShapeDtypeStruct((b, n, d), F32),
        grid_spec=grid_spec,
        compiler_params=_cparams(("parallel", "parallel", "arbitrary")),
        name="expert_combine",
    )(starts, pos4, *([y] * nblk), x, g2)


def _moe(h_f32, h_bf16, x, g2, lw):
    b, n, _ = x.shape
    cap = CAPACITY_FACTOR * n // N_EXPERTS
    aff = _router(h_f32, lw['router_w_t'])
    pos, cnt = _select(aff, cap)
    tile = _moe_tile(n)
    starts = jnp.concatenate([cnt[:, :, ::tile], jnp.full((b, N_EXPERTS, 1), cap, jnp.int32)], axis=2)
    starts = jnp.transpose(starts, (1, 0, 2)).reshape(-1)
    pos4 = pos.reshape(b, N_EXPERTS, 1, n)
    aff4 = aff.reshape(b, N_EXPERTS, 1, n)
    y = _experts(starts, pos4, aff4, h_bf16, lw['exp_w1'], lw['exp_w3'], lw['exp_w2'], lw['layer'], cap)
    return _combine(starts, pos4, y, x, g2)


def _pad_w_in(w_in):
    d = w_in.shape[0]
    sizes = (A_WIDTH,) * 3 + (B_WIDTH,) * 3 + (B_DECAY_RANK, B_ICL_RANK, B_GATE_RANK) + (C_WIDTH,) * 3 + (N_BRANCH * D_MODEL,)
    cuts = np.cumsum((0,) + sizes)
    piece = lambda lo, hi: w_in[:, int(cuts[lo]):int(cuts[hi])]
    zeros = jnp.zeros((d, COL_GATES - COL_LOWRANK - 256), w_in.dtype)
    out = jnp.concatenate([piece(0, 3), piece(3, 6), piece(9, 12), piece(6, 9), zeros, piece(12, 13)], axis=1)
    assert out.shape[1] == N_IN_PAD
    return out.astype(BF16)


def _rows_at(w, lo, total=256):
    pad = [(0, 0)] * w.ndim
    pad[-2] = (lo, total - lo - w.shape[-2])
    return jnp.pad(w, pad)


def _rope_tables(rows):
    half = A_HD // 2
    inv = jnp.power(ROPE_THETA, -jnp.arange(0, half, 2, dtype=F32) / half)
    r = jnp.repeat(jnp.arange(rows, dtype=F32), GRID_W)
    col = jnp.tile(jnp.arange(GRID_W, dtype=F32), rows)
    ang = jnp.concatenate([r[:, None] * inv, col[:, None] * inv], axis=-1)
    cos, sin = jnp.cos(ang), jnp.sin(ang)
    reps = A_WIDTH // A_HD
    cos_t = jnp.tile(jnp.concatenate([cos, cos], axis=-1), (1, reps))
    sin_t = jnp.tile(jnp.concatenate([-sin, sin], axis=-1), (1, reps))
    return cos_t, sin_t


def _layer_weights(i, norm2_g, w_in, q_norm_g, k_norm_g, diff_lambda, diff_subln_g, rwkv_conv_w, rwkv_w0, rwkv_w_up,
                   rwkv_a0, rwkv_a_up, rwkv_g_up, rwkv_k_k, rwkv_k_a, rwkv_r_k, rwkv_ln_g, rwkv_ln_b, conv_w,
                   w_branch, w_out, router_w, exp_w1, exp_w3, exp_w2):
    row = lambda v: v.reshape(1, -1)
    return {
        'norm2_g': row(norm2_g[i]),
        'w_in_pad': _pad_w_in(w_in[i]),
        'q_g': row(jnp.tile(q_norm_g[i], A_WIDTH // A_HD)), 'k_g': row(jnp.tile(k_norm_g[i], A_WIDTH // A_HD)),
        'diff_lambda': diff_lambda[i], 'subln_g': row(diff_subln_g[i]),
        'rwkv_conv_w': rwkv_conv_w[i], 'rwkv_w0': rwkv_w0[i], 'rwkv_a0': rwkv_a0[i],
        'w_up_pad': _rows_at(rwkv_w_up[i], 0), 'a_up_pad': _rows_at(rwkv_a_up[i], B_DECAY_RANK),
        'g_up_pad': _rows_at(rwkv_g_up[i], B_DECAY_RANK + B_ICL_RANK),
        'rwkv_k_k': row(rwkv_k_k[i]), 'rwkv_k_a': row(rwkv_k_a[i]), 'rwkv_r_k': row(rwkv_r_k[i]),
        'rwkv_ln_g': row(rwkv_ln_g[i]), 'rwkv_ln_b': row(rwkv_ln_b[i]),
        'conv_w': conv_w[i], 'w_branch_bf16': w_branch[i].astype(BF16), 'w_out_bf16': w_out[i].astype(BF16),
        'router_w_t': router_w[i].T,
        'exp_w1': exp_w1, 'exp_w3': exp_w3, 'exp_w2': exp_w2, 'layer': i,
        'ones64': _group_ones(B_WIDTH, B_HD).astype(BF16),
    }


def _mixer_inputs(x, norm1_g, sc1, sh1, lw, cos_t, sin_t):
    p = _norm_mod_proj(x, norm1_g, sc1, sh1, lw['w_in_pad'])
    q, k = _qkv_prep(p, lw['q_g'], lw['k_g'], lw['ones64'], cos_t, sin_t)
    prep = dict(zip(PREP_NAMES, _rwkv_prep(p, lw)))
    return p, (q, k), prep


def kernel(x, c, ctx, c_ctx, norm1_g, norm2_g, ada_w, ada_b, w_in, q_norm_g, k_norm_g, diff_lambda, diff_subln_g,
           rwkv_conv_w, rwkv_w0, rwkv_w_up, rwkv_a0, rwkv_a_up, rwkv_g_up, rwkv_k_k, rwkv_k_a, rwkv_r_k, rwkv_ln_g,
           rwkv_ln_b, conv_w, w_branch, w_out, router_w, exp_w1, exp_w3, exp_w2):
    bsz, n_lat, d = x.shape
    depth = ada_w.shape[0]
    cos_t, sin_t = _rope_tables(n_lat // GRID_W)

    cond = jnp.concatenate([c, jnp.broadcast_to(c_ctx[None, :], (SUBLANES - bsz % SUBLANES, d))], axis=0)
    ada = _ada_params(cond, ada_w, ada_b)

    xc, xl = ctx, x
    for i in range(depth):
        lam_init = 0.8 - 0.6 * math.exp(-0.3 * i)
        need_ctx = i != depth - 1
        lw = _layer_weights(i, norm2_g, w_in, q_norm_g, k_norm_g, diff_lambda, diff_subln_g, rwkv_conv_w, rwkv_w0,
                            rwkv_w_up, rwkv_a0, rwkv_a_up, rwkv_g_up, rwkv_k_k, rwkv_k_a, rwkv_r_k, rwkv_ln_g,
                            rwkv_ln_b, conv_w, w_branch, w_out, router_w, exp_w1, exp_w3, exp_w2)
        mods = ada[i].reshape(-1, 6, d)
        lat = [mods[:bsz, j][:, None, :] for j in range(6)]
        cxt = [jnp.broadcast_to(mods[bsz, j][None, None, :], (bsz, 1, d)) for j in range(6)]

        pc, (qc, kc), prep_c = _mixer_inputs(xc, norm1_g[i], cxt[1], cxt[0], lw, None, None)
        pl_, (ql, kl), prep_l = _mixer_inputs(xl, norm1_g[i], lat[1], lat[0], lw, cos_t, sin_t)

        v_cols = slice(COL_ATTN + 2 * A_WIDTH, COL_ATTN + 3 * A_WIDTH)
        k_all = jnp.concatenate([kc, kl], axis=1)
        v_all = jnp.concatenate([pc[:, :, v_cols], pl_[:, :, v_cols]], axis=1)
        al = _flash_attention(lw['diff_lambda'], lw['subln_g'], ql, k_all, v_all, lam_init)
        o0, o1 = _rwkv_bidirectional(prep_c, prep_l)
        n_ctx = xc.shape[1]

        xl, hl, hlb = _merge(al, o0, o1, n_ctx, prep_l['bonus'], prep_l['g'], pl_, xl, lw, lat[2], lat[4], lat[3])
        xl = _moe(hl, hlb, xl, lat[5], lw)
        if need_ctx:
            ac = _ctx_attention(lw['diff_lambda'], lw['subln_g'], qc, kc, pc, lam_init)
            xc, hc, hcb = _merge(ac, o0, o1, 0, prep_c['bonus'], prep_c['g'], pc, xc, lw, cxt[2], cxt[4], cxt[3])
            xc = _moe(hc, hcb, xc, cxt[5], lw)
    return xl
```
